```python
import math
import jax, jax.numpy as jnp
from jax import lax
import numpy as np

D_MODEL = 1024
BATCH = 16
SEQ = 2048
DEPTH = 1
DEC_BATCH = 32
DEC_SEQ = 32
PAST_LEN = 4096

CHUNK = 64
Q_BLOCK = 128
DIFF_HEADS = 4
HEAD_DIM = 64
DIFF_WIDTH = DIFF_HEADS * 2 * HEAD_DIM
GMLP_GROUPS = 4
GMLP_CHUNK = 128
GMLP_GROUP_DIM = 128
GMLP_WIDTH = GMLP_GROUPS * GMLP_GROUP_DIM
MIX_WIDTH = DIFF_WIDTH + GMLP_WIDTH
IN_COLS = 3 * DIFF_WIDTH + 2 * GMLP_WIDTH
D_FF = 2816
N_MEM = 256
MEM_HEADS = 4
MEM_HEAD_DIM = D_MODEL // MEM_HEADS
LN_EPS = 1e-5
ALPHA = (2 * DEPTH) ** 0.25
BETA = (8 * DEPTH) ** -0.25

kernel_name = "hybrid_diffattn_gmlp_macaron_deepnorm_stream_step"

F32 = jnp.float32


def layer_norm(h, g, b):
    h32 = h.astype(F32)
    mu = jnp.mean(h32, axis=-1, keepdims=True)
    var = jnp.mean(jnp.square(h32 - mu), axis=-1, keepdims=True)
    return ((h32 - mu) * lax.rsqrt(var + LN_EPS) * g.astype(F32) + b.astype(F32)).astype(h.dtype)


def post_norm(x, h, g, b):
    return layer_norm(ALPHA * x + h, g, b)


def swiglu_half(x, w_gu, w_down):
    gate, up = jnp.split(x @ w_gu, 2, axis=-1)
    return 0.5 * ((jax.nn.silu(gate) * up) @ w_down)


def lambda_init_for(layer_idx):
    return 0.8 - 0.6 * math.exp(-0.3 * layer_idx)


def diff_lambda(lq1, lk1, lq2, lk2, lam_init):
    return (jnp.exp(jnp.sum(lq1.astype(F32) * lk1.astype(F32)))
            - jnp.exp(jnp.sum(lq2.astype(F32) * lk2.astype(F32))) + lam_init)


def alibi_slopes():
    return jnp.exp2(-8.0 * jnp.arange(1, DIFF_HEADS + 1, dtype=F32) / DIFF_HEADS)


def diff_attention(q, k, v, q_start, lam):
    B, T = q.shape[0], q.shape[1]
    S = k.shape[1]
    qb = min(Q_BLOCK, T)
    nb = T // qb
    k_pos = jnp.arange(S, dtype=jnp.int32)
    slopes = alibi_slopes()
    scale = HEAD_DIM ** -0.5

    def block(args):
        q_blk, start = args
        q_pos = start + jnp.arange(qb, dtype=jnp.int32)
        s = jnp.einsum('bqhmd,bkhmd->bhmqk', q_blk, k).astype(F32) * scale
        dist = jnp.abs(q_pos[:, None] - k_pos[None, :]).astype(F32)
        allowed = (k_pos[None, :] // CHUNK) <= (q_pos[:, None] // CHUNK)
        s = s - slopes[None, :, None, None, None] * dist[None, None, None]
        s = jnp.where(allowed, s, -jnp.inf)
        p = jax.nn.softmax(s, axis=-1)
        a = p[:, :, 0] - lam * p[:, :, 1]
        return jnp.einsum('bhqk,bkhe->bqhe', a.astype(v.dtype), v)

    q_blocks = jnp.moveaxis(q.reshape(B, nb, qb, DIFF_HEADS, 2, HEAD_DIM), 1, 0)
    starts = q_start + qb * jnp.arange(nb, dtype=jnp.int32)
    out = lax.map(block, (q_blocks, starts))
    return jnp.moveaxis(out, 0, 1).reshape(B, T, DIFF_HEADS, 2 * HEAD_DIM)


def spatial_gating(u, v, ln_g, ln_b, ws, bs):
    B, T, _ = u.shape
    n = min(T, GMLP_CHUNK)
    nc = T // n
    v = layer_norm(v, ln_g, ln_b)
    vg = v.reshape(B, nc, n, GMLP_GROUPS, GMLP_GROUP_DIM)
    w = ws[:, :n, :n] * jnp.tril(jnp.ones((n, n), ws.dtype))
    mixed = jnp.einsum('gts,bcsgd->bctgd', w, vg) + bs[:, :n].T[None, None, :, :, None]
    out = u * mixed.reshape(B, T, GMLP_WIDTH)
    return out, vg.reshape(B, T, GMLP_GROUPS, GMLP_GROUP_DIM)


def cross_attention(x, mem_k, mem_v, wq, wo):
    B, T, _ = x.shape
    q = (x @ wq).reshape(B, T, MEM_HEADS, MEM_HEAD_DIM)
    s = jnp.einsum('bthd,bmhd->bhtm', q, mem_k).astype(F32) * (MEM_HEAD_DIM ** -0.5)
    p = jax.nn.softmax(s, axis=-1)
    o = jnp.einsum('bhtm,bmhd->bthd', p.astype(mem_v.dtype), mem_v).reshape(B, T, D_MODEL)
    return o @ wo


def trunk_layer(x, k_past, v_past, mem_k, mem_v, layer_idx, p):
    B, T, _ = x.shape
    x = post_norm(x, swiglu_half(x, p['ffn1_w_gu'], p['ffn1_w_down']), p['ln1_g'], p['ln1_b'])

    z = x @ p['w_in']
    q, k, v, gu, gv = jnp.split(
        z, [DIFF_WIDTH, 2 * DIFF_WIDTH, 3 * DIFF_WIDTH, 3 * DIFF_WIDTH + GMLP_WIDTH], axis=-1)
    q = q.reshape(B, T, DIFF_HEADS, 2, HEAD_DIM)
    k = k.reshape(B, T, DIFF_HEADS, 2, HEAD_DIM)
    v = v.reshape(B, T, DIFF_HEADS, 2 * HEAD_DIM)
    if k_past is None:
        q_start, k_all, v_all = 0, k, v
    else:
        q_start = k_past.shape[1]
        k_all = jnp.concatenate([k_past, k], axis=1)
        v_all = jnp.concatenate([v_past, v], axis=1)

    lam_init = lambda_init_for(layer_idx)
    lam = diff_lambda(p['lambda_q1'], p['lambda_k1'], p['lambda_q2'], p['lambda_k2'], lam_init)
    a = diff_attention(q, k_all, v_all, q_start, lam)
    a32 = a.astype(F32)
    a32 = a32 * lax.rsqrt(jnp.mean(jnp.square(a32), axis=-1, keepdims=True) + LN_EPS)
    a = (a32 * p['subln_g'].astype(F32) * (1.0 - lam_init)).astype(x.dtype).reshape(B, T, DIFF_WIDTH)

    g_out, v_rows = spatial_gating(jax.nn.gelu(gu), jax.nn.gelu(gv), p['gmlp_ln_g'], p['gmlp_ln_b'],
                                   p['gmlp_ws'], p['gmlp_bs'])

    mix = jnp.concatenate([a, g_out], axis=-1) @ p['w_out']
    x = post_norm(x, mix, p['ln2_g'], p['ln2_b'])
    x = post_norm(x, cross_attention(x, mem_k, mem_v, p['cross_wq'], p['cross_wo']), p['ln3_g'], p['ln3_b'])
    x = post_norm(x, swiglu_half(x, p['ffn2_w_gu'], p['ffn2_w_down']), p['ln4_g'], p['ln4_b'])
    return x, k, v, v_rows


def setup_inputs(seed: int = 0) -> dict:
    key = jax.random.key(seed)
    ks = iter(jax.random.split(key, 48))

    def nrm(shape, scale):
        return jax.random.normal(next(ks), shape, F32) * scale

    def gain(shape):
        return 1.0 + nrm(shape, 0.01)

    L = DEPTH
    return {
        "x_prompt": nrm((BATCH, SEQ, D_MODEL), 1.0),
        "x_sample": nrm((DEC_BATCH, DEC_SEQ, D_MODEL), 1.0),
        "cache_k": nrm((L, DEC_BATCH, PAST_LEN, DIFF_HEADS, 2, HEAD_DIM), 1.0),
        "cache_v": nrm((L, DEC_BATCH, PAST_LEN, DIFF_HEADS, 2 * HEAD_DIM), 1.0),
        "cache_mem_k": nrm((L, DEC_BATCH, N_MEM, MEM_HEADS, MEM_HEAD_DIM), 1.0),
        "cache_mem_v": nrm((L, DEC_BATCH, N_MEM, MEM_HEADS, MEM_HEAD_DIM), 1.0),
        "mem_prompt": nrm((BATCH, N_MEM, D_MODEL), 1.0),
        "ffn1_w_gu": nrm((L, D_MODEL, 2 * D_FF), D_MODEL ** -0.5),
        "ffn1_w_down": nrm((L, D_FF, D_MODEL), BETA * D_FF ** -0.5),
        "ln1_g": gain((L, D_MODEL)),
        "ln1_b": nrm((L, D_MODEL), 0.01),
        "w_in": nrm((L, D_MODEL, IN_COLS), D_MODEL ** -0.5),
        "lambda_q1": nrm((L, HEAD_DIM), 0.1),
        "lambda_k1": nrm((L, HEAD_DIM), 0.1),
        "lambda_q2": nrm((L, HEAD_DIM), 0.1),
        "lambda_k2": nrm((L, HEAD_DIM), 0.1),
        "subln_g": gain((L, 2 * HEAD_DIM)),
        "gmlp_ln_g": gain((L, GMLP_WIDTH)),
        "gmlp_ln_b": nrm((L, GMLP_WIDTH), 0.01),
        "gmlp_ws": nrm((L, GMLP_GROUPS, GMLP_CHUNK, GMLP_CHUNK), 0.05),
        "gmlp_bs": gain((L, GMLP_GROUPS, GMLP_CHUNK)),
        "w_out": nrm((L, MIX_WIDTH, D_MODEL), BETA * MIX_WIDTH ** -0.5),
        "ln2_g": gain((L, D_MODEL)),
        "ln2_b": nrm((L, D_MODEL), 0.01),
        "cross_wq": nrm((L, D_MODEL, D_MODEL), D_MODEL ** -0.5),
        "cross_wk": nrm((L, D_MODEL, D_MODEL), D_MODEL ** -0.5),
        "cross_wv": nrm((L, D_MODEL, D_MODEL), D_MODEL ** -0.5),
        "cross_wo": nrm((L, D_MODEL, D_MODEL), BETA * D_MODEL ** -0.5),
        "ln3_g": gain((L, D_MODEL)),
        "ln3_b": nrm((L, D_MODEL), 0.01),
        "ffn2_w_gu": nrm((L, D_MODEL, 2 * D_FF), D_MODEL ** -0.5),
        "ffn2_w_down": nrm((L, D_FF, D_MODEL), BETA * D_FF ** -0.5),
        "ln4_g": gain((L, D_MODEL)),
        "ln4_b": nrm((L, D_MODEL), 0.01),
    }


def reference(x_prompt, x_sample, cache_k, cache_v, cache_mem_k, cache_mem_v, mem_prompt,
              ffn1_w_gu, ffn1_w_down, ln1_g, ln1_b, w_in, lambda_q1, lambda_k1, lambda_q2, lambda_k2,
              subln_g, gmlp_ln_g, gmlp_ln_b, gmlp_ws, gmlp_bs, w_out, ln2_g, ln2_b,
              cross_wq, cross_wk, cross_wv, cross_wo, ln3_g, ln3_b,
              ffn2_w_gu, ffn2_w_down, ln4_g, ln4_b):
    xp, xs = x_prompt, x_sample
    Bp = mem_prompt.shape[0]
    kp_l, vp_l, mkp_l, mvp_l, ks_l, vs_l, gvs_l = [], [], [], [], [], [], []
    for l in range(DEPTH):
        p = {
            'ffn1_w_gu': ffn1_w_gu[l], 'ffn1_w_down': ffn1_w_down[l], 'ln1_g': ln1_g[l], 'ln1_b': ln1_b[l],
            'w_in': w_in[l], 'lambda_q1': lambda_q1[l], 'lambda_k1': lambda_k1[l],
            'lambda_q2': lambda_q2[l], 'lambda_k2': lambda_k2[l], 'subln_g': subln_g[l],
            'gmlp_ln_g': gmlp_ln_g[l], 'gmlp_ln_b': gmlp_ln_b[l], 'gmlp_ws': gmlp_ws[l], 'gmlp_bs': gmlp_bs[l],
            'w_out': w_out[l], 'ln2_g': ln2_g[l], 'ln2_b': ln2_b[l],
            'cross_wq': cross_wq[l], 'cross_wo': cross_wo[l], 'ln3_g': ln3_g[l], 'ln3_b': ln3_b[l],
            'ffn2_w_gu': ffn2_w_gu[l], 'ffn2_w_down': ffn2_w_down[l], 'ln4_g': ln4_g[l], 'ln4_b': ln4_b[l],
        }
        mem_k = (mem_prompt @ cross_wk[l]).reshape(Bp, N_MEM, MEM_HEADS, MEM_HEAD_DIM)
        mem_v = (mem_prompt @ cross_wv[l]).reshape(Bp, N_MEM, MEM_HEADS, MEM_HEAD_DIM)
        xp, kp, vp, _ = trunk_layer(xp, None, None, mem_k, mem_v, l, p)
        xs, ks_new, vs_new, gv_new = trunk_layer(xs, cache_k[l], cache_v[l], cache_mem_k[l], cache_mem_v[l], l, p)
        kp_l.append(kp); vp_l.append(vp); mkp_l.append(mem_k); mvp_l.append(mem_v)
        ks_l.append(ks_new); vs_l.append(vs_new); gvs_l.append(gv_new)
    new_k_prompt = jnp.stack(kp_l)
    new_v_prompt = jnp.stack(vp_l)
    new_mem_k_prompt = jnp.stack(mkp_l)
    new_mem_v_prompt = jnp.stack(mvp_l)
    new_k_sample = jnp.stack(ks_l)
    new_v_sample = jnp.stack(vs_l)
    new_gmlp_v_sample = jnp.stack(gvs_l)
    return (xp, xs, new_k_prompt, new_v_prompt, new_mem_k_prompt, new_mem_v_prompt,
            new_k_sample, new_v_sample, new_gmlp_v_sample)
```

```python
import functools
import math

import jax
import jax.numpy as jnp
from jax import lax
from jax.experimental import pallas as pl
from jax.experimental.pallas import tpu as pltpu

F32 = jnp.float32
BF16 = jnp.bfloat16

CHUNK = 64
DIFF_HEADS = 4
HEAD_DIM = 64
HEAD_WIDTH = 2 * HEAD_DIM
DIFF_WIDTH = DIFF_HEADS * HEAD_WIDTH
GMLP_GROUPS = 4
GMLP_CHUNK = 128
GMLP_GROUP_DIM = 128
GMLP_WIDTH = GMLP_GROUPS * GMLP_GROUP_DIM
MEM_HEADS = 4
LN_EPS = 1e-5
MASKED = -1e30

ROW_TILE = 512
FF_CHUNK = 256
ATTN_Q_TILE = 256
CROSS_Q_TILE = 512
VMEM_LIMIT_BYTES = 56 * 1024 * 1024


def _params(*semantics):
    return pltpu.CompilerParams(dimension_semantics=semantics, vmem_limit_bytes=VMEM_LIMIT_BYTES)


def _resident(shape):
    return pl.BlockSpec(shape, lambda *_: (0,) * len(shape), pipeline_mode=pl.Buffered(1))


def _rows(tile, width):
    return pl.BlockSpec((tile, width), lambda i: (i, 0))


def _layer_norm(h, g, b):
    mu = jnp.mean(h, axis=-1, keepdims=True)
    d = h - mu
    var = jnp.mean(d * d, axis=-1, keepdims=True)
    return d * lax.rsqrt(var + LN_EPS) * g + b


def _dot(a, b):
    return jnp.dot(a, b, preferred_element_type=F32)


def _dot_nt(a, b):
    return lax.dot_general(a, b, (((1,), (1,)), ((), ())), preferred_element_type=F32)


def _ffn_kernel(*refs, alpha, d_ff, with_proj):
    if with_proj:
        (o_ref, xin_ref, wo_ref, lpg_ref, lpb_ref,
         wgu_ref, wd_ref, g_ref, b_ref, out_ref, act_ref) = refs
        x = _layer_norm(alpha * xin_ref[...] + _dot(o_ref[...], wo_ref[...]), lpg_ref[...], lpb_ref[...])
    else:
        xin_ref, wgu_ref, wd_ref, g_ref, b_ref, out_ref, act_ref = refs
        x = xin_ref[...]
    xb = x.astype(BF16)
    for c in range(d_ff // FF_CHUNK):
        lo = c * FF_CHUNK
        gate = _dot(xb, wgu_ref[:, lo:lo + FF_CHUNK])
        up = _dot(xb, wgu_ref[:, d_ff + lo:d_ff + lo + FF_CHUNK])
        act_ref[:, lo:lo + FF_CHUNK] = (gate * jax.nn.sigmoid(gate) * up).astype(BF16)
    y = _dot(act_ref[...], wd_ref[...])
    out_ref[...] = _layer_norm(alpha * x + 0.5 * y, g_ref[...], b_ref[...])


def _ffn(x, w_gu, w_down, ln_g, ln_b, alpha, proj=None):
    n, d = x.shape
    d_ff = w_down.shape[0]
    tile = min(ROW_TILE, n)
    in_specs, args = [], []
    if proj is not None:
        o, wo, pg, pb = proj
        in_specs += [_rows(tile, o.shape[1])]
        args += [o]
    in_specs += [_rows(tile, d)]
    args += [x]
    if proj is not None:
        in_specs += [_resident(wo.shape), _resident((1, d)), _resident((1, d))]
        args += [wo, pg, pb]
    in_specs += [_resident(w_gu.shape), _resident(w_down.shape), _resident((1, d)), _resident((1, d))]
    args += [w_gu, w_down, ln_g, ln_b]
    return pl.pallas_call(
        functools.partial(_ffn_kernel, alpha=alpha, d_ff=d_ff, with_proj=proj is not None),
        grid=(n // tile,),
        in_specs=in_specs,
        out_specs=_rows(tile, d),
        out_shape=jax.ShapeDtypeStruct((n, d), F32),
        scratch_shapes=[pltpu.VMEM((tile, d_ff), BF16)],
        compiler_params=_params("parallel"),
        name="ffn_proj" if proj is not None else "ffn",
    )(*args)


def _inproj_kernel(x_ref, w_ref, lng_ref, lnb_ref, ws_ref, bst_ref, q_ref, k_ref, v_ref, g_ref, *vrows_ref):
    tile = x_ref.shape[0]
    z = _dot(x_ref[...].astype(BF16), w_ref[...])
    q_ref[...] = z[:, :DIFF_WIDTH].astype(BF16)
    k_ref[...] = z[:, DIFF_WIDTH:2 * DIFF_WIDTH]
    v_ref[...] = z[:, 2 * DIFF_WIDTH:3 * DIFF_WIDTH]
    u = jax.nn.gelu(z[:, 3 * DIFF_WIDTH:3 * DIFF_WIDTH + GMLP_WIDTH])
    gv = jax.nn.gelu(z[:, 3 * DIFF_WIDTH + GMLP_WIDTH:])
    vn = _layer_norm(gv, lng_ref[...], lnb_ref[...])
    if vrows_ref:
        vrows_ref[0][...] = vn
    vnb = vn.astype(BF16)
    row = lax.broadcasted_iota(jnp.int32, (GMLP_CHUNK, GMLP_CHUNK), 0)
    col = lax.broadcasted_iota(jnp.int32, (GMLP_CHUNK, GMLP_CHUNK), 1)
    for g in range(GMLP_GROUPS):
        cols = slice(g * GMLP_GROUP_DIM, (g + 1) * GMLP_GROUP_DIM)
        w = jnp.where(row >= col, ws_ref[g], 0.0).astype(BF16)
        bias = bst_ref[:, g:g + 1]
        for c in range(tile // GMLP_CHUNK):
            rows = slice(c * GMLP_CHUNK, (c + 1) * GMLP_CHUNK)
            mixed = _dot(w, vnb[rows, cols]) + bias
            g_ref[rows, cols] = (u[rows, cols] * mixed).astype(BF16)


def _inproj(x, w_in, ln_g, ln_b, ws, bs_t, want_vrows):
    n, d = x.shape
    tile = min(ROW_TILE, n)
    out_shape = [jax.ShapeDtypeStruct((n, DIFF_WIDTH), BF16),
                 jax.ShapeDtypeStruct((n, DIFF_WIDTH), F32),
                 jax.ShapeDtypeStruct((n, DIFF_WIDTH), F32),
                 jax.ShapeDtypeStruct((n, GMLP_WIDTH), BF16)]
    out_specs = [_rows(tile, DIFF_WIDTH)] * 3 + [_rows(tile, GMLP_WIDTH)]
    if want_vrows:
        out_shape.append(jax.ShapeDtypeStruct((n, GMLP_WIDTH), F32))
        out_specs.append(_rows(tile, GMLP_WIDTH))
    return pl.pallas_call(
        _inproj_kernel,
        grid=(n // tile,),
        in_specs=[_rows(tile, d), _resident(w_in.shape), _resident((1, GMLP_WIDTH)), _resident((1, GMLP_WIDTH)),
                  _resident(ws.shape), _resident(bs_t.shape)],
        out_specs=out_specs,
        out_shape=out_shape,
        compiler_params=_params("parallel"),
        name="inproj",
    )(x, w_in, ln_g, ln_b, ws, bs_t)


def _head_scalars(head, lq1_ref, lk1_ref, lq2_ref, lk2_ref, lam_init):
    slope = jnp.exp2(jnp.zeros((1, 1), F32) - 8.0 * (head + 1).astype(F32) / DIFF_HEADS)
    lam = (jnp.exp(jnp.sum(lq1_ref[...] * lk1_ref[...], axis=-1, keepdims=True))
           - jnp.exp(jnp.sum(lq2_ref[...] * lk2_ref[...], axis=-1, keepdims=True)) + lam_init)
    return slope, lam


def _stack_maps(q):
    lane = lax.broadcasted_iota(jnp.int32, q.shape, 1)
    zero = jnp.zeros_like(q)
    return jnp.concatenate([jnp.where(lane < HEAD_DIM, q, zero), jnp.where(lane >= HEAD_DIM, q, zero)], axis=0)


def _near_bias(slope, t, base_q, base_k, nk):
    r = lax.broadcasted_iota(jnp.int32, (t, nk), 0) + base_q
    c = lax.broadcasted_iota(jnp.int32, (t, nk), 1) + base_k
    bias = slope * (r - jnp.abs(r - c)).astype(F32)
    return jnp.where(c // CHUNK <= r // CHUNK, bias, MASKED)


def _diff_combine(parts, values, t, lam):
    m = functools.reduce(jnp.maximum, [jnp.max(s, axis=-1, keepdims=True) for s in parts])
    ps = [jnp.exp(s - m) for s in parts]
    denom = functools.reduce(jnp.add, [jnp.sum(p, axis=-1, keepdims=True) for p in ps])
    c0 = 1.0 / denom[:t]
    c1 = lam / denom[t:]
    out = None
    for p, v in zip(ps, values):
        a = (p[:t] * c0 - p[t:] * c1).astype(BF16)
        o = _dot(a, v)
        out = o if out is None else out + o
    return out


def _sub_norm(o, g, lam_init):
    return o * lax.rsqrt(jnp.mean(o * o, axis=-1, keepdims=True) + LN_EPS) * g * (1.0 - lam_init)


def _attn_prompt_kernel(q_ref, k_ref, v_ref, lq1_ref, lk1_ref, lq2_ref, lk2_ref, sg_ref, o_ref,
                        kb_ref, vb_ref, *, lam_init):
    seq = q_ref.shape[0]
    tq = ATTN_Q_TILE
    slope, lam = _head_scalars(pl.program_id(1), lq1_ref, lk1_ref, lq2_ref, lk2_ref, lam_init)
    kb_ref[...] = k_ref[...].astype(BF16)
    vb_ref[...] = v_ref[...].astype(BF16)
    diag = _near_bias(slope, tq, 0, 0, tq)
    diag = jnp.concatenate([diag, diag], axis=0)
    scale = HEAD_DIM ** -0.5
    for i in range(seq // tq):
        lo = i * tq
        qs = _stack_maps(q_ref[lo:lo + tq, :] * scale)
        parts = [_dot_nt(qs, kb_ref[lo:lo + tq, :]) + (diag + slope * float(lo))]
        values = [vb_ref[lo:lo + tq, :]]
        if lo:
            far = slope * lax.broadcasted_iota(jnp.int32, (1, lo), 1).astype(F32)
            parts.append(_dot_nt(qs, kb_ref[0:lo, :]) + far)
            values.append(vb_ref[0:lo, :])
        out = _diff_combine(parts, values, tq, lam)
        o_ref[lo:lo + tq, :] = _sub_norm(out, sg_ref[...], lam_init).astype(BF16)


def _attn_prompt(q, k, v, lam_vecs, subln_g, batch, seq, lam_init):
    blk = pl.BlockSpec((seq, HEAD_WIDTH), lambda b, h: (b, h))
    vec = pl.BlockSpec((1, HEAD_DIM), lambda b, h: (0, 0))
    return pl.pallas_call(
        functools.partial(_attn_prompt_kernel, lam_init=lam_init),
        grid=(batch, DIFF_HEADS),
        in_specs=[blk, blk, blk, vec, vec, vec, vec, pl.BlockSpec((1, HEAD_WIDTH), lambda b, h: (0, 0))],
        out_specs=blk,
        out_shape=jax.ShapeDtypeStruct((batch * seq, DIFF_WIDTH), BF16),
        scratch_shapes=[pltpu.VMEM((seq, HEAD_WIDTH), BF16), pltpu.VMEM((seq, HEAD_WIDTH), BF16)],
        compiler_params=_params("parallel", "parallel"),
        name="attn_prompt",
    )(q, k, v, *lam_vecs, subln_g)


def _attn_sample_kernel(q_ref, kn_ref, vn_ref, kc_ref, vc_ref, lq1_ref, lk1_ref, lq2_ref, lk2_ref, sg_ref, o_ref,
                        *, lam_init):
    t = q_ref.shape[0]
    past = kc_ref.shape[0]
    scale = HEAD_DIM ** -0.5
    far_pos = lax.broadcasted_iota(jnp.int32, (1, past), 1).astype(F32)
    for h in range(DIFF_HEADS):
        cols = slice(h * HEAD_WIDTH, (h + 1) * HEAD_WIDTH)
        slope, lam = _head_scalars(jnp.int32(h), lq1_ref, lk1_ref, lq2_ref, lk2_ref, lam_init)
        qs = _stack_maps(q_ref[:, cols] * scale)
        near = _near_bias(slope, t, past, past, t)
        parts = [_dot_nt(qs, kn_ref[:, cols].astype(BF16)) + jnp.concatenate([near, near], axis=0),
                 _dot_nt(qs, kc_ref[:, cols].astype(BF16)) + slope * far_pos]
        values = [vn_ref[:, cols].astype(BF16), vc_ref[:, cols].astype(BF16)]
        out = _diff_combine(parts, values, t, lam)
        o_ref[:, cols] = _sub_norm(out, sg_ref[...], lam_init).astype(BF16)


def _attn_sample(q, k_new, v_new, k_cache, v_cache, lam_vecs, subln_g, batch, seq, lam_init):
    past = k_cache.shape[1]
    new = pl.BlockSpec((seq, DIFF_WIDTH), lambda b: (b, 0))
    cache = pl.BlockSpec((None, past, DIFF_WIDTH), lambda b: (b, 0, 0))
    vec = pl.BlockSpec((1, HEAD_DIM), lambda b: (0, 0))
    return pl.pallas_call(
        functools.partial(_attn_sample_kernel, lam_init=lam_init),
        grid=(batch,),
        in_specs=[new, new, new, cache, cache, vec, vec, vec, vec, pl.BlockSpec((1, HEAD_WIDTH), lambda b: (0, 0))],
        out_specs=new,
        out_shape=jax.ShapeDtypeStruct((batch * seq, DIFF_WIDTH), BF16),
        compiler_params=_params("parallel"),
        name="attn_sample",
    )(q, k_new, v_new, k_cache, v_cache, *lam_vecs, subln_g)


def _outproj_kernel(a_ref, g_ref, x_ref, wo_ref, lg_ref, lb_ref, wq_ref, x2_ref, q2_ref, *, alpha):
    half = a_ref.shape[1]
    mix = _dot(a_ref[...], wo_ref[:half, :]) + _dot(g_ref[...], wo_ref[half:, :])
    x2 = _layer_norm(alpha * x_ref[...] + mix, lg_ref[...], lb_ref[...])
    x2_ref[...] = x2
    q2_ref[...] = _dot(x2.astype(BF16), wq_ref[...]).astype(BF16)


def _outproj(a, g, x, w_out, ln_g, ln_b, wq, alpha):
    n, d = x.shape
    tile = min(ROW_TILE, n)
    return pl.pallas_call(
        functools.partial(_outproj_kernel, alpha=alpha),
        grid=(n // tile,),
        in_specs=[_rows(tile, a.shape[1]), _rows(tile, g.shape[1]), _rows(tile, d), _resident(w_out.shape),
                  _resident((1, d)), _resident((1, d)), _resident(wq.shape)],
        out_specs=[_rows(tile, d), _rows(tile, d)],
        out_shape=[jax.ShapeDtypeStruct((n, d), F32), jax.ShapeDtypeStruct((n, d), BF16)],
        compiler_params=_params("parallel"),
        name="outproj",
    )(a, g, x, w_out, ln_g, ln_b, wq)


def _cross_kernel(q_ref, mk_ref, mv_ref, o_ref):
    hd = q_ref.shape[1] // MEM_HEADS
    scale = hd ** -0.5
    for h in range(MEM_HEADS):
        cols = slice(h * hd, (h + 1) * hd)
        s = _dot_nt(q_ref[:, cols], mk_ref[:, cols].astype(BF16)) * scale
        p = jnp.exp(s - jnp.max(s, axis=-1, keepdims=True))
        o = _dot(p.astype(BF16), mv_ref[:, cols].astype(BF16))
        o_ref[:, cols] = (o * (1.0 / jnp.sum(p, axis=-1, keepdims=True))).astype(BF16)


def _cross(q2, mem_k, mem_v, batch, seq):
    d = q2.shape[1]
    n_mem = mem_k.shape[0] // batch
    tq = min(CROSS_Q_TILE, seq)
    per = seq // tq
    qblk = pl.BlockSpec((tq, d), lambda b, i: (b * per + i, 0))
    mblk = pl.BlockSpec((n_mem, d), lambda b, i: (b, 0))
    return pl.pallas_call(
        _cross_kernel,
        grid=(batch, per),
        in_specs=[qblk, mblk, mblk],
        out_specs=qblk,
        out_shape=jax.ShapeDtypeStruct(q2.shape, BF16),
        compiler_params=_params("parallel", "parallel"),
        name="cross",
    )(q2, mem_k, mem_v)


def _memkv_kernel(m_ref, wk_ref, wv_ref, k_ref, v_ref):
    mb = m_ref[...].astype(BF16)
    k_ref[...] = _dot(mb, wk_ref[...])
    v_ref[...] = _dot(mb, wv_ref[...])


def _memkv(mem, wk, wv):
    n, d = mem.shape
    tile = min(ROW_TILE, n)
    return pl.pallas_call(
        _memkv_kernel,
        grid=(n // tile,),
        in_specs=[_rows(tile, d), _resident(wk.shape), _resident(wv.shape)],
        out_specs=[_rows(tile, d), _rows(tile, d)],
        out_shape=[jax.ShapeDtypeStruct((n, d), F32)] * 2,
        compiler_params=_params("parallel"),
        name="memkv",
    )(mem, wk, wv)


def _gating_weights(ws, bs, seq):
    n = min(seq, GMLP_CHUNK)
    rep = GMLP_CHUNK // n
    w = ws[:, :n, :n]
    if rep > 1:
        eye = jnp.eye(rep, dtype=ws.dtype)
        w = jnp.einsum('ab,gts->gatbs', eye, w).reshape(GMLP_GROUPS, GMLP_CHUNK, GMLP_CHUNK)
    return w, jnp.tile(bs[:, :n], (1, rep)).T


def _trunk(x, batch, seq, mem_k, mem_v, cache, p, layer_idx, alpha, want_vrows):
    lam_init = 0.8 - 0.6 * math.exp(-0.3 * layer_idx)
    ws, bs_t = _gating_weights(p['gmlp_ws'], p['gmlp_bs'], seq)
    lam_vecs = (p['lambda_q1'], p['lambda_k1'], p['lambda_q2'], p['lambda_k2'])

    x1 = _ffn(x, p['ffn1_w_gu'], p['ffn1_w_down'], p['ln1_g'], p['ln1_b'], alpha)
    outs = _inproj(x1, p['w_in'], p['gmlp_ln_g'], p['gmlp_ln_b'], ws, bs_t, want_vrows)
    q, k, v, gated = outs[:4]
    if cache is None:
        a = _attn_prompt(q, k, v, lam_vecs, p['subln_g'], batch, seq, lam_init)
    else:
        a = _attn_sample(q, k, v, cache[0], cache[1], lam_vecs, p['subln_g'], batch, seq, lam_init)
    x2, q2 = _outproj(a, gated, x1, p['w_out'], p['ln2_g'], p['ln2_b'], p['cross_wq'], alpha)
    o = _cross(q2, mem_k, mem_v, batch, seq)
    y = _ffn(x2, p['ffn2_w_gu'], p['ffn2_w_down'], p['ln4_g'], p['ln4_b'], alpha,
             proj=(o, p['cross_wo'], p['ln3_g'], p['ln3_b']))
    return y, k, v, (outs[4] if want_vrows else None)


_MATRICES = ('ffn1_w_gu', 'ffn1_w_down', 'w_in', 'w_out', 'cross_wq', 'cross_wk', 'cross_wv', 'cross_wo',
             'ffn2_w_gu', 'ffn2_w_down')
_ROW_VECTORS = ('ln1_g', 'ln1_b', 'lambda_q1', 'lambda_k1', 'lambda_q2', 'lambda_k2', 'subln_g', 'gmlp_ln_g',
                'gmlp_ln_b', 'ln2_g', 'ln2_b', 'ln3_g', 'ln3_b', 'ln4_g', 'ln4_b')


def kernel(x_prompt, x_sample, cache_k, cache_v, cache_mem_k, cache_mem_v, mem_prompt, ffn1_w_gu, ffn1_w_down, ln1_g, ln1_b, w_in, lambda_q1, lambda_k1, lambda_q2, lambda_k2, subln_g, gmlp_ln_g, gmlp_ln_b, gmlp_ws, gmlp_bs, w_out, ln2_g, ln2_b, cross_wq, cross_wk, cross_wv, cross_wo, ln3_g, ln3_b, ffn2_w_gu, ffn2_w_down, ln4_g, ln4_b):
    weights = dict(ffn1_w_gu=ffn1_w_gu, ffn1_w_down=ffn1_w_down, ln1_g=ln1_g, ln1_b=ln1_b, w_in=w_in,
                   lambda_q1=lambda_q1, lambda_k1=lambda_k1, lambda_q2=lambda_q2, lambda_k2=lambda_k2,
                   subln_g=subln_g, gmlp_ln_g=gmlp_ln_g, gmlp_ln_b=gmlp_ln_b, gmlp_ws=gmlp_ws, gmlp_bs=gmlp_bs,
                   w_out=w_out, ln2_g=ln2_g, ln2_b=ln2_b, cross_wq=cross_wq, cross_wk=cross_wk, cross_wv=cross_wv,
                   cross_wo=cross_wo, ln3_g=ln3_g, ln3_b=ln3_b, ffn2_w_gu=ffn2_w_gu, ffn2_w_down=ffn2_w_down,
                   ln4_g=ln4_g, ln4_b=ln4_b)
    depth = w_in.shape[0]
    alpha = (2 * depth) ** 0.25
    bp, tp, d = x_prompt.shape
    bs_, ts, _ = x_sample.shape
    n_mem = mem_prompt.shape[1]
    past = cache_k.shape[2]

    xp = x_prompt.reshape(bp * tp, d)
    xs = x_sample.reshape(bs_ * ts, d)
    mem2d = mem_prompt.reshape(bp * n_mem, d)
    per_layer = [[] for _ in range(7)]
    for l in range(depth):
        p = {}
        for name, w in weights.items():
            w = w[l]
            if name in _MATRICES:
                w = w.astype(BF16)
            elif name in _ROW_VECTORS:
                w = w.reshape(1, -1)
            p[name] = w
        mem_k, mem_v = _memkv(mem2d, p['cross_wk'], p['cross_wv'])
        xp, kp, vp, _ = _trunk(xp, bp, tp, mem_k, mem_v, None, p, l, alpha, False)
        cache = (cache_k[l].reshape(bs_, past, DIFF_WIDTH), cache_v[l].reshape(bs_, past, DIFF_WIDTH))
        xs, ks, vs, gvs = _trunk(xs, bs_, ts, cache_mem_k[l].reshape(bs_ * n_mem, d),
                                 cache_mem_v[l].reshape(bs_ * n_mem, d), cache, p, l, alpha, True)
        for acc, val in zip(per_layer, (kp, vp, mem_k, mem_v, ks, vs, gvs)):
            acc.append(val)

    kp, vp, mk, mv, ks, vs, gvs = (vals[0][None] if depth == 1 else jnp.stack(vals) for vals in per_layer)
    mem_hd = d // MEM_HEADS
    return (xp.reshape(bp, tp, d), xs.reshape(bs_, ts, d),
            kp.reshape(depth, bp, tp, DIFF_HEADS, 2, HEAD_DIM),
            vp.reshape(depth, bp, tp, DIFF_HEADS, HEAD_WIDTH),
            mk.reshape(depth, bp, n_mem, MEM_HEADS, mem_hd),
            mv.reshape(depth, bp, n_mem, MEM_HEADS, mem_hd),
            ks.reshape(depth, bs_, ts, DIFF_HEADS, 2, HEAD_DIM),
            vs.reshape(depth, bs_, ts, DIFF_HEADS, HEAD_WIDTH),
            gvs.reshape(depth, bs_, ts, GMLP_GROUPS, GMLP_GROUP_DIM))
```

```python
import functools
import math

import jax
import jax.numpy as jnp
from jax import lax
from jax.experimental import pallas as pl
from jax.experimental.pallas import tpu as pltpu

F32 = jnp.float32
BF16 = jnp.bfloat16

CHUNK = 64
DIFF_HEADS = 4
HEAD_DIM = 64
HEAD_WIDTH = 2 * HEAD_DIM
DIFF_WIDTH = DIFF_HEADS * HEAD_WIDTH
GMLP_GROUPS = 4
GMLP_CHUNK = 128
GMLP_GROUP_DIM = 128
GMLP_WIDTH = GMLP_GROUPS * GMLP_GROUP_DIM
MEM_HEADS = 4
LN_EPS = 1e-5
MASKED = -1e30

ROW_TILE = 512
FF_CHUNK = 256
ATTN_Q_TILE = 256
CROSS_Q_TILE = 512
VMEM_LIMIT_BYTES = 56 * 1024 * 1024


def _params(*semantics):
    return pltpu.CompilerParams(dimension_semantics=semantics, vmem_limit_bytes=VMEM_LIMIT_BYTES)


def _resident(shape):
    return pl.BlockSpec(shape, lambda *_: (0,) * len(shape), pipeline_mode=pl.Buffered(1))


def _rows(tile, width):
    return pl.BlockSpec((tile, width), lambda i: (i, 0))


def _layer_norm(h, g, b):
    mu = jnp.mean(h, axis=-1, keepdims=True)
    d = h - mu
    var = jnp.mean(d * d, axis=-1, keepdims=True)
    return d * lax.rsqrt(var + LN_EPS) * g + b


def _dot(a, b):
    return jnp.dot(a, b, preferred_element_type=F32)


def _dot_nt(a, b):
    return lax.dot_general(a, b, (((1,), (1,)), ((), ())), preferred_element_type=F32)


def _ffn_kernel(*refs, alpha, d_ff, with_proj):
    if with_proj:
        (o_ref, xin_ref, wo_ref, lpg_ref, lpb_ref,
         wgu_ref, wd_ref, g_ref, b_ref, out_ref, act_ref) = refs
        x = _layer_norm(alpha * xin_ref[...] + _dot(o_ref[...], wo_ref[...]), lpg_ref[...], lpb_ref[...])
    else:
        xin_ref, wgu_ref, wd_ref, g_ref, b_ref, out_ref, act_ref = refs
        x = xin_ref[...]
    xb = x.astype(BF16)
    for c in range(d_ff // FF_CHUNK):
        lo = c * FF_CHUNK
        gate = _dot(xb, wgu_ref[:, lo:lo + FF_CHUNK])
        up = _dot(xb, wgu_ref[:, d_ff + lo:d_ff + lo + FF_CHUNK])
        act_ref[:, lo:lo + FF_CHUNK] = (gate * jax.nn.sigmoid(gate) * up).astype(BF16)
    y = _dot(act_ref[...], wd_ref[...])
    out_ref[...] = _layer_norm(alpha * x + 0.5 * y, g_ref[...], b_ref[...])


def _ffn(x, w_gu, w_down, ln_g, ln_b, alpha, proj=None):
    n, d = x.shape
    d_ff = w_down.shape[0]
    tile = min(ROW_TILE, n)
    in_specs, args = [], []
    if proj is not None:
        o, wo, pg, pb = proj
        in_specs += [_rows(tile, o.shape[1])]
        args += [o]
    in_specs += [_rows(tile, d)]
    args += [x]
    if proj is not None:
        in_specs += [_resident(wo.shape), _resident((1, d)), _resident((1, d))]
        args += [wo, pg, pb]
    in_specs += [_resident(w_gu.shape), _resident(w_down.shape), _resident((1, d)), _resident((1, d))]
    args += [w_gu, w_down, ln_g, ln_b]
    return pl.pallas_call(
        functools.partial(_ffn_kernel, alpha=alpha, d_ff=d_ff, with_proj=proj is not None),
        grid=(n // tile,),
        in_specs=in_specs,
        out_specs=_rows(tile, d),
        out_shape=jax.ShapeDtypeStruct((n, d), F32),
        scratch_shapes=[pltpu.VMEM((tile, d_ff), BF16)],
        compiler_params=_params("parallel"),
        name="ffn_proj" if proj is not None else "ffn",
    )(*args)


def _inproj_kernel(x_ref, w_ref, wk_ref, lng_ref, lnb_ref, ws_ref, bst_ref, q_ref, k_ref, v_ref, g_ref, *vrows_ref,
                   keys_on_lanes):
    tile = x_ref.shape[0]
    xb = x_ref[...].astype(BF16)
    z = _dot(xb, w_ref[...])
    q_ref[...] = z[:, :DIFF_WIDTH].astype(BF16)
    k_ref[...] = _dot_nt(wk_ref[...], xb) if keys_on_lanes else _dot(xb, wk_ref[...])
    for h in range(DIFF_HEADS):
        v_ref[:, h, :] = z[:, DIFF_WIDTH + h * HEAD_WIDTH:DIFF_WIDTH + (h + 1) * HEAD_WIDTH]
    u = jax.nn.gelu(z[:, 2 * DIFF_WIDTH:2 * DIFF_WIDTH + GMLP_WIDTH])
    gv = jax.nn.gelu(z[:, 2 * DIFF_WIDTH + GMLP_WIDTH:])
    vn = _layer_norm(gv, lng_ref[...], lnb_ref[...])
    if vrows_ref:
        for g in range(GMLP_GROUPS):
            vrows_ref[0][:, g, :] = vn[:, g * GMLP_GROUP_DIM:(g + 1) * GMLP_GROUP_DIM]
    vnb = vn.astype(BF16)
    row = lax.broadcasted_iota(jnp.int32, (GMLP_CHUNK, GMLP_CHUNK), 0)
    col = lax.broadcasted_iota(jnp.int32, (GMLP_CHUNK, GMLP_CHUNK), 1)
    for g in range(GMLP_GROUPS):
        cols = slice(g * GMLP_GROUP_DIM, (g + 1) * GMLP_GROUP_DIM)
        w = jnp.where(row >= col, ws_ref[g], 0.0).astype(BF16)
        bias = bst_ref[:, g:g + 1]
        for c in range(tile // GMLP_CHUNK):
            rows = slice(c * GMLP_CHUNK, (c + 1) * GMLP_CHUNK)
            mixed = _dot(w, vnb[rows, cols]) + bias
            g_ref[rows, cols] = (u[rows, cols] * mixed).astype(BF16)


def _head_rows(tile, heads, width):
    return pl.BlockSpec((tile, heads, width), lambda i: (i, 0, 0))


def _inproj(x, w_rest, w_keys, ln_g, ln_b, ws, bs_t, batch, seq, keys_on_lanes, want_vrows):
    n, d = x.shape
    tile = min(ROW_TILE, n)
    if keys_on_lanes:
        per = seq // tile
        k_shape = jax.ShapeDtypeStruct((batch, DIFF_WIDTH, seq), F32)
        k_spec = pl.BlockSpec((None, DIFF_WIDTH, tile), lambda i: (i // per, 0, i % per))
    else:
        k_shape = jax.ShapeDtypeStruct((n, DIFF_WIDTH), F32)
        k_spec = _rows(tile, DIFF_WIDTH)
    out_shape = [jax.ShapeDtypeStruct((n, DIFF_WIDTH), BF16), k_shape,
                 jax.ShapeDtypeStruct((n, DIFF_HEADS, HEAD_WIDTH), F32),
                 jax.ShapeDtypeStruct((n, GMLP_WIDTH), BF16)]
    out_specs = [_rows(tile, DIFF_WIDTH), k_spec, _head_rows(tile, DIFF_HEADS, HEAD_WIDTH), _rows(tile, GMLP_WIDTH)]
    if want_vrows:
        out_shape.append(jax.ShapeDtypeStruct((n, GMLP_GROUPS, GMLP_GROUP_DIM), F32))
        out_specs.append(_head_rows(tile, GMLP_GROUPS, GMLP_GROUP_DIM))
    return pl.pallas_call(
        functools.partial(_inproj_kernel, keys_on_lanes=keys_on_lanes),
        grid=(n // tile,),
        in_specs=[_rows(tile, d), _resident(w_rest.shape), _resident(w_keys.shape), _resident((1, GMLP_WIDTH)),
                  _resident((1, GMLP_WIDTH)), _resident(ws.shape), _resident(bs_t.shape)],
        out_specs=out_specs,
        out_shape=out_shape,
        compiler_params=_params("parallel"),
        name="inproj",
    )(x, w_rest, w_keys, ln_g, ln_b, ws, bs_t)


def _head_scalars(head, lq1_ref, lk1_ref, lq2_ref, lk2_ref, lam_init):
    slope = jnp.exp2(jnp.zeros((1, 1), F32) - 8.0 * (head + 1).astype(F32) / DIFF_HEADS)
    lam = (jnp.exp(jnp.sum(lq1_ref[...] * lk1_ref[...], axis=-1, keepdims=True))
           - jnp.exp(jnp.sum(lq2_ref[...] * lk2_ref[...], axis=-1, keepdims=True)) + lam_init)
    return slope, lam


def _stack_maps(q):
    lane = lax.broadcasted_iota(jnp.int32, q.shape, 1)
    zero = jnp.zeros_like(q)
    return jnp.concatenate([jnp.where(lane < HEAD_DIM, q, zero), jnp.where(lane >= HEAD_DIM, q, zero)], axis=0)


def _near_bias(slope, t, base_q, base_k, nk):
    r = lax.broadcasted_iota(jnp.int32, (t, nk), 0) + base_q
    c = lax.broadcasted_iota(jnp.int32, (t, nk), 1) + base_k
    bias = slope * (r - jnp.abs(r - c)).astype(F32)
    return jnp.where(c // CHUNK <= r // CHUNK, bias, MASKED)


def _diff_combine(parts, values, t, lam):
    m = functools.reduce(jnp.maximum, [jnp.max(s, axis=-1, keepdims=True) for s in parts])
    ps = [jnp.exp(s - m) for s in parts]
    denom = functools.reduce(jnp.add, [jnp.sum(p, axis=-1, keepdims=True) for p in ps])
    c0 = 1.0 / denom[:t]
    c1 = lam / denom[t:]
    out = None
    for p, v in zip(ps, values):
        a = (p[:t] * c0 - p[t:] * c1).astype(BF16)
        o = _dot(a, v)
        out = o if out is None else out + o
    return out


def _sub_norm(o, g, lam_init):
    return o * lax.rsqrt(jnp.mean(o * o, axis=-1, keepdims=True) + LN_EPS) * g * (1.0 - lam_init)


def _attn_prompt_kernel(q_ref, k_ref, v_ref, lq1_ref, lk1_ref, lq2_ref, lk2_ref, sg_ref, o_ref,
                        kb_ref, vb_ref, *, lam_init):
    seq = q_ref.shape[0]
    tq = ATTN_Q_TILE
    head = pl.program_id(1)
    slope, lam = _head_scalars(head, lq1_ref, lk1_ref, lq2_ref, lk2_ref, lam_init)
    kb_ref[...] = k_ref[...].astype(BF16)
    for h in range(DIFF_HEADS):
        @pl.when(head == h)
        def _():
            vb_ref[...] = v_ref[:, h, :].astype(BF16)
    diag = _near_bias(slope, tq, 0, 0, tq)
    diag = jnp.concatenate([diag, diag], axis=0)
    scale = HEAD_DIM ** -0.5
    for i in range(seq // tq):
        lo = i * tq
        qs = _stack_maps(q_ref[lo:lo + tq, :] * scale)
        parts = [_dot(qs, kb_ref[:, lo:lo + tq]) + (diag + slope * float(lo))]
        values = [vb_ref[lo:lo + tq, :]]
        if lo:
            far = slope * lax.broadcasted_iota(jnp.int32, (1, lo), 1).astype(F32)
            parts.append(_dot(qs, kb_ref[:, 0:lo]) + far)
            values.append(vb_ref[0:lo, :])
        out = _diff_combine(parts, values, tq, lam)
        o_ref[lo:lo + tq, :] = _sub_norm(out, sg_ref[...], lam_init).astype(BF16)


def _attn_prompt(q, k_t, v, lam_vecs, subln_g, batch, seq, lam_init):
    blk = pl.BlockSpec((seq, HEAD_WIDTH), lambda b, h: (b, h))
    vec = pl.BlockSpec((1, HEAD_DIM), lambda b, h: (0, 0))
    return pl.pallas_call(
        functools.partial(_attn_prompt_kernel, lam_init=lam_init),
        grid=(batch, DIFF_HEADS),
        in_specs=[blk, pl.BlockSpec((None, HEAD_WIDTH, seq), lambda b, h: (b, h, 0)),
                  pl.BlockSpec((seq, DIFF_HEADS, HEAD_WIDTH), lambda b, h: (b, 0, 0)),
                  vec, vec, vec, vec, pl.BlockSpec((1, HEAD_WIDTH), lambda b, h: (0, 0))],
        out_specs=blk,
        out_shape=jax.ShapeDtypeStruct((batch * seq, DIFF_WIDTH), BF16),
        scratch_shapes=[pltpu.VMEM((HEAD_WIDTH, seq), BF16), pltpu.VMEM((seq, HEAD_WIDTH), BF16)],
        compiler_params=_params("parallel", "parallel"),
        name="attn_prompt",
    )(q, k_t, v, *lam_vecs, subln_g)


def _attn_sample_kernel(q_ref, kn_ref, vn_ref, kc_ref, vc_ref, lq1_ref, lk1_ref, lq2_ref, lk2_ref, sg_ref, o_ref,
                        *, lam_init):
    t = q_ref.shape[0]
    past = kc_ref.shape[1]
    scale = HEAD_DIM ** -0.5
    far_pos = lax.broadcasted_iota(jnp.int32, (1, past), 1).astype(F32)
    for h in range(DIFF_HEADS):
        cols = slice(h * HEAD_WIDTH, (h + 1) * HEAD_WIDTH)
        slope, lam = _head_scalars(jnp.int32(h), lq1_ref, lk1_ref, lq2_ref, lk2_ref, lam_init)
        qs = _stack_maps(q_ref[:, cols] * scale)
        near = _near_bias(slope, t, past, past, t)
        parts = [_dot_nt(qs, kn_ref[:, cols].astype(BF16)) + jnp.concatenate([near, near], axis=0),
                 _dot(qs, kc_ref[cols, :].astype(BF16)) + slope * far_pos]
        values = [vn_ref[:, h, :].astype(BF16), vc_ref[:, h, :].astype(BF16)]
        out = _diff_combine(parts, values, t, lam)
        o_ref[:, cols] = _sub_norm(out, sg_ref[...], lam_init).astype(BF16)


def _attn_sample(q, k_new, v_new, k_cache_t, v_cache, lam_vecs, subln_g, batch, seq, lam_init):
    past = v_cache.shape[1]
    new = pl.BlockSpec((seq, DIFF_WIDTH), lambda b: (b, 0))
    vec = pl.BlockSpec((1, HEAD_DIM), lambda b: (0, 0))
    return pl.pallas_call(
        functools.partial(_attn_sample_kernel, lam_init=lam_init),
        grid=(batch,),
        in_specs=[new, new, pl.BlockSpec((seq, DIFF_HEADS, HEAD_WIDTH), lambda b: (b, 0, 0)),
                  pl.BlockSpec((None, DIFF_WIDTH, past), lambda b: (b, 0, 0)),
                  pl.BlockSpec((None, past, DIFF_HEADS, HEAD_WIDTH), lambda b: (b, 0, 0, 0)),
                  vec, vec, vec, vec, pl.BlockSpec((1, HEAD_WIDTH), lambda b: (0, 0))],
        out_specs=new,
        out_shape=jax.ShapeDtypeStruct((batch * seq, DIFF_WIDTH), BF16),
        compiler_params=_params("parallel"),
        name="attn_sample",
    )(q, k_new, v_new, k_cache_t, v_cache, *lam_vecs, subln_g)


def _outproj_kernel(a_ref, g_ref, x_ref, wo_ref, lg_ref, lb_ref, wq_ref, x2_ref, q2_ref, *, alpha):
    half = a_ref.shape[1]
    mix = _dot(a_ref[...], wo_ref[:half, :]) + _dot(g_ref[...], wo_ref[half:, :])
    x2 = _layer_norm(alpha * x_ref[...] + mix, lg_ref[...], lb_ref[...])
    x2_ref[...] = x2
    q2_ref[...] = _dot(x2.astype(BF16), wq_ref[...]).astype(BF16)


def _outproj(a, g, x, w_out, ln_g, ln_b, wq, alpha):
    n, d = x.shape
    tile = min(ROW_TILE, n)
    return pl.pallas_call(
        functools.partial(_outproj_kernel, alpha=alpha),
        grid=(n // tile,),
        in_specs=[_rows(tile, a.shape[1]), _rows(tile, g.shape[1]), _rows(tile, d), _resident(w_out.shape),
                  _resident((1, d)), _resident((1, d)), _resident(wq.shape)],
        out_specs=[_rows(tile, d), _rows(tile, d)],
        out_shape=[jax.ShapeDtypeStruct((n, d), F32), jax.ShapeDtypeStruct((n, d), BF16)],
        compiler_params=_params("parallel"),
        name="outproj",
    )(a, g, x, w_out, ln_g, ln_b, wq)


def _cross_kernel(q_ref, mk_ref, mv_ref, o_ref):
    hd = mk_ref.shape[2]
    scale = hd ** -0.5
    for h in range(MEM_HEADS):
        cols = slice(h * hd, (h + 1) * hd)
        s = _dot_nt(q_ref[:, cols], mk_ref[:, h, :].astype(BF16)) * scale
        p = jnp.exp(s - jnp.max(s, axis=-1, keepdims=True))
        o = _dot(p.astype(BF16), mv_ref[:, h, :].astype(BF16))
        o_ref[:, cols] = (o * (1.0 / jnp.sum(p, axis=-1, keepdims=True))).astype(BF16)


def _cross(q2, mem_k, mem_v, batch, seq):
    d = q2.shape[1]
    n_mem = mem_k.shape[0] // batch
    tq = min(CROSS_Q_TILE, seq)
    per = seq // tq
    qblk = pl.BlockSpec((tq, d), lambda b, i: (b * per + i, 0))
    mblk = pl.BlockSpec((n_mem,) + mem_k.shape[1:], lambda b, i: (b, 0, 0))
    return pl.pallas_call(
        _cross_kernel,
        grid=(batch, per),
        in_specs=[qblk, mblk, mblk],
        out_specs=qblk,
        out_shape=jax.ShapeDtypeStruct(q2.shape, BF16),
        compiler_params=_params("parallel", "parallel"),
        name="cross",
    )(q2, mem_k, mem_v)


def _memkv_kernel(m_ref, wk_ref, wv_ref, k_ref, v_ref):
    hd = k_ref.shape[2]
    mb = m_ref[...].astype(BF16)
    k = _dot(mb, wk_ref[...])
    v = _dot(mb, wv_ref[...])
    for h in range(MEM_HEADS):
        k_ref[:, h, :] = k[:, h * hd:(h + 1) * hd]
        v_ref[:, h, :] = v[:, h * hd:(h + 1) * hd]


def _memkv(mem, wk, wv):
    n, d = mem.shape
    hd = d // MEM_HEADS
    tile = min(ROW_TILE, n)
    return pl.pallas_call(
        _memkv_kernel,
        grid=(n // tile,),
        in_specs=[_rows(tile, d), _resident(wk.shape), _resident(wv.shape)],
        out_specs=[_head_rows(tile, MEM_HEADS, hd)] * 2,
        out_shape=[jax.ShapeDtypeStruct((n, MEM_HEADS, hd), F32)] * 2,
        compiler_params=_params("parallel"),
        name="memkv",
    )(mem, wk, wv)


def _gating_weights(ws, bs, seq):
    n = min(seq, GMLP_CHUNK)
    rep = GMLP_CHUNK // n
    w = ws[:, :n, :n]
    if rep > 1:
        eye = jnp.eye(rep, dtype=ws.dtype)
        w = jnp.einsum('ab,gts->gatbs', eye, w).reshape(GMLP_GROUPS, GMLP_CHUNK, GMLP_CHUNK)
    return w, jnp.tile(bs[:, :n], (1, rep)).T


def _trunk(x, batch, seq, mem_k, mem_v, cache, p, layer_idx, alpha, want_vrows):
    lam_init = 0.8 - 0.6 * math.exp(-0.3 * layer_idx)
    ws, bs_t = _gating_weights(p['gmlp_ws'], p['gmlp_bs'], seq)
    lam_vecs = (p['lambda_q1'], p['lambda_k1'], p['lambda_q2'], p['lambda_k2'])

    x1 = _ffn(x, p['ffn1_w_gu'], p['ffn1_w_down'], p['ln1_g'], p['ln1_b'], alpha)
    keys_on_lanes = cache is None
    outs = _inproj(x1, p['w_in_rest'], p['w_in_keys_t'] if keys_on_lanes else p['w_in_keys'], p['gmlp_ln_g'],
                   p['gmlp_ln_b'], ws, bs_t, batch, seq, keys_on_lanes, want_vrows)
    q, k, v, gated = outs[:4]
    if cache is None:
        a = _attn_prompt(q, k, v, lam_vecs, p['subln_g'], batch, seq, lam_init)
    else:
        a = _attn_sample(q, k, v, cache[0], cache[1], lam_vecs, p['subln_g'], batch, seq, lam_init)
    x2, q2 = _outproj(a, gated, x1, p['w_out'], p['ln2_g'], p['ln2_b'], p['cross_wq'], alpha)
    o = _cross(q2, mem_k, mem_v, batch, seq)
    y = _ffn(x2, p['ffn2_w_gu'], p['ffn2_w_down'], p['ln4_g'], p['ln4_b'], alpha,
             proj=(o, p['cross_wo'], p['ln3_g'], p['ln3_b']))
    return y, k, v, (outs[4] if want_vrows else None)


_MATRICES = ('ffn1_w_gu', 'ffn1_w_down', 'w_in', 'w_out', 'cross_wq', 'cross_wk', 'cross_wv', 'cross_wo',
             'ffn2_w_gu', 'ffn2_w_down')
_ROW_VECTORS = ('ln1_g', 'ln1_b', 'lambda_q1', 'lambda_k1', 'lambda_q2', 'lambda_k2', 'subln_g', 'gmlp_ln_g',
                'gmlp_ln_b', 'ln2_g', 'ln2_b', 'ln3_g', 'ln3_b', 'ln4_g', 'ln4_b')


def kernel(x_prompt, x_sample, cache_k, cache_v, cache_mem_k, cache_mem_v, mem_prompt, ffn1_w_gu, ffn1_w_down, ln1_g, ln1_b, w_in, lambda_q1, lambda_k1, lambda_q2, lambda_k2, subln_g, gmlp_ln_g, gmlp_ln_b, gmlp_ws, gmlp_bs, w_out, ln2_g, ln2_b, cross_wq, cross_wk, cross_wv, cross_wo, ln3_g, ln3_b, ffn2_w_gu, ffn2_w_down, ln4_g, ln4_b):
    weights = dict(ffn1_w_gu=ffn1_w_gu, ffn1_w_down=ffn1_w_down, ln1_g=ln1_g, ln1_b=ln1_b, w_in=w_in,
                   lambda_q1=lambda_q1, lambda_k1=lambda_k1, lambda_q2=lambda_q2, lambda_k2=lambda_k2,
                   subln_g=subln_g, gmlp_ln_g=gmlp_ln_g, gmlp_ln_b=gmlp_ln_b, gmlp_ws=gmlp_ws, gmlp_bs=gmlp_bs,
                   w_out=w_out, ln2_g=ln2_g, ln2_b=ln2_b, cross_wq=cross_wq, cross_wk=cross_wk, cross_wv=cross_wv,
                   cross_wo=cross_wo, ln3_g=ln3_g, ln3_b=ln3_b, ffn2_w_gu=ffn2_w_gu, ffn2_w_down=ffn2_w_down,
                   ln4_g=ln4_g, ln4_b=ln4_b)
    depth = w_in.shape[0]
    alpha = (2 * depth) ** 0.25
    bp, tp, d = x_prompt.shape
    bs_, ts, _ = x_sample.shape
    n_mem = mem_prompt.shape[1]
    past = cache_k.shape[2]
    mem_hd = d // MEM_HEADS

    xp =x_prompt.reshape(bp * tp, d)
    xs = x_sample.reshape(bs_ * ts, d)
    mem2d = mem_prompt.reshape(bp * n_mem, d)
    per_layer = [[] for _ in range(7)]
    for l in range(depth):
        p = {}
        for name, w in weights.items():
            w = w[l]
            if name in _MATRICES:
                w = w.astype(BF16)
            elif name in _ROW_VECTORS:
                w = w.reshape(1, -1)
            p[name] = w
        w_in_l = p.pop('w_in')
        p['w_in_rest'] = jnp.concatenate([w_in_l[:, :DIFF_WIDTH], w_in_l[:, 2 * DIFF_WIDTH:]], axis=1)
        p['w_in_keys'] = w_in_l[:, DIFF_WIDTH:2 * DIFF_WIDTH]
        p['w_in_keys_t'] = p['w_in_keys'].T
        mem_k, mem_v = _memkv(mem2d, p['cross_wk'], p['cross_wv'])
        xp, kp_t, vp, _ = _trunk(xp, bp, tp, mem_k, mem_v, None, p, l, alpha, False)
        cache = (jnp.transpose(cache_k[l], (0, 2, 3, 4, 1)).reshape(bs_, DIFF_WIDTH, past), cache_v[l])
        xs, ks, vs, gvs = _trunk(xs, bs_, ts, cache_mem_k[l].reshape(bs_ * n_mem, MEM_HEADS, mem_hd),
                                 cache_mem_v[l].reshape(bs_ * n_mem, MEM_HEADS, mem_hd), cache, p, l, alpha, True)
        kp = jnp.transpose(kp_t.reshape(bp, DIFF_HEADS, 2, HEAD_DIM, tp), (0, 4, 1, 2, 3))
        for acc, val in zip(per_layer, (kp, vp, mem_k, mem_v, ks, vs, gvs)):
            acc.append(val)

    kp, vp, mk, mv, ks, vs, gvs = (vals[0][None] if depth == 1 else jnp.stack(vals) for vals in per_layer)
    return (xp.reshape(bp, tp, d), xs.reshape(bs_, ts, d),
            kp.reshape(depth, bp, tp, DIFF_HEADS, 2, HEAD_DIM),
            vp.reshape(depth, bp, tp, DIFF_HEADS, HEAD_WIDTH),
            mk.reshape(depth, bp, n_mem, MEM_HEADS, mem_hd),
            mv.reshape(depth, bp, n_mem, MEM_HEADS, mem_hd),
            ks.reshape(depth, bs_, ts, DIFF_HEADS, 2, HEAD_DIM),
            vs.reshape(depth, bs_, ts, DIFF_HEADS, HEAD_WIDTH),
            gvs.reshape(depth, bs_, ts, GMLP_GROUPS, GMLP_GROUP_DIM))
```

```python
import functools
import math

import jax
import jax.numpy as jnp
from jax import lax
from jax.experimental import pallas as pl
from jax.experimental.pallas import tpu as pltpu

F32 = jnp.float32
BF16 = jnp.bfloat16

CHUNK = 64
DIFF_HEADS = 4
HEAD_DIM = 64
HEAD_WIDTH = 2 * HEAD_DIM
DIFF_WIDTH = DIFF_HEADS * HEAD_WIDTH
GMLP_GROUPS = 4
GMLP_CHUNK = 128
GMLP_GROUP_DIM = 128
GMLP_WIDTH = GMLP_GROUPS * GMLP_GROUP_DIM
MEM_HEADS = 4
LN_EPS = 1e-5
MASKED = -1e30
LANES = 128

ROW_TILE = 512
FF_CHUNK = 256
ATTN_Q_TILE = 256
CROSS_Q_TILE = 512
VMEM_LIMIT_BYTES = 56 * 1024 * 1024


def _params(*semantics):
    return pltpu.CompilerParams(dimension_semantics=semantics, vmem_limit_bytes=VMEM_LIMIT_BYTES)


def _resident(shape):
    return pl.BlockSpec(shape, lambda *_: (0,) * len(shape), pipeline_mode=pl.Buffered(1))


def _rows(tile, width):
    return pl.BlockSpec((tile, width), lambda i: (i, 0))


def _layer_norm(h, g, b):
    mu = jnp.mean(h, axis=-1, keepdims=True)
    d = h - mu
    var = jnp.mean(d * d, axis=-1, keepdims=True)
    return d * lax.rsqrt(var + LN_EPS) * g + b


def _dot(a, b):
    return jnp.dot(a, b, preferred_element_type=F32)


def _dot_nt(a, b):
    return lax.dot_general(a, b, (((1,), (1,)), ((), ())), preferred_element_type=F32)


def _head_view_rows(rows, heads, width):
    return rows * heads * (width // LANES)


def _head_load(ref, h, heads, rows, width):
    tiles = width // LANES
    parts = [ref[pl.ds(j * heads + h, rows, stride=heads * tiles), :] for j in range(tiles)]
    return parts[0] if tiles == 1 else jnp.concatenate(parts, axis=1)


def _head_store(ref, h, heads, value):
    rows, width = value.shape
    tiles = width // LANES
    for j in range(tiles):
        ref[pl.ds(j * heads + h, rows, stride=heads * tiles), :] = value[:, j * LANES:(j + 1) * LANES]


def _to_head_view(x):
    *lead, heads, width = x.shape
    tiles = width // LANES
    x = x.reshape(*lead, heads, tiles, LANES)
    x = jnp.swapaxes(x, -3, -2)
    return x.reshape(-1, LANES)


def _from_head_view(x, lead, heads, width):
    tiles = width // LANES
    x = x.reshape(*lead, tiles, heads, LANES)
    x = jnp.swapaxes(x, -3, -2)
    return x.reshape(*lead, heads, width)


def _ffn_kernel(*refs, alpha, d_ff, with_proj):
    if with_proj:
        (o_ref, xin_ref, wo_ref, lpg_ref, lpb_ref,
         wgu_ref, wd_ref, g_ref, b_ref, out_ref, act_ref) = refs
        x = _layer_norm(alpha * xin_ref[...] + _dot(o_ref[...], wo_ref[...]), lpg_ref[...], lpb_ref[...])
    else:
        xin_ref, wgu_ref, wd_ref, g_ref, b_ref, out_ref, act_ref = refs
        x = xin_ref[...]
    xb = x.astype(BF16)
    for c in range(d_ff // FF_CHUNK):
        lo = c * FF_CHUNK
        gate = _dot(xb, wgu_ref[:, lo:lo + FF_CHUNK])
        up = _dot(xb, wgu_ref[:, d_ff + lo:d_ff + lo + FF_CHUNK])
        act_ref[:, lo:lo + FF_CHUNK] = (gate * jax.nn.sigmoid(gate) * up).astype(BF16)
    y = _dot(act_ref[...], wd_ref[...])
    out_ref[...] = _layer_norm(alpha * x + 0.5 * y, g_ref[...], b_ref[...])


def _ffn(x, w_gu, w_down, ln_g, ln_b, alpha, proj=None):
    n, d = x.shape
    d_ff = w_down.shape[0]
    tile = min(ROW_TILE, n)
    in_specs, args = [], []
    if proj is not None:
        o, wo, pg, pb = proj
        in_specs += [_rows(tile, o.shape[1])]
        args += [o]
    in_specs += [_rows(tile, d)]
    args += [x]
    if proj is not None:
        in_specs += [_resident(wo.shape), _resident((1, d)), _resident((1, d))]
        args += [wo, pg, pb]
    in_specs += [_resident(w_gu.shape), _resident(w_down.shape), _resident((1, d)), _resident((1, d))]
    args += [w_gu, w_down, ln_g, ln_b]
    return pl.pallas_call(
        functools.partial(_ffn_kernel, alpha=alpha, d_ff=d_ff, with_proj=proj is not None),
        grid=(n // tile,),
        in_specs=in_specs,
        out_specs=_rows(tile, d),
        out_shape=jax.ShapeDtypeStruct((n, d), F32),
        scratch_shapes=[pltpu.VMEM((tile, d_ff), BF16)],
        compiler_params=_params("parallel"),
        name="ffn_proj" if proj is not None else "ffn",
    )(*args)


def _inproj_kernel(x_ref, w_ref, wk_ref, lng_ref, lnb_ref, ws_ref, bst_ref, q_ref, k_ref, v_ref, g_ref, *vrows_ref,
                   keys_on_lanes):
    tile = x_ref.shape[0]
    xb = x_ref[...].astype(BF16)
    z = _dot(xb, w_ref[...])
    q_ref[...] = z[:, :DIFF_WIDTH].astype(BF16)
    k_ref[...] = _dot_nt(wk_ref[...], xb) if keys_on_lanes else _dot(xb, wk_ref[...])
    for h in range(DIFF_HEADS):
        _head_store(v_ref, h, DIFF_HEADS, z[:, DIFF_WIDTH + h * HEAD_WIDTH:DIFF_WIDTH + (h + 1) * HEAD_WIDTH])
    u = jax.nn.gelu(z[:, 2 * DIFF_WIDTH:2 * DIFF_WIDTH + GMLP_WIDTH])
    gv = jax.nn.gelu(z[:, 2 * DIFF_WIDTH + GMLP_WIDTH:])
    vn = _layer_norm(gv, lng_ref[...], lnb_ref[...])
    if vrows_ref:
        for g in range(GMLP_GROUPS):
            _head_store(vrows_ref[0], g, GMLP_GROUPS, vn[:, g * GMLP_GROUP_DIM:(g + 1) * GMLP_GROUP_DIM])
    vnb = vn.astype(BF16)
    row = lax.broadcasted_iota(jnp.int32, (GMLP_CHUNK, GMLP_CHUNK), 0)
    col = lax.broadcasted_iota(jnp.int32, (GMLP_CHUNK, GMLP_CHUNK), 1)
    for g in range(GMLP_GROUPS):
        cols = slice(g * GMLP_GROUP_DIM, (g + 1) * GMLP_GROUP_DIM)
        w = jnp.where(row >= col, ws_ref[g], 0.0).astype(BF16)
        bias = bst_ref[:, g:g + 1]
        for c in range(tile // GMLP_CHUNK):
            rows = slice(c * GMLP_CHUNK, (c + 1) * GMLP_CHUNK)
            mixed = _dot(w, vnb[rows, cols]) + bias
            g_ref[rows, cols] = (u[rows, cols] * mixed).astype(BF16)


def _head_rows(tile, heads, width):
    return _rows(_head_view_rows(tile, heads, width), LANES)


def _head_shape(rows, heads, width):
    return jax.ShapeDtypeStruct((_head_view_rows(rows, heads, width), LANES), F32)


def _inproj(x, w_rest, w_keys, ln_g, ln_b, ws, bs_t, batch, seq, keys_on_lanes, want_vrows):
    n, d = x.shape
    tile = min(ROW_TILE, n)
    if keys_on_lanes:
        per = seq // tile
        k_shape = jax.ShapeDtypeStruct((batch, DIFF_WIDTH, seq), F32)
        k_spec = pl.BlockSpec((None, DIFF_WIDTH, tile), lambda i: (i // per, 0, i % per))
    else:
        k_shape = jax.ShapeDtypeStruct((n, DIFF_WIDTH), F32)
        k_spec = _rows(tile, DIFF_WIDTH)
    out_shape = [jax.ShapeDtypeStruct((n, DIFF_WIDTH), BF16), k_shape,
                 _head_shape(n, DIFF_HEADS, HEAD_WIDTH),
                 jax.ShapeDtypeStruct((n, GMLP_WIDTH), BF16)]
    out_specs = [_rows(tile, DIFF_WIDTH), k_spec, _head_rows(tile, DIFF_HEADS, HEAD_WIDTH), _rows(tile, GMLP_WIDTH)]
    if want_vrows:
        out_shape.append(_head_shape(n, GMLP_GROUPS, GMLP_GROUP_DIM))
        out_specs.append(_head_rows(tile, GMLP_GROUPS, GMLP_GROUP_DIM))
    return pl.pallas_call(
        functools.partial(_inproj_kernel, keys_on_lanes=keys_on_lanes),
        grid=(n // tile,),
        in_specs=[_rows(tile, d), _resident(w_rest.shape), _resident(w_keys.shape), _resident((1, GMLP_WIDTH)),
                  _resident((1, GMLP_WIDTH)), _resident(ws.shape), _resident(bs_t.shape)],
        out_specs=out_specs,
        out_shape=out_shape,
        compiler_params=_params("parallel"),
        name="inproj",
    )(x, w_rest, w_keys, ln_g, ln_b, ws, bs_t)


def _head_scalars(head, lq1_ref, lk1_ref, lq2_ref, lk2_ref, lam_init):
    slope = jnp.exp2(jnp.zeros((1, 1), F32) - 8.0 * (head + 1).astype(F32) / DIFF_HEADS)
    lam = (jnp.exp(jnp.sum(lq1_ref[...] * lk1_ref[...], axis=-1, keepdims=True))
           - jnp.exp(jnp.sum(lq2_ref[...] * lk2_ref[...], axis=-1, keepdims=True)) + lam_init)
    return slope, lam


def _stack_maps(q):
    lane = lax.broadcasted_iota(jnp.int32, q.shape, 1)
    zero = jnp.zeros_like(q)
    return jnp.concatenate([jnp.where(lane < HEAD_DIM, q, zero), jnp.where(lane >= HEAD_DIM, q, zero)], axis=0)


def _near_bias(slope, t, base_q, base_k, nk):
    r = lax.broadcasted_iota(jnp.int32, (t, nk), 0) + base_q
    c = lax.broadcasted_iota(jnp.int32, (t, nk), 1) + base_k
    bias = slope * (r - jnp.abs(r - c)).astype(F32)
    return jnp.where(c // CHUNK <= r // CHUNK, bias, MASKED)


def _diff_combine(parts, values, t, lam):
    m = functools.reduce(jnp.maximum, [jnp.max(s, axis=-1, keepdims=True) for s in parts])
    ps = [jnp.exp(s - m) for s in parts]
    denom = functools.reduce(jnp.add, [jnp.sum(p, axis=-1, keepdims=True) for p in ps])
    c0 = 1.0 / denom[:t]
    c1 = lam / denom[t:]
    out = None
    for p, v in zip(ps, values):
        a = (p[:t] * c0 - p[t:] * c1).astype(BF16)
        o = _dot(a, v)
        out = o if out is None else out + o
    return out


def _sub_norm(o, g, lam_init):
    return o * lax.rsqrt(jnp.mean(o * o, axis=-1, keepdims=True) + LN_EPS) * g * (1.0 - lam_init)


def _attn_prompt_kernel(q_ref, k_ref, v_ref, lq1_ref, lk1_ref, lq2_ref, lk2_ref, sg_ref, o_ref,
                        kb_ref, vb_ref, *, lam_init):
    seq = q_ref.shape[0]
    tq = ATTN_Q_TILE
    head = pl.program_id(1)
    slope, lam = _head_scalars(head, lq1_ref, lk1_ref, lq2_ref, lk2_ref, lam_init)
    kb_ref[...] = k_ref[...].astype(BF16)
    for h in range(DIFF_HEADS):
        @pl.when(head == h)
        def _():
            vb_ref[...] = _head_load(v_ref, h, DIFF_HEADS, seq, HEAD_WIDTH).astype(BF16)
    diag = _near_bias(slope, tq, 0, 0, tq)
    diag = jnp.concatenate([diag, diag], axis=0)
    scale = HEAD_DIM ** -0.5
    for i in range(seq // tq):
        lo = i * tq
        qs = _stack_maps(q_ref[lo:lo + tq, :] * scale)
        parts = [_dot(qs, kb_ref[:, lo:lo + tq]) + (diag + slope * float(lo))]
        values = [vb_ref[lo:lo + tq, :]]
        if lo:
            far = slope * lax.broadcasted_iota(jnp.int32, (1, lo), 1).astype(F32)
            parts.append(_dot(qs, kb_ref[:, 0:lo]) + far)
            values.append(vb_ref[0:lo, :])
        out = _diff_combine(parts, values, tq, lam)
        o_ref[lo:lo + tq, :] = _sub_norm(out, sg_ref[...], lam_init).astype(BF16)


def _attn_prompt(q, k_t, v, lam_vecs, subln_g, batch, seq, lam_init):
    blk = pl.BlockSpec((seq, HEAD_WIDTH), lambda b, h: (b, h))
    vec = pl.BlockSpec((1, HEAD_DIM), lambda b, h: (0, 0))
    return pl.pallas_call(
        functools.partial(_attn_prompt_kernel, lam_init=lam_init),
        grid=(batch, DIFF_HEADS),
        in_specs=[blk, pl.BlockSpec((None, HEAD_WIDTH, seq), lambda b, h: (b, h, 0)),
                  pl.BlockSpec((_head_view_rows(seq, DIFF_HEADS, HEAD_WIDTH), LANES), lambda b, h: (b, 0)),
                  vec, vec, vec, vec, pl.BlockSpec((1, HEAD_WIDTH), lambda b, h: (0, 0))],
        out_specs=blk,
        out_shape=jax.ShapeDtypeStruct((batch * seq, DIFF_WIDTH), BF16),
        scratch_shapes=[pltpu.VMEM((HEAD_WIDTH, seq), BF16), pltpu.VMEM((seq, HEAD_WIDTH), BF16)],
        compiler_params=_params("parallel", "parallel"),
        name="attn_prompt",
    )(q, k_t, v, *lam_vecs, subln_g)


def _attn_sample_kernel(q_ref, kn_ref, vn_ref, kc_ref, vc_ref, lq1_ref, lk1_ref, lq2_ref, lk2_ref, sg_ref, o_ref,
                        *, lam_init):
    t = q_ref.shape[0]
    past = kc_ref.shape[1]
    scale = HEAD_DIM ** -0.5
    far_pos = lax.broadcasted_iota(jnp.int32, (1, past), 1).astype(F32)
    for h in range(DIFF_HEADS):
        cols = slice(h * HEAD_WIDTH, (h + 1) * HEAD_WIDTH)
        slope, lam = _head_scalars(jnp.int32(h), lq1_ref, lk1_ref, lq2_ref, lk2_ref, lam_init)
        qs = _stack_maps(q_ref[:, cols] * scale)
        near = _near_bias(slope, t, past, past, t)
        parts = [_dot_nt(qs, kn_ref[:, cols].astype(BF16)) + jnp.concatenate([near, near], axis=0),
                 _dot(qs, kc_ref[cols, :].astype(BF16)) + slope * far_pos]
        values = [_head_load(vn_ref, h, DIFF_HEADS, t, HEAD_WIDTH).astype(BF16),
                  _head_load(vc_ref, h, DIFF_HEADS, past, HEAD_WIDTH).astype(BF16)]
        out = _diff_combine(parts, values, t, lam)
        o_ref[:, cols] = _sub_norm(out, sg_ref[...], lam_init).astype(BF16)


def _attn_sample(q, k_new, v_new, k_cache_t, v_cache, lam_vecs, subln_g, batch, seq, lam_init):
    past = k_cache_t.shape[2]
    new = pl.BlockSpec((seq, DIFF_WIDTH), lambda b: (b, 0))
    vec = pl.BlockSpec((1, HEAD_DIM), lambda b: (0, 0))
    return pl.pallas_call(
        functools.partial(_attn_sample_kernel, lam_init=lam_init),
        grid=(batch,),
        in_specs=[new, new, _head_rows(seq, DIFF_HEADS, HEAD_WIDTH),
                  pl.BlockSpec((None, DIFF_WIDTH, past), lambda b: (b, 0, 0)),
                  _head_rows(past, DIFF_HEADS, HEAD_WIDTH),
                  vec, vec, vec, vec, pl.BlockSpec((1, HEAD_WIDTH), lambda b: (0, 0))],
        out_specs=new,
        out_shape=jax.ShapeDtypeStruct((batch * seq, DIFF_WIDTH), BF16),
        compiler_params=_params("parallel"),
        name="attn_sample",
    )(q, k_new, v_new, k_cache_t, v_cache, *lam_vecs, subln_g)


def _outproj_kernel(a_ref, g_ref, x_ref, wo_ref, lg_ref, lb_ref, wq_ref, x2_ref, q2_ref, *, alpha):
    half = a_ref.shape[1]
    mix = _dot(a_ref[...], wo_ref[:half, :]) + _dot(g_ref[...], wo_ref[half:, :])
    x2 = _layer_norm(alpha * x_ref[...] + mix, lg_ref[...], lb_ref[...])
    x2_ref[...] = x2
    q2_ref[...] = _dot(x2.astype(BF16), wq_ref[...]).astype(BF16)


def _outproj(a, g, x, w_out, ln_g, ln_b, wq, alpha):
    n, d = x.shape
    tile = min(ROW_TILE, n)
    return pl.pallas_call(
        functools.partial(_outproj_kernel, alpha=alpha),
        grid=(n // tile,),
        in_specs=[_rows(tile, a.shape[1]), _rows(tile, g.shape[1]), _rows(tile, d), _resident(w_out.shape),
                  _resident((1, d)), _resident((1, d)), _resident(wq.shape)],
        out_specs=[_rows(tile, d), _rows(tile, d)],
        out_shape=[jax.ShapeDtypeStruct((n, d), F32), jax.ShapeDtypeStruct((n, d), BF16)],
        compiler_params=_params("parallel"),
        name="outproj",
    )(a, g, x, w_out, ln_g, ln_b, wq)


def _cross_kernel(q_ref, mk_ref, mv_ref, o_ref, *, n_mem):
    hd = q_ref.shape[1] // MEM_HEADS
    scale = hd ** -0.5
    for h in range(MEM_HEADS):
        cols = slice(h * hd, (h + 1) * hd)
        mk = _head_load(mk_ref, h, MEM_HEADS, n_mem, hd).astype(BF16)
        mv = _head_load(mv_ref, h, MEM_HEADS, n_mem, hd).astype(BF16)
        s = _dot_nt(q_ref[:, cols], mk) * scale
        p = jnp.exp(s - jnp.max(s, axis=-1, keepdims=True))
        o = _dot(p.astype(BF16), mv)
        o_ref[:, cols] = (o * (1.0 / jnp.sum(p, axis=-1, keepdims=True))).astype(BF16)


def _cross(q2, mem_k, mem_v, batch, seq, n_mem):
    d = q2.shape[1]
    tq = min(CROSS_Q_TILE, seq)
    per = seq // tq
    qblk = pl.BlockSpec((tq, d), lambda b, i: (b * per + i, 0))
    mblk = pl.BlockSpec((_head_view_rows(n_mem, MEM_HEADS, d // MEM_HEADS), LANES), lambda b, i: (b, 0))
    return pl.pallas_call(
        functools.partial(_cross_kernel, n_mem=n_mem),
        grid=(batch, per),
        in_specs=[qblk, mblk, mblk],
        out_specs=qblk,
        out_shape=jax.ShapeDtypeStruct(q2.shape, BF16),
        compiler_params=_params("parallel", "parallel"),
        name="cross",
    )(q2, mem_k, mem_v)


def _memkv_kernel(m_ref, wk_ref, wv_ref, k_ref, v_ref):
    mb = m_ref[...].astype(BF16)
    k = _dot(mb, wk_ref[...])
    v = _dot(mb, wv_ref[...])
    hd = k.shape[1] // MEM_HEADS
    for h in range(MEM_HEADS):
        _head_store(k_ref, h, MEM_HEADS, k[:, h * hd:(h + 1) * hd])
        _head_store(v_ref, h, MEM_HEADS, v[:, h * hd:(h + 1) * hd])


def _memkv(mem, wk, wv):
    n, d = mem.shape
    hd = d // MEM_HEADS
    tile = min(ROW_TILE, n)
    return pl.pallas_call(
        _memkv_kernel,
        grid=(n // tile,),
        in_specs=[_rows(tile, d), _resident(wk.shape), _resident(wv.shape)],
        out_specs=[_head_rows(tile, MEM_HEADS, hd)] * 2,
        out_shape=[_head_shape(n, MEM_HEADS, hd)] * 2,
        compiler_params=_params("parallel"),
        name="memkv",
    )(mem, wk, wv)


def _gating_weights(ws, bs, seq):
    n = min(seq, GMLP_CHUNK)
    rep = GMLP_CHUNK // n
    w = ws[:, :n, :n]
    if rep > 1:
        eye = jnp.eye(rep, dtype=ws.dtype)
        w = jnp.einsum('ab,gts->gatbs', eye, w).reshape(GMLP_GROUPS, GMLP_CHUNK, GMLP_CHUNK)
    return w, jnp.tile(bs[:, :n], (1, rep)).T


def _trunk(x, batch, seq, mem_k, mem_v, cache, p, layer_idx, alpha, want_vrows):
    lam_init = 0.8 - 0.6 * math.exp(-0.3 * layer_idx)
    ws, bs_t = _gating_weights(p['gmlp_ws'], p['gmlp_bs'], seq)
    lam_vecs = (p['lambda_q1'], p['lambda_k1'], p['lambda_q2'], p['lambda_k2'])

    x1 = _ffn(x, p['ffn1_w_gu'], p['ffn1_w_down'], p['ln1_g'], p['ln1_b'], alpha)
    keys_on_lanes = cache is None
    outs = _inproj(x1, p['w_in_rest'], p['w_in_keys_t'] if keys_on_lanes else p['w_in_keys'], p['gmlp_ln_g'],
                   p['gmlp_ln_b'], ws, bs_t, batch, seq, keys_on_lanes, want_vrows)
    q, k, v, gated = outs[:4]
    if cache is None:
        a = _attn_prompt(q, k, v, lam_vecs, p['subln_g'], batch, seq, lam_init)
    else:
        a = _attn_sample(q, k, v, cache[0], cache[1], lam_vecs, p['subln_g'], batch, seq, lam_init)
    x2, q2 = _outproj(a, gated, x1, p['w_out'], p['ln2_g'], p['ln2_b'], p['cross_wq'], alpha)
    n_mem = mem_k.size // (batch * x.shape[1])
    o = _cross(q2, mem_k, mem_v, batch, seq, n_mem)
    y = _ffn(x2, p['ffn2_w_gu'], p['ffn2_w_down'], p['ln4_g'], p['ln4_b'], alpha,
             proj=(o, p['cross_wo'], p['ln3_g'], p['ln3_b']))
    return y, k, v, (outs[4] if want_vrows else None)


_MATRICES = ('ffn1_w_gu', 'ffn1_w_down', 'w_in', 'w_out', 'cross_wq', 'cross_wk', 'cross_wv', 'cross_wo',
             'ffn2_w_gu', 'ffn2_w_down')
_ROW_VECTORS = ('ln1_g', 'ln1_b', 'lambda_q1', 'lambda_k1', 'lambda_q2', 'lambda_k2', 'subln_g', 'gmlp_ln_g',
                'gmlp_ln_b', 'ln2_g', 'ln2_b', 'ln3_g', 'ln3_b', 'ln4_g', 'ln4_b')


def kernel(x_prompt, x_sample, cache_k, cache_v, cache_mem_k, cache_mem_v, mem_prompt, ffn1_w_gu, ffn1_w_down, ln1_g, ln1_b, w_in, lambda_q1, lambda_k1, lambda_q2, lambda_k2, subln_g, gmlp_ln_g, gmlp_ln_b, gmlp_ws, gmlp_bs, w_out, ln2_g, ln2_b, cross_wq, cross_wk, cross_wv, cross_wo, ln3_g, ln3_b, ffn2_w_gu, ffn2_w_down, ln4_g, ln4_b):
    weights = dict(ffn1_w_gu=ffn1_w_gu, ffn1_w_down=ffn1_w_down, ln1_g=ln1_g, ln1_b=ln1_b, w_in=w_in,
                   lambda_q1=lambda_q1, lambda_k1=lambda_k1, lambda_q2=lambda_q2, lambda_k2=lambda_k2,
                   subln_g=subln_g, gmlp_ln_g=gmlp_ln_g, gmlp_ln_b=gmlp_ln_b, gmlp_ws=gmlp_ws, gmlp_bs=gmlp_bs,
                   w_out=w_out, ln2_g=ln2_g, ln2_b=ln2_b, cross_wq=cross_wq, cross_wk=cross_wk, cross_wv=cross_wv,
                   cross_wo=cross_wo, ln3_g=ln3_g, ln3_b=ln3_b, ffn2_w_gu=ffn2_w_gu, ffn2_w_down=ffn2_w_down,
                   ln4_g=ln4_g, ln4_b=ln4_b)
    depth = w_in.shape[0]
    alpha = (2 * depth) ** 0.25
    bp, tp, d = x_prompt.shape
    bs_, ts, _ = x_sample.shape
    n_mem = mem_prompt.shape[1]
    past = cache_k.shape[2]
    mem_hd = d // MEM_HEADS

    xp =x_prompt.reshape(bp * tp, d)
    xs = x_sample.reshape(bs_ * ts, d)
    mem2d = mem_prompt.reshape(bp * n_mem, d)
    per_layer = [[] for _ in range(7)]
    for l in range(depth):
        p = {}
        for name, w in weights.items():
            w = w[l]
            if name in _MATRICES:
                w = w.astype(BF16)
            elif name in _ROW_VECTORS:
                w = w.reshape(1, -1)
            p[name] = w
        w_in_l = p.pop('w_in')
        p['w_in_rest'] = jnp.concatenate([w_in_l[:, :DIFF_WIDTH], w_in_l[:, 2 * DIFF_WIDTH:]], axis=1)
        p['w_in_keys'] = w_in_l[:, DIFF_WIDTH:2 * DIFF_WIDTH]
        p['w_in_keys_t'] = p['w_in_keys'].T
        mem_k, mem_v = _memkv(mem2d, p['cross_wk'], p['cross_wv'])
        xp, kp_t, vp, _ = _trunk(xp, bp, tp, mem_k, mem_v, None, p, l, alpha, False)
        cache = (jnp.transpose(cache_k[l], (0, 2, 3, 4, 1)).reshape(bs_, DIFF_WIDTH, past),
                 _to_head_view(cache_v[l]))
        xs, ks, vs, gvs = _trunk(xs, bs_, ts, _to_head_view(cache_mem_k[l]), _to_head_view(cache_mem_v[l]),
                                 cache, p, l, alpha, True)
        kp = jnp.transpose(kp_t.reshape(bp, DIFF_HEADS, 2, HEAD_DIM, tp), (0, 4, 1, 2, 3))
        vals = (kp, _from_head_view(vp, (bp, tp), DIFF_HEADS, HEAD_WIDTH),
                _from_head_view(mem_k, (bp, n_mem), MEM_HEADS, mem_hd),
                _from_head_view(mem_v, (bp, n_mem), MEM_HEADS, mem_hd), ks,
                _from_head_view(vs, (bs_, ts), DIFF_HEADS, HEAD_WIDTH),
                _from_head_view(gvs, (bs_, ts), GMLP_GROUPS, GMLP_GROUP_DIM))
        for acc, val in zip(per_layer, vals):
            acc.append(val)

    kp, vp, mk, mv, ks, vs, gvs = (vals[0][None] if depth == 1 else jnp.stack(vals) for vals in per_layer)
    return (xp.reshape(bp, tp, d), xs.reshape(bs_, ts, d),
            kp.reshape(depth, bp, tp, DIFF_HEADS, 2, HEAD_DIM),
            vp.reshape(depth, bp, tp, DIFF_HEADS, HEAD_WIDTH),
            mk.reshape(depth, bp, n_mem, MEM_HEADS, mem_hd),
            mv.reshape(depth, bp, n_mem, MEM_HEADS, mem_hd),
            ks.reshape(depth, bs_, ts, DIFF_HEADS, 2, HEAD_DIM),
            vs.reshape(depth, bs_, ts, DIFF_HEADS, HEAD_WIDTH),
            gvs.reshape(depth, bs_, ts, GMLP_GROUPS, GMLP_GROUP_DIM))
```

```python
import functools
import math

import jax
import jax.numpy as jnp
from jax import lax
from jax.experimental import pallas as pl
from jax.experimental.pallas import tpu as pltpu

F32 = jnp.float32
BF16 = jnp.bfloat16

CHUNK = 64
DIFF_HEADS = 4
HEAD_DIM = 64
HEAD_WIDTH = 2 * HEAD_DIM
DIFF_WIDTH = DIFF_HEADS * HEAD_WIDTH
GMLP_GROUPS = 4
GMLP_CHUNK = 128
GMLP_GROUP_DIM = 128
GMLP_WIDTH = GMLP_GROUPS * GMLP_GROUP_DIM
MEM_HEADS = 4
LN_EPS = 1e-5
MASKED = -1e30
LANES = 128
LOG2E = math.log2(math.e)
Q_SCALE = HEAD_DIM ** -0.5 * LOG2E
POS_SPLIT = 64
BIAS_TERMS = 3

ROW_TILE = 512
FF_CHUNK = 256
ATTN_Q_TILE = 256
CROSS_Q_TILE = 512
VMEM_LIMIT_BYTES = 56 * 1024 * 1024


def _params(*semantics):
    return pltpu.CompilerParams(dimension_semantics=semantics, vmem_limit_bytes=VMEM_LIMIT_BYTES)


def _resident(shape):
    return pl.BlockSpec(shape, lambda *_: (0,) * len(shape), pipeline_mode=pl.Buffered(1))


def _rows(tile, width):
    return pl.BlockSpec((tile, width), lambda i: (i, 0))


def _layer_norm(h, g, b):
    mu = jnp.mean(h, axis=-1, keepdims=True)
    d = h - mu
    var = jnp.mean(d * d, axis=-1, keepdims=True)
    return d * lax.rsqrt(var + LN_EPS) * g + b


def _dot(a, b):
    return jnp.dot(a, b, preferred_element_type=F32)


def _dot_nt(a, b):
    return lax.dot_general(a, b, (((1,), (1,)), ((), ())), preferred_element_type=F32)


def _head_view_rows(rows, heads, width):
    return rows * heads * (width // LANES)


def _head_load(ref, h, heads, rows, width):
    tiles = width // LANES
    parts = [ref[pl.ds(j * heads + h, rows, stride=heads * tiles), :] for j in range(tiles)]
    return parts[0] if tiles == 1 else jnp.concatenate(parts, axis=1)


def _head_store(ref, h, heads, value):
    rows, width = value.shape
    tiles = width // LANES
    for j in range(tiles):
        ref[pl.ds(j * heads + h, rows, stride=heads * tiles), :] = value[:, j * LANES:(j + 1) * LANES]


def _to_head_view(x):
    *lead, heads, width = x.shape
    tiles = width // LANES
    x = x.reshape(*lead, heads, tiles, LANES)
    x = jnp.swapaxes(x, -3, -2)
    return x.reshape(-1, LANES)


def _from_head_view(x, lead, heads, width):
    tiles = width // LANES
    x = x.reshape(*lead, tiles, heads, LANES)
    x = jnp.swapaxes(x, -3, -2)
    return x.reshape(*lead, heads, width)


def _ffn_kernel(*refs, alpha, d_ff, with_proj):
    if with_proj:
        (o_ref, xin_ref, wo_ref, lpg_ref, lpb_ref,
         wgu_ref, wd_ref, g_ref, b_ref, out_ref, act_ref) = refs
        x = _layer_norm(alpha * xin_ref[...] + _dot(o_ref[...], wo_ref[...]), lpg_ref[...], lpb_ref[...])
    else:
        xin_ref, wgu_ref, wd_ref, g_ref, b_ref, out_ref, act_ref = refs
        x = xin_ref[...]
    xb = x.astype(BF16)
    for c in range(d_ff // FF_CHUNK):
        lo = c * FF_CHUNK
        gate = _dot(xb, wgu_ref[:, lo:lo + FF_CHUNK])
        up = _dot(xb, wgu_ref[:, d_ff + lo:d_ff + lo + FF_CHUNK])
        act_ref[:, lo:lo + FF_CHUNK] = (gate * jax.nn.sigmoid(gate) * up).astype(BF16)
    y = _dot(act_ref[...], wd_ref[...])
    out_ref[...] = _layer_norm(alpha * x + 0.5 * y, g_ref[...], b_ref[...])


def _ffn(x, w_gu, w_down, ln_g, ln_b, alpha, proj=None):
    n, d = x.shape
    d_ff = w_down.shape[0]
    tile = min(ROW_TILE, n)
    in_specs, args = [], []
    if proj is not None:
        o, wo, pg, pb = proj
        in_specs += [_rows(tile, o.shape[1])]
        args += [o]
    in_specs += [_rows(tile, d)]
    args += [x]
    if proj is not None:
        in_specs += [_resident(wo.shape), _resident((1, d)), _resident((1, d))]
        args += [wo, pg, pb]
    in_specs += [_resident(w_gu.shape), _resident(w_down.shape), _resident((1, d)), _resident((1, d))]
    args += [w_gu, w_down, ln_g, ln_b]
    return pl.pallas_call(
        functools.partial(_ffn_kernel, alpha=alpha, d_ff=d_ff, with_proj=proj is not None),
        grid=(n // tile,),
        in_specs=in_specs,
        out_specs=_rows(tile, d),
        out_shape=jax.ShapeDtypeStruct((n, d), F32),
        scratch_shapes=[pltpu.VMEM((tile, d_ff), BF16)],
        compiler_params=_params("parallel"),
        name="ffn_proj" if proj is not None else "ffn",
    )(*args)


def _inproj_kernel(x_ref, w_ref, wk_ref, lng_ref, lnb_ref, ws_ref, bst_ref, q_ref, k_ref, v_ref, g_ref, *vrows_ref,
                   keys_on_lanes):
    tile = x_ref.shape[0]
    xb = x_ref[...].astype(BF16)
    z = _dot(xb, w_ref[...])
    q_ref[...] = (z[:, :DIFF_WIDTH] * Q_SCALE).astype(BF16)
    k_ref[...] = _dot_nt(wk_ref[...], xb) if keys_on_lanes else _dot(xb, wk_ref[...])
    for h in range(DIFF_HEADS):
        _head_store(v_ref, h, DIFF_HEADS, z[:, DIFF_WIDTH + h * HEAD_WIDTH:DIFF_WIDTH + (h + 1) * HEAD_WIDTH])
    u = jax.nn.gelu(z[:, 2 * DIFF_WIDTH:2 * DIFF_WIDTH + GMLP_WIDTH])
    gv = jax.nn.gelu(z[:, 2 * DIFF_WIDTH + GMLP_WIDTH:])
    vn = _layer_norm(gv, lng_ref[...], lnb_ref[...])
    if vrows_ref:
        for g in range(GMLP_GROUPS):
            _head_store(vrows_ref[0], g, GMLP_GROUPS, vn[:, g * GMLP_GROUP_DIM:(g + 1) * GMLP_GROUP_DIM])
    vnb = vn.astype(BF16)
    row = lax.broadcasted_iota(jnp.int32, (GMLP_CHUNK, GMLP_CHUNK), 0)
    col = lax.broadcasted_iota(jnp.int32, (GMLP_CHUNK, GMLP_CHUNK), 1)
    for g in range(GMLP_GROUPS):
        cols = slice(g * GMLP_GROUP_DIM, (g + 1) * GMLP_GROUP_DIM)
        w = jnp.where(row >= col, ws_ref[g], 0.0).astype(BF16)
        bias = bst_ref[:, g:g + 1]
        for c in range(tile // GMLP_CHUNK):
            rows = slice(c * GMLP_CHUNK, (c + 1) * GMLP_CHUNK)
            mixed = _dot(w, vnb[rows, cols]) + bias
            g_ref[rows, cols] = (u[rows, cols] * mixed).astype(BF16)


def _head_rows(tile, heads, width):
    return _rows(_head_view_rows(tile, heads, width), LANES)


def _head_shape(rows, heads, width):
    return jax.ShapeDtypeStruct((_head_view_rows(rows, heads, width), LANES), F32)


def _inproj(x, w_rest, w_keys, ln_g, ln_b, ws, bs_t, batch, seq, keys_on_lanes, want_vrows):
    n, d = x.shape
    tile = min(ROW_TILE, n)
    if keys_on_lanes:
        per = seq // tile
        k_shape = jax.ShapeDtypeStruct((batch, DIFF_WIDTH, seq), F32)
        k_spec = pl.BlockSpec((None, DIFF_WIDTH, tile), lambda i: (i // per, 0, i % per))
    else:
        k_shape = jax.ShapeDtypeStruct((n, DIFF_WIDTH), F32)
        k_spec = _rows(tile, DIFF_WIDTH)
    out_shape = [jax.ShapeDtypeStruct((n, DIFF_WIDTH), BF16), k_shape,
                 _head_shape(n, DIFF_HEADS, HEAD_WIDTH),
                 jax.ShapeDtypeStruct((n, GMLP_WIDTH), BF16)]
    out_specs = [_rows(tile, DIFF_WIDTH), k_spec, _head_rows(tile, DIFF_HEADS, HEAD_WIDTH), _rows(tile, GMLP_WIDTH)]
    if want_vrows:
        out_shape.append(_head_shape(n, GMLP_GROUPS, GMLP_GROUP_DIM))
        out_specs.append(_head_rows(tile, GMLP_GROUPS, GMLP_GROUP_DIM))
    return pl.pallas_call(
        functools.partial(_inproj_kernel, keys_on_lanes=keys_on_lanes),
        grid=(n // tile,),
        in_specs=[_rows(tile, d), _resident(w_rest.shape), _resident(w_keys.shape), _resident((1, GMLP_WIDTH)),
                  _resident((1, GMLP_WIDTH)), _resident(ws.shape), _resident(bs_t.shape)],
        out_specs=out_specs,
        out_shape=out_shape,
        compiler_params=_params("parallel"),
        name="inproj",
    )(x, w_rest, w_keys, ln_g, ln_b, ws, bs_t)


def _head_scalars(head, lq1_ref, lk1_ref, lq2_ref, lk2_ref, lam_init):
    slope = jnp.exp2(jnp.zeros((1, 1), F32) - 8.0 * (head + 1).astype(F32) / DIFF_HEADS) * LOG2E
    lam = (jnp.exp(jnp.sum(lq1_ref[...] * lk1_ref[...], axis=-1, keepdims=True))
           - jnp.exp(jnp.sum(lq2_ref[...] * lk2_ref[...], axis=-1, keepdims=True)) + lam_init)
    return slope, lam


def _bias_lanes(slope, rows):
    lane = lax.broadcasted_iota(jnp.int32, (1, LANES), 1)
    out = jnp.zeros((1, LANES), F32)
    rest = slope
    for i in range(BIAS_TERMS):
        term = rest.astype(BF16).astype(F32)
        out = jnp.where(lane == 2 * i, term * POS_SPLIT, jnp.where(lane == 2 * i + 1, term, out))
        rest = rest - term
    return jnp.broadcast_to(out, (rows, LANES)).astype(BF16)


def _position_rows(n):
    assert n <= POS_SPLIT * 256
    row = lax.broadcasted_iota(jnp.int32, (LANES, n), 0)
    pos = lax.broadcasted_iota(jnp.int32, (LANES, n), 1)
    val = jnp.where(row % 2 == 0, pos // POS_SPLIT, pos % POS_SPLIT)
    return jnp.where(row < 2 * BIAS_TERMS, val, 0).astype(F32).astype(BF16)


def _stack_maps(q):
    lane = lax.broadcasted_iota(jnp.int32, q.shape, 1)
    zero = jnp.zeros_like(q)
    return jnp.concatenate([jnp.where(lane < HEAD_DIM, q, zero), jnp.where(lane >= HEAD_DIM, q, zero)], axis=0)


def _near_bias(slope, t, base_q, base_k, nk):
    r = lax.broadcasted_iota(jnp.int32, (t, nk), 0) + base_q
    c = lax.broadcasted_iota(jnp.int32, (t, nk), 1) + base_k
    bias = slope * (r - jnp.abs(r - c)).astype(F32)
    return jnp.where(c // CHUNK <= r // CHUNK, bias, MASKED)


def _diff_combine(parts, values, t, lam):
    m = functools.reduce(jnp.maximum, [jnp.max(s, axis=-1, keepdims=True) for s in parts])
    res = functools.reduce(jnp.add, [_dot(jnp.exp2(s - m).astype(BF16), v) for s, v in zip(parts, values)])
    out, denom = res[:, :HEAD_WIDTH], res[:, HEAD_WIDTH:HEAD_WIDTH + 1]
    return out[:t] * (1.0 / denom[:t]) - out[t:] * (lam / denom[t:])


def _with_ones(v):
    lane = lax.broadcasted_iota(jnp.int32, v.shape, 1)
    return jnp.concatenate([v, jnp.where(lane == 0, 1.0, 0.0).astype(v.dtype)], axis=1)


def _sub_norm(o, g, lam_init):
    return o * lax.rsqrt(jnp.mean(o * o, axis=-1, keepdims=True) + LN_EPS) * g * (1.0 - lam_init)


def _attn_prompt_kernel(q_ref, k_ref, v_ref, lq1_ref, lk1_ref, lq2_ref, lk2_ref, sg_ref, o_ref,
                        kb_ref, vb_ref, *, lam_init):
    seq = q_ref.shape[0]
    tq = ATTN_Q_TILE
    head = pl.program_id(1)
    slope, lam = _head_scalars(head, lq1_ref, lk1_ref, lq2_ref, lk2_ref, lam_init)
    kb_ref[:HEAD_WIDTH, :] = k_ref[...].astype(BF16)

    @pl.when((pl.program_id(0) == 0) & (head == 0))
    def _():
        kb_ref[HEAD_WIDTH:, :] = _position_rows(seq)
        vb_ref[...] = _with_ones(jnp.zeros((seq, HEAD_WIDTH), BF16))

    for h in range(DIFF_HEADS):
        @pl.when(head == h)
        def _():
            vb_ref[:, :HEAD_WIDTH] = _head_load(v_ref, h, DIFF_HEADS, seq, HEAD_WIDTH).astype(BF16)
    q_bias = _bias_lanes(slope, 2 * tq)
    r = lax.broadcasted_iota(jnp.int32, (tq, tq), 0)
    c = lax.broadcasted_iota(jnp.int32, (tq, tq), 1)
    diag = jnp.where(c // CHUNK <= r // CHUNK, -2.0 * slope * jnp.maximum(c - r, 0).astype(F32), MASKED)
    diag = jnp.concatenate([diag, diag], axis=0)
    for i in range(seq // tq):
        lo = i * tq
        qs = jnp.concatenate([_stack_maps(q_ref[lo:lo + tq, :]), q_bias], axis=1)
        parts = [_dot(qs, kb_ref[:, lo:lo + tq]) + diag]
        values = [vb_ref[lo:lo + tq, :]]
        if lo:
            parts.append(_dot(qs, kb_ref[:, 0:lo]))
            values.append(vb_ref[0:lo, :])
        out = _diff_combine(parts, values, tq, lam)
        o_ref[lo:lo + tq, :] = _sub_norm(out, sg_ref[...], lam_init).astype(BF16)


def _attn_prompt(q, k_t, v, lam_vecs, subln_g, batch, seq, lam_init):
    blk = pl.BlockSpec((seq, HEAD_WIDTH), lambda b, h: (b, h))
    vec = pl.BlockSpec((1, HEAD_DIM), lambda b, h: (0, 0))
    return pl.pallas_call(
        functools.partial(_attn_prompt_kernel, lam_init=lam_init),
        grid=(batch, DIFF_HEADS),
        in_specs=[blk, pl.BlockSpec((None, HEAD_WIDTH, seq), lambda b, h: (b, h, 0)),
                  pl.BlockSpec((_head_view_rows(seq, DIFF_HEADS, HEAD_WIDTH), LANES), lambda b, h: (b, 0)),
                  vec, vec, vec, vec, pl.BlockSpec((1, HEAD_WIDTH), lambda b, h: (0, 0))],
        out_specs=blk,
        out_shape=jax.ShapeDtypeStruct((batch * seq, DIFF_WIDTH), BF16),
        scratch_shapes=[pltpu.VMEM((HEAD_WIDTH + LANES, seq), BF16), pltpu.VMEM((seq, 2 * HEAD_WIDTH), BF16)],
        compiler_params=_params("arbitrary", "arbitrary"),
        name="attn_prompt",
    )(q, k_t, v, *lam_vecs, subln_g)


def _attn_sample_kernel(q_ref, kn_ref, vn_ref, kc_ref, vc_ref, lq1_ref, lk1_ref, lq2_ref, lk2_ref, sg_ref, o_ref,
                        *, lam_init):
    t = q_ref.shape[0]
    past = kc_ref.shape[1]
    far_pos = lax.broadcasted_iota(jnp.int32, (1, past), 1).astype(F32)
    for h in range(DIFF_HEADS):
        cols = slice(h * HEAD_WIDTH, (h + 1) * HEAD_WIDTH)
        slope, lam = _head_scalars(jnp.int32(h), lq1_ref, lk1_ref, lq2_ref, lk2_ref, lam_init)
        qs = _stack_maps(q_ref[:, cols])
        near = _near_bias(slope, t, past, past, t)
        parts = [_dot_nt(qs, kn_ref[:, cols].astype(BF16)) + jnp.concatenate([near, near], axis=0),
                 _dot(qs, kc_ref[cols, :].astype(BF16)) + slope * far_pos]
        values = [_with_ones(_head_load(vn_ref, h, DIFF_HEADS, t, HEAD_WIDTH).astype(BF16)),
                  _with_ones(_head_load(vc_ref, h, DIFF_HEADS, past, HEAD_WIDTH).astype(BF16))]
        out = _diff_combine(parts, values, t, lam)
        o_ref[:, cols] = _sub_norm(out, sg_ref[...], lam_init).astype(BF16)


def _attn_sample(q, k_new, v_new, k_cache_t, v_cache, lam_vecs, subln_g, batch, seq, lam_init):
    past = k_cache_t.shape[2]
    new = pl.BlockSpec((seq, DIFF_WIDTH), lambda b: (b, 0))
    vec = pl.BlockSpec((1, HEAD_DIM), lambda b: (0, 0))
    return pl.pallas_call(
        functools.partial(_attn_sample_kernel, lam_init=lam_init),
        grid=(batch,),
        in_specs=[new, new, _head_rows(seq, DIFF_HEADS, HEAD_WIDTH),
                  pl.BlockSpec((None, DIFF_WIDTH, past), lambda b: (b, 0, 0)),
                  _head_rows(past, DIFF_HEADS, HEAD_WIDTH),
                  vec, vec, vec, vec, pl.BlockSpec((1, HEAD_WIDTH), lambda b: (0, 0))],
        out_specs=new,
        out_shape=jax.ShapeDtypeStruct((batch * seq, DIFF_WIDTH), BF16),
        compiler_params=_params("parallel"),
        name="attn_sample",
    )(q, k_new, v_new, k_cache_t, v_cache, *lam_vecs, subln_g)


def _outproj_kernel(a_ref, g_ref, x_ref, wo_ref, lg_ref, lb_ref, wq_ref, x2_ref, q2_ref, *, alpha):
    half = a_ref.shape[1]
    mix = _dot(a_ref[...], wo_ref[:half, :]) + _dot(g_ref[...], wo_ref[half:, :])
    x2 = _layer_norm(alpha * x_ref[...] + mix, lg_ref[...], lb_ref[...])
    x2_ref[...] = x2
    q2_ref[...] = _dot(x2.astype(BF16), wq_ref[...]).astype(BF16)


def _outproj(a, g, x, w_out, ln_g, ln_b, wq, alpha):
    n, d = x.shape
    tile = min(ROW_TILE, n)
    return pl.pallas_call(
        functools.partial(_outproj_kernel, alpha=alpha),
        grid=(n // tile,),
        in_specs=[_rows(tile, a.shape[1]), _rows(tile, g.shape[1]), _rows(tile, d), _resident(w_out.shape),
                  _resident((1, d)), _resident((1, d)), _resident(wq.shape)],
        out_specs=[_rows(tile, d), _rows(tile, d)],
        out_shape=[jax.ShapeDtypeStruct((n, d), F32), jax.ShapeDtypeStruct((n, d), BF16)],
        compiler_params=_params("parallel"),
        name="outproj",
    )(a, g, x, w_out, ln_g, ln_b, wq)


def _cross_kernel(q_ref, mk_ref, mv_ref, o_ref, *, n_mem):
    hd = q_ref.shape[1] // MEM_HEADS
    scale = hd ** -0.5
    for h in range(MEM_HEADS):
        cols = slice(h * hd, (h + 1) * hd)
        mk = _head_load(mk_ref, h, MEM_HEADS, n_mem, hd).astype(BF16)
        mv = _head_load(mv_ref, h, MEM_HEADS, n_mem, hd).astype(BF16)
        s = _dot_nt(q_ref[:, cols], mk) * scale
        p = jnp.exp(s - jnp.max(s, axis=-1, keepdims=True))
        o = _dot(p.astype(BF16), mv)
        o_ref[:, cols] = (o * (1.0 / jnp.sum(p, axis=-1, keepdims=True))).astype(BF16)


def _cross(q2, mem_k, mem_v, batch, seq, n_mem):
    d = q2.shape[1]
    tq = min(CROSS_Q_TILE, seq)
    per = seq // tq
    qblk = pl.BlockSpec((tq, d), lambda b, i: (b * per + i, 0))
    mblk = pl.BlockSpec((_head_view_rows(n_mem, MEM_HEADS, d // MEM_HEADS), LANES), lambda b, i: (b, 0))
    return pl.pallas_call(
        functools.partial(_cross_kernel, n_mem=n_mem),
        grid=(batch, per),
        in_specs=[qblk, mblk, mblk],
        out_specs=qblk,
        out_shape=jax.ShapeDtypeStruct(q2.shape, BF16),
        compiler_params=_params("parallel", "parallel"),
        name="cross",
    )(q2, mem_k, mem_v)


def _memkv_kernel(m_ref, wk_ref, wv_ref, k_ref, v_ref):
    mb = m_ref[...].astype(BF16)
    k = _dot(mb, wk_ref[...])
    v = _dot(mb, wv_ref[...])
    hd = k.shape[1] // MEM_HEADS
    for h in range(MEM_HEADS):
        _head_store(k_ref, h, MEM_HEADS, k[:, h * hd:(h + 1) * hd])
        _head_store(v_ref, h, MEM_HEADS, v[:, h * hd:(h + 1) * hd])


def _memkv(mem, wk, wv):
    n, d = mem.shape
    hd = d // MEM_HEADS
    tile = min(ROW_TILE, n)
    return pl.pallas_call(
        _memkv_kernel,
        grid=(n // tile,),
        in_specs=[_rows(tile, d), _resident(wk.shape), _resident(wv.shape)],
        out_specs=[_head_rows(tile, MEM_HEADS, hd)] * 2,
        out_shape=[_head_shape(n, MEM_HEADS, hd)] * 2,
        compiler_params=_params("parallel"),
        name="memkv",
    )(mem, wk, wv)


def _gating_weights(ws, bs, seq):
    n = min(seq, GMLP_CHUNK)
    rep = GMLP_CHUNK // n
    w = ws[:, :n, :n]
    if rep > 1:
        eye = jnp.eye(rep, dtype=ws.dtype)
        w = jnp.einsum('ab,gts->gatbs', eye, w).reshape(GMLP_GROUPS, GMLP_CHUNK, GMLP_CHUNK)
    return w, jnp.tile(bs[:, :n], (1, rep)).T


def _trunk(x, batch, seq, mem_k, mem_v, cache, p, layer_idx, alpha, want_vrows):
    lam_init = 0.8 - 0.6 * math.exp(-0.3 * layer_idx)
    ws, bs_t = _gating_weights(p['gmlp_ws'], p['gmlp_bs'], seq)
    lam_vecs = (p['lambda_q1'], p['lambda_k1'], p['lambda_q2'], p['lambda_k2'])

    x1 = _ffn(x, p['ffn1_w_gu'], p['ffn1_w_down'], p['ln1_g'], p['ln1_b'], alpha)
    keys_on_lanes = cache is None
    outs = _inproj(x1, p['w_in_rest'], p['w_in_keys_t'] if keys_on_lanes else p['w_in_keys'], p['gmlp_ln_g'],
                   p['gmlp_ln_b'], ws, bs_t, batch, seq, keys_on_lanes, want_vrows)
    q, k, v, gated = outs[:4]
    if cache is None:
        a = _attn_prompt(q, k, v, lam_vecs, p['subln_g'], batch, seq, lam_init)
    else:
        a = _attn_sample(q, k, v, cache[0], cache[1], lam_vecs, p['subln_g'], batch, seq, lam_init)
    x2, q2 = _outproj(a, gated, x1, p['w_out'], p['ln2_g'], p['ln2_b'], p['cross_wq'], alpha)
    n_mem = mem_k.size // (batch * x.shape[1])
    o = _cross(q2, mem_k, mem_v, batch, seq, n_mem)
    y = _ffn(x2, p['ffn2_w_gu'], p['ffn2_w_down'], p['ln4_g'], p['ln4_b'], alpha,
             proj=(o, p['cross_wo'], p['ln3_g'], p['ln3_b']))
    return y, k, v, (outs[4] if want_vrows else None)


_MATRICES = ('ffn1_w_gu', 'ffn1_w_down', 'w_in', 'w_out', 'cross_wq', 'cross_wk', 'cross_wv', 'cross_wo',
             'ffn2_w_gu', 'ffn2_w_down')
_ROW_VECTORS = ('ln1_g', 'ln1_b', 'lambda_q1', 'lambda_k1', 'lambda_q2', 'lambda_k2', 'subln_g', 'gmlp_ln_g',
                'gmlp_ln_b', 'ln2_g', 'ln2_b', 'ln3_g', 'ln3_b', 'ln4_g', 'ln4_b')


def kernel(x_prompt, x_sample, cache_k, cache_v, cache_mem_k, cache_mem_v, mem_prompt, ffn1_w_gu, ffn1_w_down, ln1_g, ln1_b, w_in, lambda_q1, lambda_k1, lambda_q2, lambda_k2, subln_g, gmlp_ln_g, gmlp_ln_b, gmlp_ws, gmlp_bs, w_out, ln2_g, ln2_b, cross_wq, cross_wk, cross_wv, cross_wo, ln3_g, ln3_b, ffn2_w_gu, ffn2_w_down, ln4_g, ln4_b):
    weights = dict(ffn1_w_gu=ffn1_w_gu, ffn1_w_down=ffn1_w_down, ln1_g=ln1_g, ln1_b=ln1_b, w_in=w_in,
                   lambda_q1=lambda_q1, lambda_k1=lambda_k1, lambda_q2=lambda_q2, lambda_k2=lambda_k2,
                   subln_g=subln_g, gmlp_ln_g=gmlp_ln_g, gmlp_ln_b=gmlp_ln_b, gmlp_ws=gmlp_ws, gmlp_bs=gmlp_bs,
                   w_out=w_out, ln2_g=ln2_g, ln2_b=ln2_b, cross_wq=cross_wq, cross_wk=cross_wk, cross_wv=cross_wv,
                   cross_wo=cross_wo, ln3_g=ln3_g, ln3_b=ln3_b, ffn2_w_gu=ffn2_w_gu, ffn2_w_down=ffn2_w_down,
                   ln4_g=ln4_g, ln4_b=ln4_b)
    depth = w_in.shape[0]
    alpha = (2 * depth) ** 0.25
    bp, tp, d = x_prompt.shape
    bs_, ts, _ = x_sample.shape
    n_mem = mem_prompt.shape[1]
    past = cache_k.shape[2]
    mem_hd = d // MEM_HEADS

    xp =x_prompt.reshape(bp * tp, d)
    xs = x_sample.reshape(bs_ * ts, d)
    mem2d = mem_prompt.reshape(bp * n_mem, d)
    per_layer = [[] for _ in range(7)]
    for l in range(depth):
        p = {}
        for name, w in weights.items():
            w = w[l]
            if name in _MATRICES:
                w = w.astype(BF16)
            elif name in _ROW_VECTORS:
                w = w.reshape(1, -1)
            p[name] = w
        w_in_l = p.pop('w_in')
        p['w_in_rest'] = jnp.concatenate([w_in_l[:, :DIFF_WIDTH], w_in_l[:, 2 * DIFF_WIDTH:]], axis=1)
        p['w_in_keys'] = w_in_l[:, DIFF_WIDTH:2 * DIFF_WIDTH]
        p['w_in_keys_t'] = p['w_in_keys'].T
        mem_k, mem_v = _memkv(mem2d, p['cross_wk'], p['cross_wv'])
        xp, kp_t, vp, _ = _trunk(xp, bp, tp, mem_k, mem_v, None, p, l, alpha, False)
        cache = (jnp.transpose(cache_k[l], (0, 2, 3, 4, 1)).reshape(bs_, DIFF_WIDTH, past),
                 _to_head_view(cache_v[l]))
        xs, ks, vs, gvs = _trunk(xs, bs_, ts, _to_head_view(cache_mem_k[l]), _to_head_view(cache_mem_v[l]),
                                 cache, p, l, alpha, True)
        kp = jnp.transpose(kp_t.reshape(bp, DIFF_HEADS, 2, HEAD_DIM, tp), (0, 4, 1, 2, 3))
        vals = (kp, _from_head_view(vp, (bp, tp), DIFF_HEADS, HEAD_WIDTH),
                _from_head_view(mem_k, (bp, n_mem), MEM_HEADS, mem_hd),
                _from_head_view(mem_v, (bp, n_mem), MEM_HEADS, mem_hd), ks,
                _from_head_view(vs, (bs_, ts), DIFF_HEADS, HEAD_WIDTH),
                _from_head_view(gvs, (bs_, ts), GMLP_GROUPS, GMLP_GROUP_DIM))
        for acc, val in zip(per_layer, vals):
            acc.append(val)

    kp, vp, mk, mv, ks, vs, gvs = (vals[0][None] if depth == 1 else jnp.stack(vals) for vals in per_layer)
    return (xp.reshape(bp, tp, d), xs.reshape(bs_, ts, d),
            kp.reshape(depth, bp, tp, DIFF_HEADS, 2, HEAD_DIM),
            vp.reshape(depth, bp, tp, DIFF_HEADS, HEAD_WIDTH),
            mk.reshape(depth, bp, n_mem, MEM_HEADS, mem_hd),
            mv.reshape(depth, bp, n_mem, MEM_HEADS, mem_hd),
            ks.reshape(depth, bs_, ts, DIFF_HEADS, 2, HEAD_DIM),
            vs.reshape(depth, bs_, ts, DIFF_HEADS, HEAD_WIDTH),
            gvs.reshape(depth, bs_, ts, GMLP_GROUPS, GMLP_GROUP_DIM))
```

```python
import functools
import math

import jax
import jax.numpy as jnp
from jax import lax
from jax.experimental import pallas as pl
from jax.experimental.pallas import tpu as pltpu

F32 = jnp.float32
BF16 = jnp.bfloat16

CHUNK = 64
DIFF_HEADS = 4
HEAD_DIM = 64
HEAD_WIDTH = 2 * HEAD_DIM
DIFF_WIDTH = DIFF_HEADS * HEAD_WIDTH
GMLP_GROUPS = 4
GMLP_CHUNK = 128
GMLP_GROUP_DIM = 128
GMLP_WIDTH = GMLP_GROUPS * GMLP_GROUP_DIM
MEM_HEADS = 4
LN_EPS = 1e-5
MASKED = -1e30
LANES = 128
LOG2E = math.log2(math.e)
Q_SCALE = HEAD_DIM ** -0.5 * LOG2E
POS_SPLIT = 64
BIAS_TERMS = 3

ROW_TILE = 512
FF_CHUNK = 256
FFN_SUB_TILE = 256
ATTN_Q_TILE = 256
CROSS_Q_TILE = 512
VMEM_LIMIT_BYTES = 56 * 1024 * 1024


def _params(*semantics):
    return pltpu.CompilerParams(dimension_semantics=semantics, vmem_limit_bytes=VMEM_LIMIT_BYTES)


def _resident(shape):
    return pl.BlockSpec(shape, lambda *_: (0,) * len(shape), pipeline_mode=pl.Buffered(1))


def _rows(tile, width):
    return pl.BlockSpec((tile, width), lambda i: (i, 0))


def _layer_norm(h, g, b):
    mu = jnp.mean(h, axis=-1, keepdims=True)
    d = h - mu
    var = jnp.mean(d * d, axis=-1, keepdims=True)
    return d * lax.rsqrt(var + LN_EPS) * g + b


def _dot(a, b):
    return jnp.dot(a, b, preferred_element_type=F32)


def _dot_nt(a, b):
    return lax.dot_general(a, b, (((1,), (1,)), ((), ())), preferred_element_type=F32)


def _head_view_rows(rows, heads, width):
    return rows * heads * (width // LANES)


def _head_load(ref, h, heads, rows, width):
    tiles = width // LANES
    parts = [ref[pl.ds(j * heads + h, rows, stride=heads * tiles), :] for j in range(tiles)]
    return parts[0] if tiles == 1 else jnp.concatenate(parts, axis=1)


def _head_store(ref, h, heads, value, row0=0):
    rows, width = value.shape
    tiles = width // LANES
    for j in range(tiles):
        start = (row0 * tiles + j) * heads + h
        ref[pl.ds(start, rows, stride=heads * tiles), :] = value[:, j * LANES:(j + 1) * LANES]


def _to_head_view(x):
    *lead, heads, width = x.shape
    tiles = width // LANES
    x = x.reshape(*lead, heads, tiles, LANES)
    x = jnp.swapaxes(x, -3, -2)
    return x.reshape(-1, LANES)


def _from_head_view(x, lead, heads, width):
    tiles = width // LANES
    x = x.reshape(*lead, tiles, heads, LANES)
    x = jnp.swapaxes(x, -3, -2)
    return x.reshape(*lead, heads, width)


def _ffn_kernel(*refs, alpha, d_ff, with_proj):
    if with_proj:
        (o_ref, xin_ref, wo_ref, lpg_ref, lpb_ref,
         wgu_ref, wd_ref, g_ref, b_ref, out_ref, act_ref) = refs
    else:
        xin_ref, wgu_ref, wd_ref, g_ref, b_ref, out_ref, act_ref = refs
    tile = xin_ref.shape[0]
    sub = min(FFN_SUB_TILE, tile)
    subs = [slice(r * sub, (r + 1) * sub) for r in range(tile // sub)]
    xs = []
    for rows in subs:
        x = xin_ref[rows, :]
        if with_proj:
            x = _layer_norm(alpha * x + _dot(o_ref[rows, :], wo_ref[...]), lpg_ref[...], lpb_ref[...])
        xs.append(x)
    for rows, x in zip(subs, xs):
        xb = x.astype(BF16)
        for c in range(d_ff // FF_CHUNK):
            lo = c * FF_CHUNK
            gate = _dot(xb, wgu_ref[:, lo:lo + FF_CHUNK])
            up = _dot(xb, wgu_ref[:, d_ff + lo:d_ff + lo + FF_CHUNK])
            act_ref[rows, lo:lo + FF_CHUNK] = (gate * jax.nn.sigmoid(gate) * up).astype(BF16)
    ys = [_dot(act_ref[rows, :], wd_ref[...]) for rows in subs]
    for rows, x, y in zip(subs, xs, ys):
        out_ref[rows, :] = _layer_norm(alpha * x + 0.5 * y, g_ref[...], b_ref[...])


def _ffn(x, w_gu, w_down, ln_g, ln_b, alpha, proj=None):
    n, d = x.shape
    d_ff = w_down.shape[0]
    tile = min(ROW_TILE, n)
    in_specs, args = [], []
    if proj is not None:
        o, wo, pg, pb = proj
        in_specs += [_rows(tile, o.shape[1])]
        args += [o]
    in_specs += [_rows(tile, d)]
    args += [x]
    if proj is not None:
        in_specs += [_resident(wo.shape), _resident((1, d)), _resident((1, d))]
        args += [wo, pg, pb]
    in_specs += [_resident(w_gu.shape), _resident(w_down.shape), _resident((1, d)), _resident((1, d))]
    args += [w_gu, w_down, ln_g, ln_b]
    return pl.pallas_call(
        functools.partial(_ffn_kernel, alpha=alpha, d_ff=d_ff, with_proj=proj is not None),
        grid=(n // tile,),
        in_specs=in_specs,
        out_specs=_rows(tile, d),
        out_shape=jax.ShapeDtypeStruct((n, d), F32),
        scratch_shapes=[pltpu.VMEM((tile, d_ff), BF16)],
        compiler_params=_params("parallel"),
        name="ffn_proj" if proj is not None else "ffn",
    )(*args)


def _inproj_kernel(x_ref, w_ref, wk_ref, lng_ref, lnb_ref, ws_ref, bst_ref, q_ref, k_ref, v_ref, g_ref, *vrows_ref,
                   keys_on_lanes):
    tile = x_ref.shape[0]
    sub = min(FFN_SUB_TILE, tile)
    row = lax.broadcasted_iota(jnp.int32, (GMLP_CHUNK, GMLP_CHUNK), 0)
    col = lax.broadcasted_iota(jnp.int32, (GMLP_CHUNK, GMLP_CHUNK), 1)
    mix_w = [jnp.where(row >= col, ws_ref[g], 0.0).astype(BF16) for g in range(GMLP_GROUPS)]

    def project(r0):
        xb = x_ref[r0:r0 + sub, :].astype(BF16)
        z = _dot(xb, w_ref[:, :2 * DIFF_WIDTH])
        q_ref[r0:r0 + sub, :] = (z[:, :DIFF_WIDTH] * Q_SCALE).astype(BF16)
        if keys_on_lanes:
            k_ref[:, r0:r0 + sub] = _dot_nt(wk_ref[...], xb)
        else:
            k_ref[r0:r0 + sub, :] = _dot(xb, wk_ref[...])
        for h in range(DIFF_HEADS):
            _head_store(v_ref, h, DIFF_HEADS, z[:, DIFF_WIDTH + h * HEAD_WIDTH:DIFF_WIDTH + (h + 1) * HEAD_WIDTH], r0)

    def gate(r0, zg):
        u = jax.nn.gelu(zg[:, :GMLP_WIDTH])
        vn = _layer_norm(jax.nn.gelu(zg[:, GMLP_WIDTH:]), lng_ref[...], lnb_ref[...])
        if vrows_ref:
            for g in range(GMLP_GROUPS):
                _head_store(vrows_ref[0], g, GMLP_GROUPS, vn[:, g * GMLP_GROUP_DIM:(g + 1) * GMLP_GROUP_DIM], r0)
        vnb = vn.astype(BF16)
        for g in range(GMLP_GROUPS):
            cols = slice(g * GMLP_GROUP_DIM, (g + 1) * GMLP_GROUP_DIM)
            bias = bst_ref[:, g:g + 1]
            for c in range(sub // GMLP_CHUNK):
                rows = slice(c * GMLP_CHUNK, (c + 1) * GMLP_CHUNK)
                mixed = _dot(mix_w[g], vnb[rows, cols]) + bias
                g_ref[r0 + rows.start:r0 + rows.stop, cols] = (u[rows, cols] * mixed).astype(BF16)

    starts = list(range(0, tile, sub))
    zgs = {}
    for i, r0 in enumerate(starts):
        zgs[r0] = _dot(x_ref[r0:r0 + sub, :].astype(BF16), w_ref[:, 2 * DIFF_WIDTH:])
        if i:
            gate(starts[i - 1], zgs.pop(starts[i - 1]))
        project(r0)
    gate(starts[-1], zgs.pop(starts[-1]))


def _head_rows(tile, heads, width):
    return _rows(_head_view_rows(tile, heads, width), LANES)


def _head_shape(rows, heads, width):
    return jax.ShapeDtypeStruct((_head_view_rows(rows, heads, width), LANES), F32)


def _inproj(x, w_rest, w_keys, ln_g, ln_b, ws, bs_t, batch, seq, keys_on_lanes, want_vrows):
    n, d = x.shape
    tile = min(ROW_TILE, n)
    if keys_on_lanes:
        per = seq // tile
        k_shape = jax.ShapeDtypeStruct((batch, DIFF_WIDTH, seq), F32)
        k_spec = pl.BlockSpec((None, DIFF_WIDTH, tile), lambda i: (i // per, 0, i % per))
    else:
        k_shape = jax.ShapeDtypeStruct((n, DIFF_WIDTH), F32)
        k_spec = _rows(tile, DIFF_WIDTH)
    out_shape = [jax.ShapeDtypeStruct((n, DIFF_WIDTH), BF16), k_shape,
                 _head_shape(n, DIFF_HEADS, HEAD_WIDTH),
                 jax.ShapeDtypeStruct((n, GMLP_WIDTH), BF16)]
    out_specs = [_rows(tile, DIFF_WIDTH), k_spec, _head_rows(tile, DIFF_HEADS, HEAD_WIDTH), _rows(tile, GMLP_WIDTH)]
    if want_vrows:
        out_shape.append(_head_shape(n, GMLP_GROUPS, GMLP_GROUP_DIM))
        out_specs.append(_head_rows(tile, GMLP_GROUPS, GMLP_GROUP_DIM))
    return pl.pallas_call(
        functools.partial(_inproj_kernel, keys_on_lanes=keys_on_lanes),
        grid=(n // tile,),
        in_specs=[_rows(tile, d), _resident(w_rest.shape), _resident(w_keys.shape), _resident((1, GMLP_WIDTH)),
                  _resident((1, GMLP_WIDTH)), _resident(ws.shape), _resident(bs_t.shape)],
        out_specs=out_specs,
        out_shape=out_shape,
        compiler_params=_params("parallel"),
        name="inproj",
    )(x, w_rest, w_keys, ln_g, ln_b, ws, bs_t)


def _head_scalars(head, lq1_ref, lk1_ref, lq2_ref, lk2_ref, lam_init):
    slope = jnp.exp2(jnp.zeros((1, 1), F32) - 8.0 * (head + 1).astype(F32) / DIFF_HEADS) * LOG2E
    lam = (jnp.exp(jnp.sum(lq1_ref[...] * lk1_ref[...], axis=-1, keepdims=True))
           - jnp.exp(jnp.sum(lq2_ref[...] * lk2_ref[...], axis=-1, keepdims=True)) + lam_init)
    return slope, lam


def _bias_lanes(slope, rows):
    lane = lax.broadcasted_iota(jnp.int32, (1, LANES), 1)
    out = jnp.zeros((1, LANES), F32)
    rest = slope
    for i in range(BIAS_TERMS):
        term = rest.astype(BF16).astype(F32)
        out = jnp.where(lane == 2 * i, term * POS_SPLIT, jnp.where(lane == 2 * i + 1, term, out))
        rest = rest - term
    return jnp.broadcast_to(out, (rows, LANES)).astype(BF16)


def _position_rows(n):
    assert n <= POS_SPLIT * 256
    row = lax.broadcasted_iota(jnp.int32, (LANES, n), 0)
    pos = lax.broadcasted_iota(jnp.int32, (LANES, n), 1)
    val = jnp.where(row % 2 == 0, pos // POS_SPLIT, pos % POS_SPLIT)
    return jnp.where(row < 2 * BIAS_TERMS, val, 0).astype(F32).astype(BF16)


def _stack_maps(q):
    lane = lax.broadcasted_iota(jnp.int32, q.shape, 1)
    zero = jnp.zeros_like(q)
    return jnp.concatenate([jnp.where(lane < HEAD_DIM, q, zero), jnp.where(lane >= HEAD_DIM, q, zero)], axis=0)


def _near_bias(slope, t, base_q, base_k, nk):
    r = lax.broadcasted_iota(jnp.int32, (t, nk), 0) + base_q
    c = lax.broadcasted_iota(jnp.int32, (t, nk), 1) + base_k
    bias = slope * (r - jnp.abs(r - c)).astype(F32)
    return jnp.where(c // CHUNK <= r // CHUNK, bias, MASKED)


def _diff_combine(parts, values, t, lam):
    m = functools.reduce(jnp.maximum, [jnp.max(s, axis=-1, keepdims=True) for s in parts])
    res = functools.reduce(jnp.add, [_dot(jnp.exp2(s - m).astype(BF16), v) for s, v in zip(parts, values)])
    out, denom = res[:, :HEAD_WIDTH], res[:, HEAD_WIDTH:HEAD_WIDTH + 1]
    return out[:t] * (1.0 / denom[:t]) - out[t:] * (lam / denom[t:])


def _with_ones(v):
    lane = lax.broadcasted_iota(jnp.int32, v.shape, 1)
    return jnp.concatenate([v, jnp.where(lane == 0, 1.0, 0.0).astype(v.dtype)], axis=1)


def _sub_norm(o, g, lam_init):
    return o * lax.rsqrt(jnp.mean(o * o, axis=-1, keepdims=True) + LN_EPS) * g * (1.0 - lam_init)


def _attn_prompt_kernel(q_ref, k_ref, v_ref, lq1_ref, lk1_ref, lq2_ref, lk2_ref, sg_ref, o_ref,
                        kb_ref, vb_ref, *, lam_init):
    seq = q_ref.shape[0]
    tq = ATTN_Q_TILE
    head = pl.program_id(1)
    slope, lam = _head_scalars(head, lq1_ref, lk1_ref, lq2_ref, lk2_ref, lam_init)
    kb_ref[:HEAD_WIDTH, :] = k_ref[...].astype(BF16)

    @pl.when((pl.program_id(0) == 0) & (head == 0))
    def _():
        kb_ref[HEAD_WIDTH:, :] = _position_rows(seq)
        vb_ref[...] = _with_ones(jnp.zeros((seq, HEAD_WIDTH), BF16))

    for h in range(DIFF_HEADS):
        @pl.when(head == h)
        def _():
            vb_ref[:, :HEAD_WIDTH] = _head_load(v_ref, h, DIFF_HEADS, seq, HEAD_WIDTH).astype(BF16)
    q_bias = _bias_lanes(slope, 2 * tq)
    r = lax.broadcasted_iota(jnp.int32, (tq, tq), 0)
    c = lax.broadcasted_iota(jnp.int32, (tq, tq), 1)
    diag = jnp.where(c // CHUNK <= r // CHUNK, -2.0 * slope * jnp.maximum(c - r, 0).astype(F32), MASKED)
    diag = jnp.concatenate([diag, diag], axis=0)
    for i in range(seq // tq):
        lo = i * tq
        qs = jnp.concatenate([_stack_maps(q_ref[lo:lo + tq, :]), q_bias], axis=1)
        parts = [_dot(qs, kb_ref[:, lo:lo + tq]) + diag]
        values = [vb_ref[lo:lo + tq, :]]
        if lo:
            parts.append(_dot(qs, kb_ref[:, 0:lo]))
            values.append(vb_ref[0:lo, :])
        out = _diff_combine(parts, values, tq, lam)
        o_ref[lo:lo + tq, :] = _sub_norm(out, sg_ref[...], lam_init).astype(BF16)


def _attn_prompt(q, k_t, v, lam_vecs, subln_g, batch, seq, lam_init):
    blk = pl.BlockSpec((seq, HEAD_WIDTH), lambda b, h: (b, h))
    vec = pl.BlockSpec((1, HEAD_DIM), lambda b, h: (0, 0))
    return pl.pallas_call(
        functools.partial(_attn_prompt_kernel, lam_init=lam_init),
        grid=(batch, DIFF_HEADS),
        in_specs=[blk, pl.BlockSpec((None, HEAD_WIDTH, seq), lambda b, h: (b, h, 0)),
                  pl.BlockSpec((_head_view_rows(seq, DIFF_HEADS, HEAD_WIDTH), LANES), lambda b, h: (b, 0)),
                  vec, vec, vec, vec, pl.BlockSpec((1, HEAD_WIDTH), lambda b, h: (0, 0))],
        out_specs=blk,
        out_shape=jax.ShapeDtypeStruct((batch * seq, DIFF_WIDTH), BF16),
        scratch_shapes=[pltpu.VMEM((HEAD_WIDTH + LANES, seq), BF16), pltpu.VMEM((seq, 2 * HEAD_WIDTH), BF16)],
        compiler_params=_params("arbitrary", "arbitrary"),
        name="attn_prompt",
    )(q, k_t, v, *lam_vecs, subln_g)


def _attn_sample_kernel(q_ref, kn_ref, vn_ref, kc_ref, vc_ref, lq1_ref, lk1_ref, lq2_ref, lk2_ref, sg_ref, o_ref,
                        *, lam_init):
    t = q_ref.shape[0]
    past = kc_ref.shape[1]
    far_pos = lax.broadcasted_iota(jnp.int32, (1, past), 1).astype(F32)
    for h in range(DIFF_HEADS):
        cols = slice(h * HEAD_WIDTH, (h + 1) * HEAD_WIDTH)
        slope, lam = _head_scalars(jnp.int32(h), lq1_ref, lk1_ref, lq2_ref, lk2_ref, lam_init)
        qs = _stack_maps(q_ref[:, cols])
        near = _near_bias(slope, t, past, past, t)
        parts = [_dot_nt(qs, kn_ref[:, cols].astype(BF16)) + jnp.concatenate([near, near], axis=0),
                 _dot(qs, kc_ref[cols, :].astype(BF16)) + slope * far_pos]
        values = [_with_ones(_head_load(vn_ref, h, DIFF_HEADS, t, HEAD_WIDTH).astype(BF16)),
                  _with_ones(_head_load(vc_ref, h, DIFF_HEADS, past, HEAD_WIDTH).astype(BF16))]
        out = _diff_combine(parts, values, t, lam)
        o_ref[:, cols] = _sub_norm(out, sg_ref[...], lam_init).astype(BF16)


def _attn_sample(q, k_new, v_new, k_cache_t, v_cache, lam_vecs, subln_g, batch, seq, lam_init):
    past = k_cache_t.shape[2]
    new = pl.BlockSpec((seq, DIFF_WIDTH), lambda b: (b, 0))
    vec = pl.BlockSpec((1, HEAD_DIM), lambda b: (0, 0))
    return pl.pallas_call(
        functools.partial(_attn_sample_kernel, lam_init=lam_init),
        grid=(batch,),
        in_specs=[new, new, _head_rows(seq, DIFF_HEADS, HEAD_WIDTH),
                  pl.BlockSpec((None, DIFF_WIDTH, past), lambda b: (b, 0, 0)),
                  _head_rows(past, DIFF_HEADS, HEAD_WIDTH),
                  vec, vec, vec, vec, pl.BlockSpec((1, HEAD_WIDTH), lambda b: (0, 0))],
        out_specs=new,
        out_shape=jax.ShapeDtypeStruct((batch * seq, DIFF_WIDTH), BF16),
        compiler_params=_params("parallel"),
        name="attn_sample",
    )(q, k_new, v_new, k_cache_t, v_cache, *lam_vecs, subln_g)


def _outproj_kernel(a_ref, g_ref, x_ref, wo_ref, lg_ref, lb_ref, wq_ref, x2_ref, q2_ref, *, alpha):
    tile = x_ref.shape[0]
    sub = min(FFN_SUB_TILE, tile)
    subs = [slice(r * sub, (r + 1) * sub) for r in range(tile // sub)]
    mixes = [_dot(jnp.concatenate([a_ref[rows, :], g_ref[rows, :]], axis=1), wo_ref[...]) for rows in subs]
    for rows, mix in zip(subs, mixes):
        x2 = _layer_norm(alpha * x_ref[rows, :] + mix, lg_ref[...], lb_ref[...])
        x2_ref[rows, :] = x2
        q2_ref[rows, :] = _dot(x2.astype(BF16), wq_ref[...]).astype(BF16)


def _outproj(a, g, x, w_out, ln_g, ln_b, wq, alpha):
    n, d = x.shape
    tile = min(ROW_TILE, n)
    return pl.pallas_call(
        functools.partial(_outproj_kernel, alpha=alpha),
        grid=(n // tile,),
        in_specs=[_rows(tile, a.shape[1]), _rows(tile, g.shape[1]), _rows(tile, d), _resident(w_out.shape),
                  _resident((1, d)), _resident((1, d)), _resident(wq.shape)],
        out_specs=[_rows(tile, d), _rows(tile, d)],
        out_shape=[jax.ShapeDtypeStruct((n, d), F32), jax.ShapeDtypeStruct((n, d), BF16)],
        compiler_params=_params("parallel"),
        name="outproj",
    )(a, g, x, w_out, ln_g, ln_b, wq)


def _cross_kernel(q_ref, mk_ref, mv_ref, o_ref, *, n_mem):
    hd = q_ref.shape[1] // MEM_HEADS
    scale = hd ** -0.5
    for h in range(MEM_HEADS):
        cols = slice(h * hd, (h + 1) * hd)
        mk = _head_load(mk_ref, h, MEM_HEADS, n_mem, hd).astype(BF16)
        mv = _head_load(mv_ref, h, MEM_HEADS, n_mem, hd).astype(BF16)
        s = _dot_nt(q_ref[:, cols], mk) * scale
        p = jnp.exp(s - jnp.max(s, axis=-1, keepdims=True))
        o = _dot(p.astype(BF16), mv)
        o_ref[:, cols] = (o * (1.0 / jnp.sum(p, axis=-1, keepdims=True))).astype(BF16)


def _cross(q2, mem_k, mem_v, batch, seq, n_mem):
    d = q2.shape[1]
    tq = min(CROSS_Q_TILE, seq)
    per = seq // tq
    qblk = pl.BlockSpec((tq, d), lambda b, i: (b * per + i, 0))
    mblk = pl.BlockSpec((_head_view_rows(n_mem, MEM_HEADS, d // MEM_HEADS), LANES), lambda b, i: (b, 0))
    return pl.pallas_call(
        functools.partial(_cross_kernel, n_mem=n_mem),
        grid=(batch, per),
        in_specs=[qblk, mblk, mblk],
        out_specs=qblk,
        out_shape=jax.ShapeDtypeStruct(q2.shape, BF16),
        compiler_params=_params("parallel", "parallel"),
        name="cross",
    )(q2, mem_k, mem_v)


def _memkv_kernel(m_ref, wk_ref, wv_ref, k_ref, v_ref):
    mb = m_ref[...].astype(BF16)
    k = _dot(mb, wk_ref[...])
    v = _dot(mb, wv_ref[...])
    hd = k.shape[1] // MEM_HEADS
    for h in range(MEM_HEADS):
        _head_store(k_ref, h, MEM_HEADS, k[:, h * hd:(h + 1) * hd])
        _head_store(v_ref, h, MEM_HEADS, v[:, h * hd:(h + 1) * hd])


def _memkv(mem, wk, wv):
    n, d = mem.shape
    hd = d // MEM_HEADS
    tile = min(ROW_TILE, n)
    return pl.pallas_call(
        _memkv_kernel,
        grid=(n // tile,),
        in_specs=[_rows(tile, d), _resident(wk.shape), _resident(wv.shape)],
        out_specs=[_head_rows(tile, MEM_HEADS, hd)] * 2,
        out_shape=[_head_shape(n, MEM_HEADS, hd)] * 2,
        compiler_params=_params("parallel"),
        name="memkv",
    )(mem, wk, wv)


def _gating_weights(ws, bs, seq):
    n = min(seq, GMLP_CHUNK)
    rep = GMLP_CHUNK // n
    w = ws[:, :n, :n]
    if rep > 1:
        eye = jnp.eye(rep, dtype=ws.dtype)
        w = jnp.einsum('ab,gts->gatbs', eye, w).reshape(GMLP_GROUPS, GMLP_CHUNK, GMLP_CHUNK)
    return w, jnp.tile(bs[:, :n], (1, rep)).T


def _trunk(x, batch, seq, mem_k, mem_v, cache, p, layer_idx, alpha, want_vrows):
    lam_init = 0.8 - 0.6 * math.exp(-0.3 * layer_idx)
    ws, bs_t = _gating_weights(p['gmlp_ws'], p['gmlp_bs'], seq)
    lam_vecs = (p['lambda_q1'], p['lambda_k1'], p['lambda_q2'], p['lambda_k2'])

    x1 = _ffn(x, p['ffn1_w_gu'], p['ffn1_w_down'], p['ln1_g'], p['ln1_b'], alpha)
    keys_on_lanes = cache is None
    outs = _inproj(x1, p['w_in_rest'], p['w_in_keys_t'] if keys_on_lanes else p['w_in_keys'], p['gmlp_ln_g'],
                   p['gmlp_ln_b'], ws, bs_t, batch, seq, keys_on_lanes, want_vrows)
    q, k, v, gated = outs[:4]
    if cache is None:
        a = _attn_prompt(q, k, v, lam_vecs, p['subln_g'], batch, seq, lam_init)
    else:
        a = _attn_sample(q, k, v, cache[0], cache[1], lam_vecs, p['subln_g'], batch, seq, lam_init)
    x2, q2 = _outproj(a, gated, x1, p['w_out'], p['ln2_g'], p['ln2_b'], p['cross_wq'], alpha)
    n_mem = mem_k.size // (batch * x.shape[1])
    o = _cross(q2, mem_k, mem_v, batch, seq, n_mem)
    y = _ffn(x2, p['ffn2_w_gu'], p['ffn2_w_down'], p['ln4_g'], p['ln4_b'], alpha,
             proj=(o, p['cross_wo'], p['ln3_g'], p['ln3_b']))
    return y, k, v, (outs[4] if want_vrows else None)


_MATRICES = ('ffn1_w_gu', 'ffn1_w_down', 'w_in', 'w_out', 'cross_wq', 'cross_wk', 'cross_wv', 'cross_wo',
             'ffn2_w_gu', 'ffn2_w_down')
_ROW_VECTORS = ('ln1_g', 'ln1_b', 'lambda_q1', 'lambda_k1', 'lambda_q2', 'lambda_k2', 'subln_g', 'gmlp_ln_g',
                'gmlp_ln_b', 'ln2_g', 'ln2_b', 'ln3_g', 'ln3_b', 'ln4_g', 'ln4_b')


def kernel(x_prompt, x_sample, cache_k, cache_v, cache_mem_k, cache_mem_v, mem_prompt, ffn1_w_gu, ffn1_w_down, ln1_g, ln1_b, w_in, lambda_q1, lambda_k1, lambda_q2, lambda_k2, subln_g, gmlp_ln_g, gmlp_ln_b, gmlp_ws, gmlp_bs, w_out, ln2_g, ln2_b, cross_wq, cross_wk, cross_wv, cross_wo, ln3_g, ln3_b, ffn2_w_gu, ffn2_w_down, ln4_g, ln4_b):
    weights = dict(ffn1_w_gu=ffn1_w_gu, ffn1_w_down=ffn1_w_down, ln1_g=ln1_g, ln1_b=ln1_b, w_in=w_in,
                   lambda_q1=lambda_q1, lambda_k1=lambda_k1, lambda_q2=lambda_q2, lambda_k2=lambda_k2,
                   subln_g=subln_g, gmlp_ln_g=gmlp_ln_g, gmlp_ln_b=gmlp_ln_b, gmlp_ws=gmlp_ws, gmlp_bs=gmlp_bs,
                   w_out=w_out, ln2_g=ln2_g, ln2_b=ln2_b, cross_wq=cross_wq, cross_wk=cross_wk, cross_wv=cross_wv,
                   cross_wo=cross_wo, ln3_g=ln3_g, ln3_b=ln3_b, ffn2_w_gu=ffn2_w_gu, ffn2_w_down=ffn2_w_down,
                   ln4_g=ln4_g, ln4_b=ln4_b)
    depth = w_in.shape[0]
    alpha = (2 * depth) ** 0.25
    bp, tp, d = x_prompt.shape
    bs_, ts, _ = x_sample.shape
    n_mem = mem_prompt.shape[1]
    past = cache_k.shape[2]
    mem_hd = d // MEM_HEADS

    xp =x_prompt.reshape(bp * tp, d)
    xs = x_sample.reshape(bs_ * ts, d)
    mem2d = mem_prompt.reshape(bp * n_mem, d)
    per_layer = [[] for _ in range(7)]
    for l in range(depth):
        p = {}
        for name, w in weights.items():
            w = w[l]
            if name in _MATRICES:
                w = w.astype(BF16)
            elif name in _ROW_VECTORS:
                w = w.reshape(1, -1)
            p[name] = w
        w_in_l = p.pop('w_in')
        p['w_in_rest'] = jnp.concatenate([w_in_l[:, :DIFF_WIDTH], w_in_l[:, 2 * DIFF_WIDTH:]], axis=1)
        p['w_in_keys'] = w_in_l[:, DIFF_WIDTH:2 * DIFF_WIDTH]
        p['w_in_keys_t'] = p['w_in_keys'].T
        mem_k, mem_v = _memkv(mem2d, p['cross_wk'], p['cross_wv'])
        xp, kp_t, vp, _ = _trunk(xp, bp, tp, mem_k, mem_v, None, p, l, alpha, False)
        cache = (jnp.transpose(cache_k[l], (0, 2, 3, 4, 1)).reshape(bs_, DIFF_WIDTH, past),
                 _to_head_view(cache_v[l]))
        xs, ks, vs, gvs = _trunk(xs, bs_, ts, _to_head_view(cache_mem_k[l]), _to_head_view(cache_mem_v[l]),
                                 cache, p, l, alpha, True)
        kp = jnp.transpose(kp_t.reshape(bp, DIFF_HEADS, 2, HEAD_DIM, tp), (0, 4, 1, 2, 3))
        vals = (kp, _from_head_view(vp, (bp, tp), DIFF_HEADS, HEAD_WIDTH),
                _from_head_view(mem_k, (bp, n_mem), MEM_HEADS, mem_hd),
                _from_head_view(mem_v, (bp, n_mem), MEM_HEADS, mem_hd), ks,
                _from_head_view(vs, (bs_, ts), DIFF_HEADS, HEAD_WIDTH),
                _from_head_view(gvs, (bs_, ts), GMLP_GROUPS, GMLP_GROUP_DIM))
        for acc, val in zip(per_layer, vals):
            acc.append(val)

    kp, vp, mk, mv, ks, vs, gvs = (vals[0][None] if depth == 1 else jnp.stack(vals) for vals in per_layer)
    return (xp.reshape(bp, tp, d), xs.reshape(bs_, ts, d),
            kp.reshape(depth, bp, tp, DIFF_HEADS, 2, HEAD_DIM),
            vp.reshape(depth, bp, tp, DIFF_HEADS, HEAD_WIDTH),
            mk.reshape(depth, bp, n_mem, MEM_HEADS, mem_hd),
            mv.reshape(depth, bp, n_mem, MEM_HEADS, mem_hd),
            ks.reshape(depth, bs_, ts, DIFF_HEADS, 2, HEAD_DIM),
            vs.reshape(depth, bs_, ts, DIFF_HEADS, HEAD_WIDTH),
            gvs.reshape(depth, bs_, ts, GMLP_GROUPS, GMLP_GROUP_DIM))
```

```python
import functools
import math

import jax
import jax.numpy as jnp
from jax import lax
from jax.experimental import pallas as pl
from jax.experimental.pallas import tpu as pltpu

F32 = jnp.float32
BF16 = jnp.bfloat16

CHUNK = 64
DIFF_HEADS = 4
HEAD_DIM = 64
HEAD_WIDTH = 2 * HEAD_DIM
DIFF_WIDTH = DIFF_HEADS * HEAD_WIDTH
GMLP_GROUPS = 4
GMLP_CHUNK = 128
GMLP_GROUP_DIM = 128
GMLP_WIDTH = GMLP_GROUPS * GMLP_GROUP_DIM
MEM_HEADS = 4
LN_EPS = 1e-5
MASKED = -1e30
LANES = 128
LOG2E = math.log2(math.e)
Q_SCALE = HEAD_DIM ** -0.5 * LOG2E
POS_SPLIT = 64
BIAS_TERMS = 3

ROW_TILE = 512
FF_CHUNK = 256
FFN_SUB_TILE = 256
ATTN_Q_TILE = 256
CROSS_Q_TILE = 512
VMEM_LIMIT_BYTES = 56 * 1024 * 1024


def _params(*semantics):
    return pltpu.CompilerParams(dimension_semantics=semantics, vmem_limit_bytes=VMEM_LIMIT_BYTES)


def _resident(shape):
    return pl.BlockSpec(shape, lambda *_: (0,) * len(shape), pipeline_mode=pl.Buffered(1))


def _rows(tile, width):
    return pl.BlockSpec((tile, width), lambda i: (i, 0))


def _layer_norm(h, g, b):
    mu = jnp.mean(h, axis=-1, keepdims=True)
    d = h - mu
    var = jnp.mean(d * d, axis=-1, keepdims=True)
    return d * lax.rsqrt(var + LN_EPS) * g + b


def _dot(a, b):
    return jnp.dot(a, b, preferred_element_type=F32)


def _dot_nt(a, b):
    return lax.dot_general(a, b, (((1,), (1,)), ((), ())), preferred_element_type=F32)


def _head_view_rows(rows, heads, width):
    return rows * heads * (width // LANES)


def _head_load(ref, h, heads, rows, width):
    tiles = width // LANES
    parts = [ref[pl.ds(j * heads + h, rows, stride=heads * tiles), :] for j in range(tiles)]
    return parts[0] if tiles == 1 else jnp.concatenate(parts, axis=1)


def _head_store(ref, h, heads, value, row0=0):
    rows, width = value.shape
    tiles = width // LANES
    for j in range(tiles):
        start = (row0 * tiles + j) * heads + h
        ref[pl.ds(start, rows, stride=heads * tiles), :] = value[:, j * LANES:(j + 1) * LANES]


def _to_head_view(x):
    *lead, heads, width = x.shape
    tiles = width // LANES
    x = x.reshape(*lead, heads, tiles, LANES)
    x = jnp.swapaxes(x, -3, -2)
    return x.reshape(-1, LANES)


def _from_head_view(x, lead, heads, width):
    tiles = width // LANES
    x = x.reshape(*lead, tiles, heads, LANES)
    x = jnp.swapaxes(x, -3, -2)
    return x.reshape(*lead, heads, width)


def _ffn_kernel(*refs, alpha, d_ff, with_proj):
    if with_proj:
        (o_ref, xin_ref, wo_ref, lpg_ref, lpb_ref,
         wgu_ref, wd_ref, g_ref, b_ref, out_ref, act_ref) = refs
    else:
        xin_ref, wgu_ref, wd_ref, g_ref, b_ref, out_ref, act_ref = refs
    tile = xin_ref.shape[0]
    sub = min(FFN_SUB_TILE, tile)
    subs = [slice(r * sub, (r + 1) * sub) for r in range(tile // sub)]
    xs = []
    for rows in subs:
        x = xin_ref[rows, :]
        if with_proj:
            x = _layer_norm(alpha * x + _dot(o_ref[rows, :], wo_ref[...]), lpg_ref[...], lpb_ref[...])
        xs.append(x)
    for rows, x in zip(subs, xs):
        xb = x.astype(BF16)
        for c in range(d_ff // FF_CHUNK):
            lo = c * FF_CHUNK
            gate = _dot(xb, wgu_ref[:, lo:lo + FF_CHUNK])
            up = _dot(xb, wgu_ref[:, d_ff + lo:d_ff + lo + FF_CHUNK])
            act_ref[rows, lo:lo + FF_CHUNK] = (gate * jax.nn.sigmoid(gate) * up).astype(BF16)
    ys = [_dot(act_ref[rows, :], wd_ref[...]) for rows in subs]
    for rows, x, y in zip(subs, xs, ys):
        out_ref[rows, :] = _layer_norm(alpha * x + 0.5 * y, g_ref[...], b_ref[...])


def _ffn(x, w_gu, w_down, ln_g, ln_b, alpha, proj=None):
    n, d = x.shape
    d_ff = w_down.shape[0]
    tile = min(ROW_TILE, n)
    in_specs, args = [], []
    if proj is not None:
        o, wo, pg, pb = proj
        in_specs += [_rows(tile, o.shape[1])]
        args += [o]
    in_specs += [_rows(tile, d)]
    args += [x]
    if proj is not None:
        in_specs += [_resident(wo.shape), _resident((1, d)), _resident((1, d))]
        args += [wo, pg, pb]
    in_specs += [_resident(w_gu.shape), _resident(w_down.shape), _resident((1, d)), _resident((1, d))]
    args += [w_gu, w_down, ln_g, ln_b]
    return pl.pallas_call(
        functools.partial(_ffn_kernel, alpha=alpha, d_ff=d_ff, with_proj=proj is not None),
        grid=(n // tile,),
        in_specs=in_specs,
        out_specs=_rows(tile, d),
        out_shape=jax.ShapeDtypeStruct((n, d), F32),
        scratch_shapes=[pltpu.VMEM((tile, d_ff), BF16)],
        compiler_params=_params("parallel"),
        name="ffn_proj" if proj is not None else "ffn",
    )(*args)


def _inproj_kernel(x_ref, w_ref, wk_ref, lng_ref, lnb_ref, ws_ref, bst_ref, q_ref, k_ref, v_ref, g_ref, *vrows_ref,
                   keys_on_lanes):
    tile = x_ref.shape[0]
    sub = min(FFN_SUB_TILE, tile)
    row = lax.broadcasted_iota(jnp.int32, (GMLP_CHUNK, GMLP_CHUNK), 0)
    col = lax.broadcasted_iota(jnp.int32, (GMLP_CHUNK, GMLP_CHUNK), 1)
    mix_w = [jnp.where(row >= col, ws_ref[g], 0.0).astype(BF16) for g in range(GMLP_GROUPS)]

    def project(r0):
        xb = x_ref[r0:r0 + sub, :].astype(BF16)
        z = _dot(xb, w_ref[:, :2 * DIFF_WIDTH])
        q_ref[r0:r0 + sub, :] = (z[:, :DIFF_WIDTH] * Q_SCALE).astype(BF16)
        if keys_on_lanes:
            k_ref[:, r0:r0 + sub] = _dot_nt(wk_ref[...], xb)
        else:
            k_ref[r0:r0 + sub, :] = _dot(xb, wk_ref[...])
        for h in range(DIFF_HEADS):
            _head_store(v_ref, h, DIFF_HEADS, z[:, DIFF_WIDTH + h * HEAD_WIDTH:DIFF_WIDTH + (h + 1) * HEAD_WIDTH], r0)

    def gate(r0, zg):
        u = jax.nn.gelu(zg[:, :GMLP_WIDTH])
        vn = _layer_norm(jax.nn.gelu(zg[:, GMLP_WIDTH:]), lng_ref[...], lnb_ref[...])
        if vrows_ref:
            for g in range(GMLP_GROUPS):
                _head_store(vrows_ref[0], g, GMLP_GROUPS, vn[:, g * GMLP_GROUP_DIM:(g + 1) * GMLP_GROUP_DIM], r0)
        vnb = vn.astype(BF16)
        for g in range(GMLP_GROUPS):
            cols = slice(g * GMLP_GROUP_DIM, (g + 1) * GMLP_GROUP_DIM)
            bias = bst_ref[:, g:g + 1]
            for c in range(sub // GMLP_CHUNK):
                rows = slice(c * GMLP_CHUNK, (c + 1) * GMLP_CHUNK)
                mixed = _dot(mix_w[g], vnb[rows, cols]) + bias
                g_ref[r0 + rows.start:r0 + rows.stop, cols] = (u[rows, cols] * mixed).astype(BF16)

    starts = list(range(0, tile, sub))
    zgs = {}
    for i, r0 in enumerate(starts):
        zgs[r0] = _dot(x_ref[r0:r0 + sub, :].astype(BF16), w_ref[:, 2 * DIFF_WIDTH:])
        if i:
            gate(starts[i - 1], zgs.pop(starts[i - 1]))
        project(r0)
    gate(starts[-1], zgs.pop(starts[-1]))


def _head_rows(tile, heads, width):
    return _rows(_head_view_rows(tile, heads, width), LANES)


def _head_shape(rows, heads, width):
    return jax.ShapeDtypeStruct((_head_view_rows(rows, heads, width), LANES), F32)


def _inproj(x, w_rest, w_keys, ln_g, ln_b, ws, bs_t, batch, seq, keys_on_lanes, want_vrows):
    n, d = x.shape
    tile = min(ROW_TILE, n)
    if keys_on_lanes:
        per = seq // tile
        k_shape = jax.ShapeDtypeStruct((batch, DIFF_WIDTH, seq), F32)
        k_spec = pl.BlockSpec((None, DIFF_WIDTH, tile), lambda i: (i // per, 0, i % per))
    else:
        k_shape = jax.ShapeDtypeStruct((n, DIFF_WIDTH), F32)
        k_spec = _rows(tile, DIFF_WIDTH)
    out_shape = [jax.ShapeDtypeStruct((n, DIFF_WIDTH), BF16), k_shape,
                 _head_shape(n, DIFF_HEADS, HEAD_WIDTH),
                 jax.ShapeDtypeStruct((n, GMLP_WIDTH), BF16)]
    out_specs = [_rows(tile, DIFF_WIDTH), k_spec, _head_rows(tile, DIFF_HEADS, HEAD_WIDTH), _rows(tile, GMLP_WIDTH)]
    if want_vrows:
        out_shape.append(_head_shape(n, GMLP_GROUPS, GMLP_GROUP_DIM))
        out_specs.append(_head_rows(tile, GMLP_GROUPS, GMLP_GROUP_DIM))
    return pl.pallas_call(
        functools.partial(_inproj_kernel, keys_on_lanes=keys_on_lanes),
        grid=(n // tile,),
        in_specs=[_rows(tile, d), _resident(w_rest.shape), _resident(w_keys.shape), _resident((1, GMLP_WIDTH)),
                  _resident((1, GMLP_WIDTH)), _resident(ws.shape), _resident(bs_t.shape)],
        out_specs=out_specs,
        out_shape=out_shape,
        compiler_params=_params("parallel"),
        name="inproj",
    )(x, w_rest, w_keys, ln_g, ln_b, ws, bs_t)


def _head_scalars(head, lq1_ref, lk1_ref, lq2_ref, lk2_ref, lam_init):
    slope = jnp.exp2(jnp.zeros((1, 1), F32) - 8.0 * (head + 1).astype(F32) / DIFF_HEADS) * LOG2E
    lam = (jnp.exp(jnp.sum(lq1_ref[...] * lk1_ref[...], axis=-1, keepdims=True))
           - jnp.exp(jnp.sum(lq2_ref[...] * lk2_ref[...], axis=-1, keepdims=True)) + lam_init)
    return slope, lam


def _bias_lanes(slope, rows):
    lane = lax.broadcasted_iota(jnp.int32, (1, LANES), 1)
    out = jnp.zeros((1, LANES), F32)
    rest = slope
    for i in range(BIAS_TERMS):
        term = rest.astype(BF16).astype(F32)
        out = jnp.where(lane == 2 * i, term * POS_SPLIT, jnp.where(lane == 2 * i + 1, term, out))
        rest = rest - term
    return jnp.broadcast_to(out, (rows, LANES)).astype(BF16)


def _position_rows(n):
    assert n <= POS_SPLIT * 256
    row = lax.broadcasted_iota(jnp.int32, (LANES, n), 0)
    pos = lax.broadcasted_iota(jnp.int32, (LANES, n), 1)
    val = jnp.where(row % 2 == 0, pos // POS_SPLIT, pos % POS_SPLIT)
    return jnp.where(row < 2 * BIAS_TERMS, val, 0).astype(F32).astype(BF16)


def _stack_maps(q):
    lane = lax.broadcasted_iota(jnp.int32, q.shape, 1)
    zero = jnp.zeros_like(q)
    return jnp.concatenate([jnp.where(lane < HEAD_DIM, q, zero), jnp.where(lane >= HEAD_DIM, q, zero)], axis=0)


def _near_bias(slope, t, base_q, base_k, nk):
    r = lax.broadcasted_iota(jnp.int32, (t, nk), 0) + base_q
    c = lax.broadcasted_iota(jnp.int32, (t, nk), 1) + base_k
    bias = slope * (r - jnp.abs(r - c)).astype(F32)
    return jnp.where(c // CHUNK <= r // CHUNK, bias, MASKED)


def _diff_combine(parts, values, t, lam):
    m = functools.reduce(jnp.maximum, [jnp.max(s, axis=-1, keepdims=True) for s in parts])
    res = functools.reduce(jnp.add, [_dot(jnp.exp2(s - m).astype(BF16), v) for s, v in zip(parts, values)])
    out, denom = res[:, :HEAD_WIDTH], res[:, HEAD_WIDTH:HEAD_WIDTH + 1]
    return out[:t] * (1.0 / denom[:t]) - out[t:] * (lam / denom[t:])


def _with_ones(v):
    lane = lax.broadcasted_iota(jnp.int32, v.shape, 1)
    return jnp.concatenate([v, jnp.where(lane == 0, 1.0, 0.0).astype(v.dtype)], axis=1)


def _sub_norm(o, g, lam_init):
    return o * lax.rsqrt(jnp.mean(o * o, axis=-1, keepdims=True) + LN_EPS) * g * (1.0 - lam_init)


def _attn_prompt_kernel(q_ref, k_ref, v_ref, lq1_ref, lk1_ref, lq2_ref, lk2_ref, sg_ref, o_ref,
                        kb_ref, vb_ref, *, lam_init):
    seq = q_ref.shape[0]
    tq = ATTN_Q_TILE
    head = pl.program_id(1)
    slope, lam = _head_scalars(head, lq1_ref, lk1_ref, lq2_ref, lk2_ref, lam_init)
    kb_ref[:HEAD_WIDTH, :] = k_ref[...].astype(BF16)

    @pl.when((pl.program_id(0) == 0) & (head == 0))
    def _():
        kb_ref[HEAD_WIDTH:, :] = _position_rows(seq)
        vb_ref[...] = _with_ones(jnp.zeros((seq, HEAD_WIDTH), BF16))

    for h in range(DIFF_HEADS):
        @pl.when(head == h)
        def _():
            vb_ref[:, :HEAD_WIDTH] = _head_load(v_ref, h, DIFF_HEADS, seq, HEAD_WIDTH).astype(BF16)
    q_bias = _bias_lanes(slope, 2 * tq)
    r = lax.broadcasted_iota(jnp.int32, (tq, tq), 0)
    c = lax.broadcasted_iota(jnp.int32, (tq, tq), 1)
    diag = jnp.where(c // CHUNK <= r // CHUNK, -2.0 * slope * jnp.maximum(c - r, 0).astype(F32), MASKED)
    diag = jnp.concatenate([diag, diag], axis=0)
    def scores(lo):
        qs = jnp.concatenate([_stack_maps(q_ref[lo:lo + tq, :]), q_bias], axis=1)
        parts = [_dot(qs, kb_ref[:, lo:lo + tq]) + diag]
        if lo:
            parts.append(_dot(qs, kb_ref[:, 0:lo]))
        return parts

    starts = list(range(0, seq, tq))
    parts = scores(starts[0])
    for i, lo in enumerate(starts):
        next_parts = scores(starts[i + 1]) if i + 1 < len(starts) else None
        values = [vb_ref[lo:lo + tq, :]] + ([vb_ref[0:lo, :]] if lo else [])
        out = _diff_combine(parts, values, tq, lam)
        o_ref[lo:lo + tq, :] = _sub_norm(out, sg_ref[...], lam_init).astype(BF16)
        parts = next_parts


def _attn_prompt(q, k_t, v, lam_vecs, subln_g, batch, seq, lam_init):
    blk = pl.BlockSpec((seq, HEAD_WIDTH), lambda b, h: (b, h))
    vec = pl.BlockSpec((1, HEAD_DIM), lambda b, h: (0, 0))
    return pl.pallas_call(
        functools.partial(_attn_prompt_kernel, lam_init=lam_init),
        grid=(batch, DIFF_HEADS),
        in_specs=[blk, pl.BlockSpec((None, HEAD_WIDTH, seq), lambda b, h: (b, h, 0)),
                  pl.BlockSpec((_head_view_rows(seq, DIFF_HEADS, HEAD_WIDTH), LANES), lambda b, h: (b, 0)),
                  vec, vec, vec, vec, pl.BlockSpec((1, HEAD_WIDTH), lambda b, h: (0, 0))],
        out_specs=blk,
        out_shape=jax.ShapeDtypeStruct((batch * seq, DIFF_WIDTH), BF16),
        scratch_shapes=[pltpu.VMEM((HEAD_WIDTH + LANES, seq), BF16), pltpu.VMEM((seq, 2 * HEAD_WIDTH), BF16)],
        compiler_params=_params("arbitrary", "arbitrary"),
        name="attn_prompt",
    )(q, k_t, v, *lam_vecs, subln_g)


def _attn_sample_kernel(q_ref, kn_ref, vn_ref, kc_ref, vc_ref, lq1_ref, lk1_ref, lq2_ref, lk2_ref, sg_ref, o_ref,
                        *, lam_init):
    t = q_ref.shape[0]
    past = kc_ref.shape[1]
    far_pos = lax.broadcasted_iota(jnp.int32, (1, past), 1).astype(F32)
    for h in range(DIFF_HEADS):
        cols = slice(h * HEAD_WIDTH, (h + 1) * HEAD_WIDTH)
        slope, lam = _head_scalars(jnp.int32(h), lq1_ref, lk1_ref, lq2_ref, lk2_ref, lam_init)
        qs = _stack_maps(q_ref[:, cols])
        near = _near_bias(slope, t, past, past, t)
        parts = [_dot_nt(qs, kn_ref[:, cols].astype(BF16)) + jnp.concatenate([near, near], axis=0),
                 _dot(qs, kc_ref[cols, :].astype(BF16)) + slope * far_pos]
        values = [_with_ones(_head_load(vn_ref, h, DIFF_HEADS, t, HEAD_WIDTH).astype(BF16)),
                  _with_ones(_head_load(vc_ref, h, DIFF_HEADS, past, HEAD_WIDTH).astype(BF16))]
        out = _diff_combine(parts, values, t, lam)
        o_ref[:, cols] = _sub_norm(out, sg_ref[...], lam_init).astype(BF16)


def _attn_sample(q, k_new, v_new, k_cache_t, v_cache, lam_vecs, subln_g, batch, seq, lam_init):
    past = k_cache_t.shape[2]
    new = pl.BlockSpec((seq, DIFF_WIDTH), lambda b: (b, 0))
    vec = pl.BlockSpec((1, HEAD_DIM), lambda b: (0, 0))
    return pl.pallas_call(
        functools.partial(_attn_sample_kernel, lam_init=lam_init),
        grid=(batch,),
        in_specs=[new, new, _head_rows(seq, DIFF_HEADS, HEAD_WIDTH),
                  pl.BlockSpec((None, DIFF_WIDTH, past), lambda b: (b, 0, 0)),
                  _head_rows(past, DIFF_HEADS, HEAD_WIDTH),
                  vec, vec, vec, vec, pl.BlockSpec((1, HEAD_WIDTH), lambda b: (0, 0))],
        out_specs=new,
        out_shape=jax.ShapeDtypeStruct((batch * seq, DIFF_WIDTH), BF16),
        compiler_params=_params("parallel"),
        name="attn_sample",
    )(q, k_new, v_new, k_cache_t, v_cache, *lam_vecs, subln_g)


def _outproj_kernel(a_ref, g_ref, x_ref, wo_ref, lg_ref, lb_ref, wq_ref, x2_ref, q2_ref, *, alpha):
    tile = x_ref.shape[0]
    sub = min(FFN_SUB_TILE, tile)
    subs = [slice(r * sub, (r + 1) * sub) for r in range(tile // sub)]
    mixes = [_dot(jnp.concatenate([a_ref[rows, :], g_ref[rows, :]], axis=1), wo_ref[...]) for rows in subs]
    for rows, mix in zip(subs, mixes):
        x2 = _layer_norm(alpha * x_ref[rows, :] + mix, lg_ref[...], lb_ref[...])
        x2_ref[rows, :] = x2
        q2_ref[rows, :] = _dot(x2.astype(BF16), wq_ref[...]).astype(BF16)


def _outproj(a, g, x, w_out, ln_g, ln_b, wq, alpha):
    n, d = x.shape
    tile = min(ROW_TILE, n)
    return pl.pallas_call(
        functools.partial(_outproj_kernel, alpha=alpha),
        grid=(n // tile,),
        in_specs=[_rows(tile, a.shape[1]), _rows(tile, g.shape[1]), _rows(tile, d), _resident(w_out.shape),
                  _resident((1, d)), _resident((1, d)), _resident(wq.shape)],
        out_specs=[_rows(tile, d), _rows(tile, d)],
        out_shape=[jax.ShapeDtypeStruct((n, d), F32), jax.ShapeDtypeStruct((n, d), BF16)],
        compiler_params=_params("parallel"),
        name="outproj",
    )(a, g, x, w_out, ln_g, ln_b, wq)


def _cross_kernel(q_ref, mk_ref, mv_ref, o_ref, *, n_mem):
    hd = q_ref.shape[1] // MEM_HEADS
    scale = hd ** -0.5
    for h in range(MEM_HEADS):
        cols = slice(h * hd, (h + 1) * hd)
        mk = _head_load(mk_ref, h, MEM_HEADS, n_mem, hd).astype(BF16)
        mv = _head_load(mv_ref, h, MEM_HEADS, n_mem, hd).astype(BF16)
        s = _dot_nt(q_ref[:, cols], mk) * scale
        p = jnp.exp(s - jnp.max(s, axis=-1, keepdims=True))
        o = _dot(p.astype(BF16), mv)
        o_ref[:, cols] = (o * (1.0 / jnp.sum(p, axis=-1, keepdims=True))).astype(BF16)


def _cross(q2, mem_k, mem_v, batch, seq, n_mem):
    d = q2.shape[1]
    tq = min(CROSS_Q_TILE, seq)
    per = seq // tq
    qblk = pl.BlockSpec((tq, d), lambda b, i: (b * per + i, 0))
    mblk = pl.BlockSpec((_head_view_rows(n_mem, MEM_HEADS, d // MEM_HEADS), LANES), lambda b, i: (b, 0))
    return pl.pallas_call(
        functools.partial(_cross_kernel, n_mem=n_mem),
        grid=(batch, per),
        in_specs=[qblk, mblk, mblk],
        out_specs=qblk,
        out_shape=jax.ShapeDtypeStruct(q2.shape, BF16),
        compiler_params=_params("parallel", "parallel"),
        name="cross",
    )(q2, mem_k, mem_v)


def _memkv_kernel(m_ref, wk_ref, wv_ref, k_ref, v_ref):
    mb = m_ref[...].astype(BF16)
    k = _dot(mb, wk_ref[...])
    v = _dot(mb, wv_ref[...])
    hd = k.shape[1] // MEM_HEADS
    for h in range(MEM_HEADS):
        _head_store(k_ref, h, MEM_HEADS, k[:, h * hd:(h + 1) * hd])
        _head_store(v_ref, h, MEM_HEADS, v[:, h * hd:(h + 1) * hd])


def _memkv(mem, wk, wv):
    n, d = mem.shape
    hd = d // MEM_HEADS
    tile = min(ROW_TILE, n)
    return pl.pallas_call(
        _memkv_kernel,
        grid=(n // tile,),
        in_specs=[_rows(tile, d), _resident(wk.shape), _resident(wv.shape)],
        out_specs=[_head_rows(tile, MEM_HEADS, hd)] * 2,
        out_shape=[_head_shape(n, MEM_HEADS, hd)] * 2,
        compiler_params=_params("parallel"),
        name="memkv",
    )(mem, wk, wv)


def _gating_weights(ws, bs, seq):
    n = min(seq, GMLP_CHUNK)
    rep = GMLP_CHUNK // n
    w = ws[:, :n, :n]
    if rep > 1:
        eye = jnp.eye(rep, dtype=ws.dtype)
        w = jnp.einsum('ab,gts->gatbs', eye, w).reshape(GMLP_GROUPS, GMLP_CHUNK, GMLP_CHUNK)
    return w, jnp.tile(bs[:, :n], (1, rep)).T


def _trunk(x, batch, seq, mem_k, mem_v, cache, p, layer_idx, alpha, want_vrows):
    lam_init = 0.8 - 0.6 * math.exp(-0.3 * layer_idx)
    ws, bs_t = _gating_weights(p['gmlp_ws'], p['gmlp_bs'], seq)
    lam_vecs = (p['lambda_q1'], p['lambda_k1'], p['lambda_q2'], p['lambda_k2'])

    x1 = _ffn(x, p['ffn1_w_gu'], p['ffn1_w_down'], p['ln1_g'], p['ln1_b'], alpha)
    keys_on_lanes = cache is None
    outs = _inproj(x1, p['w_in_rest'], p['w_in_keys_t'] if keys_on_lanes else p['w_in_keys'], p['gmlp_ln_g'],
                   p['gmlp_ln_b'], ws, bs_t, batch, seq, keys_on_lanes, want_vrows)
    q, k, v, gated = outs[:4]
    if cache is None:
        a = _attn_prompt(q, k, v, lam_vecs, p['subln_g'], batch, seq, lam_init)
    else:
        a = _attn_sample(q, k, v, cache[0], cache[1], lam_vecs, p['subln_g'], batch, seq, lam_init)
    x2, q2 = _outproj(a, gated, x1, p['w_out'], p['ln2_g'], p['ln2_b'], p['cross_wq'], alpha)
    n_mem = mem_k.size // (batch * x.shape[1])
    o = _cross(q2, mem_k, mem_v, batch, seq, n_mem)
    y = _ffn(x2, p['ffn2_w_gu'], p['ffn2_w_down'], p['ln4_g'], p['ln4_b'], alpha,
             proj=(o, p['cross_wo'], p['ln3_g'], p['ln3_b']))
    return y, k, v, (outs[4] if want_vrows else None)


_MATRICES = ('ffn1_w_gu', 'ffn1_w_down', 'w_in', 'w_out', 'cross_wq', 'cross_wk', 'cross_wv', 'cross_wo',
             'ffn2_w_gu', 'ffn2_w_down')
_ROW_VECTORS = ('ln1_g', 'ln1_b', 'lambda_q1', 'lambda_k1', 'lambda_q2', 'lambda_k2', 'subln_g', 'gmlp_ln_g',
                'gmlp_ln_b', 'ln2_g', 'ln2_b', 'ln3_g', 'ln3_b', 'ln4_g', 'ln4_b')


def kernel(x_prompt, x_sample, cache_k, cache_v, cache_mem_k, cache_mem_v, mem_prompt, ffn1_w_gu, ffn1_w_down, ln1_g, ln1_b, w_in, lambda_q1, lambda_k1, lambda_q2, lambda_k2, subln_g, gmlp_ln_g, gmlp_ln_b, gmlp_ws, gmlp_bs, w_out, ln2_g, ln2_b, cross_wq, cross_wk, cross_wv, cross_wo, ln3_g, ln3_b, ffn2_w_gu, ffn2_w_down, ln4_g, ln4_b):
    weights = dict(ffn1_w_gu=ffn1_w_gu, ffn1_w_down=ffn1_w_down, ln1_g=ln1_g, ln1_b=ln1_b, w_in=w_in,
                   lambda_q1=lambda_q1, lambda_k1=lambda_k1, lambda_q2=lambda_q2, lambda_k2=lambda_k2,
                   subln_g=subln_g, gmlp_ln_g=gmlp_ln_g, gmlp_ln_b=gmlp_ln_b, gmlp_ws=gmlp_ws, gmlp_bs=gmlp_bs,
                   w_out=w_out, ln2_g=ln2_g, ln2_b=ln2_b, cross_wq=cross_wq, cross_wk=cross_wk, cross_wv=cross_wv,
                   cross_wo=cross_wo, ln3_g=ln3_g, ln3_b=ln3_b, ffn2_w_gu=ffn2_w_gu, ffn2_w_down=ffn2_w_down,
                   ln4_g=ln4_g, ln4_b=ln4_b)
    depth = w_in.shape[0]
    alpha = (2 * depth) ** 0.25
    bp, tp, d = x_prompt.shape
    bs_, ts, _ = x_sample.shape
    n_mem = mem_prompt.shape[1]
    past = cache_k.shape[2]
    mem_hd = d // MEM_HEADS

    xp =x_prompt.reshape(bp * tp, d)
    xs = x_sample.reshape(bs_ * ts, d)
    mem2d = mem_prompt.reshape(bp * n_mem, d)
    per_layer = [[] for _ in range(7)]
    for l in range(depth):
        p = {}
        for name, w in weights.items():
            w = w[l]
            if name in _MATRICES:
                w = w.astype(BF16)
            elif name in _ROW_VECTORS:
                w = w.reshape(1, -1)
            p[name] = w
        w_in_l = p.pop('w_in')
        p['w_in_rest'] = jnp.concatenate([w_in_l[:, :DIFF_WIDTH], w_in_l[:, 2 * DIFF_WIDTH:]], axis=1)
        p['w_in_keys'] = w_in_l[:, DIFF_WIDTH:2 * DIFF_WIDTH]
        p['w_in_keys_t'] = p['w_in_keys'].T
        mem_k, mem_v = _memkv(mem2d, p['cross_wk'], p['cross_wv'])
        xp, kp_t, vp, _ = _trunk(xp, bp, tp, mem_k, mem_v, None, p, l, alpha, False)
        cache = (jnp.transpose(cache_k[l], (0, 2, 3, 4, 1)).reshape(bs_, DIFF_WIDTH, past),
                 _to_head_view(cache_v[l]))
        xs, ks, vs, gvs = _trunk(xs, bs_, ts, _to_head_view(cache_mem_k[l]), _to_head_view(cache_mem_v[l]),
                                 cache, p, l, alpha, True)
        kp = jnp.transpose(kp_t.reshape(bp, DIFF_HEADS, 2, HEAD_DIM, tp), (0, 4, 1, 2, 3))
        vals = (kp, _from_head_view(vp, (bp, tp), DIFF_HEADS, HEAD_WIDTH),
                _from_head_view(mem_k, (bp, n_mem), MEM_HEADS, mem_hd),
                _from_head_view(mem_v, (bp, n_mem), MEM_HEADS, mem_hd), ks,
                _from_head_view(vs, (bs_, ts), DIFF_HEADS, HEAD_WIDTH),
                _from_head_view(gvs, (bs_, ts), GMLP_GROUPS, GMLP_GROUP_DIM))
        for acc, val in zip(per_layer, vals):
            acc.append(val)

    kp, vp, mk, mv, ks, vs, gvs = (vals[0][None] if depth == 1 else jnp.stack(vals) for vals in per_layer)
    return (xp.reshape(bp, tp, d), xs.reshape(bs_, ts, d),
            kp.reshape(depth, bp, tp, DIFF_HEADS, 2, HEAD_DIM),
            vp.reshape(depth, bp, tp, DIFF_HEADS, HEAD_WIDTH),
            mk.reshape(depth, bp, n_mem, MEM_HEADS, mem_hd),
            mv.reshape(depth, bp, n_mem, MEM_HEADS, mem_hd),
            ks.reshape(depth, bs_, ts, DIFF_HEADS, 2, HEAD_DIM),
            vs.reshape(depth, bs_, ts, DIFF_HEADS, HEAD_WIDTH),
            gvs.reshape(depth, bs_, ts, GMLP_GROUPS, GMLP_GROUP_DIM))
```

```python
import functools
import math

import jax
import jax.numpy as jnp
from jax import lax
from jax.experimental import pallas as pl
from jax.experimental.pallas import tpu as pltpu

F32 = jnp.float32
BF16 = jnp.bfloat16

CHUNK = 64
DIFF_HEADS = 4
HEAD_DIM = 64
HEAD_WIDTH = 2 * HEAD_DIM
DIFF_WIDTH = DIFF_HEADS * HEAD_WIDTH
GMLP_GROUPS = 4
GMLP_CHUNK = 128
GMLP_GROUP_DIM = 128
GMLP_WIDTH = GMLP_GROUPS * GMLP_GROUP_DIM
MEM_HEADS = 4
LN_EPS = 1e-5
MASKED = -1e30
LANES = 128
LOG2E = math.log2(math.e)
Q_SCALE = HEAD_DIM ** -0.5 * LOG2E
POS_SPLIT = 64
BIAS_TERMS = 3

ROW_TILE = 1024
FF_CHUNK = 256
FFN_SUB_TILE = 256
ATTN_Q_TILE = 256
CROSS_Q_TILE = 512
VMEM_LIMIT_BYTES = 56 * 1024 * 1024


def _params(*semantics):
    return pltpu.CompilerParams(dimension_semantics=semantics, vmem_limit_bytes=VMEM_LIMIT_BYTES)


def _resident(shape):
    return pl.BlockSpec(shape, lambda *_: (0,) * len(shape), pipeline_mode=pl.Buffered(1))


def _rows(tile, width):
    return pl.BlockSpec((tile, width), lambda i: (i, 0))


def _layer_norm(h, g, b):
    mu = jnp.mean(h, axis=-1, keepdims=True)
    d = h - mu
    var = jnp.mean(d * d, axis=-1, keepdims=True)
    return d * lax.rsqrt(var + LN_EPS) * g + b


def _dot(a, b):
    return jnp.dot(a, b, preferred_element_type=F32)


def _dot_nt(a, b):
    return lax.dot_general(a, b, (((1,), (1,)), ((), ())), preferred_element_type=F32)


def _head_view_rows(rows, heads, width):
    return rows * heads * (width // LANES)


def _head_load(ref, h, heads, rows, width):
    tiles = width // LANES
    parts = [ref[pl.ds(j * heads + h, rows, stride=heads * tiles), :] for j in range(tiles)]
    return parts[0] if tiles == 1 else jnp.concatenate(parts, axis=1)


def _head_store(ref, h, heads, value, row0=0):
    rows, width = value.shape
    tiles = width // LANES
    for j in range(tiles):
        start = (row0 * tiles + j) * heads + h
        ref[pl.ds(start, rows, stride=heads * tiles), :] = value[:, j * LANES:(j + 1) * LANES]


def _to_head_view(x):
    *lead, heads, width = x.shape
    tiles = width // LANES
    x = x.reshape(*lead, heads, tiles, LANES)
    x = jnp.swapaxes(x, -3, -2)
    return x.reshape(-1, LANES)


def _from_head_view(x, lead, heads, width):
    tiles = width // LANES
    x = x.reshape(*lead, tiles, heads, LANES)
    x = jnp.swapaxes(x, -3, -2)
    return x.reshape(*lead, heads, width)


def _ffn_kernel(*refs, alpha, d_ff, with_proj):
    if with_proj:
        (o_ref, xin_ref, wo_ref, lpg_ref, lpb_ref,
         wgu_ref, wd_ref, g_ref, b_ref, out_ref, act_ref) = refs
    else:
        xin_ref, wgu_ref, wd_ref, g_ref, b_ref, out_ref, act_ref = refs
    tile = xin_ref.shape[0]
    sub = min(FFN_SUB_TILE, tile)
    subs = [slice(r * sub, (r + 1) * sub) for r in range(tile // sub)]
    xs = []
    for rows in subs:
        x = xin_ref[rows, :]
        if with_proj:
            x = _layer_norm(alpha * x + _dot(o_ref[rows, :], wo_ref[...]), lpg_ref[...], lpb_ref[...])
        xs.append(x)
    for rows, x in zip(subs, xs):
        xb = x.astype(BF16)
        for c in range(d_ff // FF_CHUNK):
            lo = c * FF_CHUNK
            gate = _dot(xb, wgu_ref[:, lo:lo + FF_CHUNK])
            up = _dot(xb, wgu_ref[:, d_ff + lo:d_ff + lo + FF_CHUNK])
            act_ref[rows, lo:lo + FF_CHUNK] = (gate * jax.nn.sigmoid(gate) * up).astype(BF16)
    ys = [_dot(act_ref[rows, :], wd_ref[...]) for rows in subs]
    for rows, x, y in zip(subs, xs, ys):
        out_ref[rows, :] = _layer_norm(alpha * x + 0.5 * y, g_ref[...], b_ref[...])


def _ffn(x, w_gu, w_down, ln_g, ln_b, alpha, proj=None):
    n, d = x.shape
    d_ff = w_down.shape[0]
    tile = min(ROW_TILE, n)
    in_specs, args = [], []
    if proj is not None:
        o, wo, pg, pb = proj
        in_specs += [_rows(tile, o.shape[1])]
        args += [o]
    in_specs += [_rows(tile, d)]
    args += [x]
    if proj is not None:
        in_specs += [_resident(wo.shape), _resident((1, d)), _resident((1, d))]
        args += [wo, pg, pb]
    in_specs += [_resident(w_gu.shape), _resident(w_down.shape), _resident((1, d)), _resident((1, d))]
    args += [w_gu, w_down, ln_g, ln_b]
    return pl.pallas_call(
        functools.partial(_ffn_kernel, alpha=alpha, d_ff=d_ff, with_proj=proj is not None),
        grid=(n // tile,),
        in_specs=in_specs,
        out_specs=_rows(tile, d),
        out_shape=jax.ShapeDtypeStruct((n, d), F32),
        scratch_shapes=[pltpu.VMEM((tile, d_ff), BF16)],
        compiler_params=_params("parallel"),
        name="ffn_proj" if proj is not None else "ffn",
    )(*args)


def _inproj_kernel(x_ref, w_ref, wk_ref, lng_ref, lnb_ref, ws_ref, bst_ref, q_ref, k_ref, v_ref, g_ref, *vrows_ref,
                   keys_on_lanes):
    tile = x_ref.shape[0]
    sub = min(FFN_SUB_TILE, tile)
    row = lax.broadcasted_iota(jnp.int32, (GMLP_CHUNK, GMLP_CHUNK), 0)
    col = lax.broadcasted_iota(jnp.int32, (GMLP_CHUNK, GMLP_CHUNK), 1)
    mix_w = [jnp.where(row >= col, ws_ref[g], 0.0).astype(BF16) for g in range(GMLP_GROUPS)]

    def project(r0):
        xb = x_ref[r0:r0 + sub, :].astype(BF16)
        z = _dot(xb, w_ref[:, :2 * DIFF_WIDTH])
        q_ref[r0:r0 + sub, :] = (z[:, :DIFF_WIDTH] * Q_SCALE).astype(BF16)
        if keys_on_lanes:
            k_ref[:, r0:r0 + sub] = _dot_nt(wk_ref[...], xb)
        else:
            k_ref[r0:r0 + sub, :] = _dot(xb, wk_ref[...])
        for h in range(DIFF_HEADS):
            _head_store(v_ref, h, DIFF_HEADS, z[:, DIFF_WIDTH + h * HEAD_WIDTH:DIFF_WIDTH + (h + 1) * HEAD_WIDTH], r0)

    def gate(r0, zg):
        u = jax.nn.gelu(zg[:, :GMLP_WIDTH])
        vn = _layer_norm(jax.nn.gelu(zg[:, GMLP_WIDTH:]), lng_ref[...], lnb_ref[...])
        if vrows_ref:
            for g in range(GMLP_GROUPS):
                _head_store(vrows_ref[0], g, GMLP_GROUPS, vn[:, g * GMLP_GROUP_DIM:(g + 1) * GMLP_GROUP_DIM], r0)
        vnb = vn.astype(BF16)
        for g in range(GMLP_GROUPS):
            cols = slice(g * GMLP_GROUP_DIM, (g + 1) * GMLP_GROUP_DIM)
            bias = bst_ref[:, g:g + 1]
            for c in range(sub // GMLP_CHUNK):
                rows = slice(c * GMLP_CHUNK, (c + 1) * GMLP_CHUNK)
                mixed = _dot(mix_w[g], vnb[rows, cols]) + bias
                g_ref[r0 + rows.start:r0 + rows.stop, cols] = (u[rows, cols] * mixed).astype(BF16)

    starts = list(range(0, tile, sub))
    zgs = {}
    for i, r0 in enumerate(starts):
        zgs[r0] = _dot(x_ref[r0:r0 + sub, :].astype(BF16), w_ref[:, 2 * DIFF_WIDTH:])
        if i:
            gate(starts[i - 1], zgs.pop(starts[i - 1]))
        project(r0)
    gate(starts[-1], zgs.pop(starts[-1]))


def _head_rows(tile, heads, width):
    return _rows(_head_view_rows(tile, heads, width), LANES)


def _head_shape(rows, heads, width):
    return jax.ShapeDtypeStruct((_head_view_rows(rows, heads, width), LANES), F32)


def _inproj(x, w_rest, w_keys, ln_g, ln_b, ws, bs_t, batch, seq, keys_on_lanes, want_vrows):
    n, d = x.shape
    tile = min(ROW_TILE, n)
    if keys_on_lanes:
        per = seq // tile
        k_shape = jax.ShapeDtypeStruct((batch, DIFF_WIDTH, seq), F32)
        k_spec = pl.BlockSpec((None, DIFF_WIDTH, tile), lambda i: (i // per, 0, i % per))
    else:
        k_shape = jax.ShapeDtypeStruct((n, DIFF_WIDTH), F32)
        k_spec = _rows(tile, DIFF_WIDTH)
    out_shape = [jax.ShapeDtypeStruct((n, DIFF_WIDTH), BF16), k_shape,
                 _head_shape(n, DIFF_HEADS, HEAD_WIDTH),
                 jax.ShapeDtypeStruct((n, GMLP_WIDTH), BF16)]
    out_specs = [_rows(tile, DIFF_WIDTH), k_spec, _head_rows(tile, DIFF_HEADS, HEAD_WIDTH), _rows(tile, GMLP_WIDTH)]
    if want_vrows:
        out_shape.append(_head_shape(n, GMLP_GROUPS, GMLP_GROUP_DIM))
        out_specs.append(_head_rows(tile, GMLP_GROUPS, GMLP_GROUP_DIM))
    return pl.pallas_call(
        functools.partial(_inproj_kernel, keys_on_lanes=keys_on_lanes),
        grid=(n // tile,),
        in_specs=[_rows(tile, d), _resident(w_rest.shape), _resident(w_keys.shape), _resident((1, GMLP_WIDTH)),
                  _resident((1, GMLP_WIDTH)), _resident(ws.shape), _resident(bs_t.shape)],
        out_specs=out_specs,
        out_shape=out_shape,
        compiler_params=_params("parallel"),
        name="inproj",
    )(x, w_rest, w_keys, ln_g, ln_b, ws, bs_t)


def _head_scalars(head, lq1_ref, lk1_ref, lq2_ref, lk2_ref, lam_init):
    slope = jnp.exp2(jnp.zeros((1, 1), F32) - 8.0 * (head + 1).astype(F32) / DIFF_HEADS) * LOG2E
    lam = (jnp.exp(jnp.sum(lq1_ref[...] * lk1_ref[...], axis=-1, keepdims=True))
           - jnp.exp(jnp.sum(lq2_ref[...] * lk2_ref[...], axis=-1, keepdims=True)) + lam_init)
    return slope, lam


def _bias_lanes(slope, rows):
    lane = lax.broadcasted_iota(jnp.int32, (1, LANES), 1)
    out = jnp.zeros((1, LANES), F32)
    rest = slope
    for i in range(BIAS_TERMS):
        term = rest.astype(BF16).astype(F32)
        out = jnp.where(lane == 2 * i, term * POS_SPLIT, jnp.where(lane == 2 * i + 1, term, out))
        rest = rest - term
    return jnp.broadcast_to(out, (rows, LANES)).astype(BF16)


def _position_rows(n):
    assert n <= POS_SPLIT * 256
    row = lax.broadcasted_iota(jnp.int32, (LANES, n), 0)
    pos = lax.broadcasted_iota(jnp.int32, (LANES, n), 1)
    val = jnp.where(row % 2 == 0, pos // POS_SPLIT, pos % POS_SPLIT)
    return jnp.where(row < 2 * BIAS_TERMS, val, 0).astype(F32).astype(BF16)


def _stack_maps(q):
    lane = lax.broadcasted_iota(jnp.int32, q.shape, 1)
    zero = jnp.zeros_like(q)
    return jnp.concatenate([jnp.where(lane < HEAD_DIM, q, zero), jnp.where(lane >= HEAD_DIM, q, zero)], axis=0)


def _near_bias(slope, t, base_q, base_k, nk):
    r = lax.broadcasted_iota(jnp.int32, (t, nk), 0) + base_q
    c = lax.broadcasted_iota(jnp.int32, (t, nk), 1) + base_k
    bias = slope * (r - jnp.abs(r - c)).astype(F32)
    return jnp.where(c // CHUNK <= r // CHUNK, bias, MASKED)


def _diff_combine(parts, values, t, lam):
    return _weighted_values(_softmax_weights(parts), values, t, lam)


def _softmax_weights(parts):
    m = functools.reduce(jnp.maximum, [jnp.max(s, axis=-1, keepdims=True) for s in parts])
    return [jnp.exp2(s - m).astype(BF16) for s in parts]


def _weighted_values(weights, values, t, lam):
    res = functools.reduce(jnp.add, [_dot(p, v) for p, v in zip(weights, values)])
    out, denom = res[:, :HEAD_WIDTH], res[:, HEAD_WIDTH:HEAD_WIDTH + 1]
    return out[:t] * (1.0 / denom[:t]) - out[t:] * (lam / denom[t:])


def _with_ones(v):
    lane = lax.broadcasted_iota(jnp.int32, v.shape, 1)
    return jnp.concatenate([v, jnp.where(lane == 0, 1.0, 0.0).astype(v.dtype)], axis=1)


def _sub_norm(o, g, lam_init):
    return o * lax.rsqrt(jnp.mean(o * o, axis=-1, keepdims=True) + LN_EPS) * g * (1.0 - lam_init)


def _attn_prompt_kernel(q_ref, k_ref, v_ref, lq1_ref, lk1_ref, lq2_ref, lk2_ref, sg_ref, o_ref,
                        kb_ref, vb_ref, *, lam_init):
    seq = q_ref.shape[0]
    tq = ATTN_Q_TILE
    head = pl.program_id(1)
    slope, lam = _head_scalars(head, lq1_ref, lk1_ref, lq2_ref, lk2_ref, lam_init)
    kb_ref[:HEAD_WIDTH, :] = k_ref[...].astype(BF16)

    @pl.when((pl.program_id(0) == 0) & (head == 0))
    def _():
        kb_ref[HEAD_WIDTH:, :] = _position_rows(seq)
        vb_ref[...] = _with_ones(jnp.zeros((seq, HEAD_WIDTH), BF16))

    for h in range(DIFF_HEADS):
        @pl.when(head == h)
        def _():
            vb_ref[:, :HEAD_WIDTH] = _head_load(v_ref, h, DIFF_HEADS, seq, HEAD_WIDTH).astype(BF16)
    q_bias = _bias_lanes(slope, 2 * tq)
    r = lax.broadcasted_iota(jnp.int32, (tq, tq), 0)
    c = lax.broadcasted_iota(jnp.int32, (tq, tq), 1)
    diag = jnp.where(c // CHUNK <= r // CHUNK, -2.0 * slope * jnp.maximum(c - r, 0).astype(F32), MASKED)
    diag = jnp.concatenate([diag, diag], axis=0)
    def scores(lo):
        qs = jnp.concatenate([_stack_maps(q_ref[lo:lo + tq, :]), q_bias], axis=1)
        parts = [_dot(qs, kb_ref[:, lo:lo + tq]) + diag]
        if lo:
            parts.append(_dot(qs, kb_ref[:, 0:lo]))
        return parts

    def finish(lo, weights):
        values = [vb_ref[lo:lo + tq, :]] + ([vb_ref[0:lo, :]] if lo else [])
        out = _weighted_values(weights, values, tq, lam)
        o_ref[lo:lo + tq, :] = _sub_norm(out, sg_ref[...], lam_init).astype(BF16)

    starts = list(range(0, seq, tq))
    parts = scores(starts[0])
    for i, lo in enumerate(starts):
        next_parts = scores(starts[i + 1]) if i + 1 < len(starts) else None
        finish(lo, _softmax_weights(parts))
        parts = next_parts


def _attn_prompt(q, k_t, v, lam_vecs, subln_g, batch, seq, lam_init):
    blk = pl.BlockSpec((seq, HEAD_WIDTH), lambda b, h: (b, h))
    vec = pl.BlockSpec((1, HEAD_DIM), lambda b, h: (0, 0))
    return pl.pallas_call(
        functools.partial(_attn_prompt_kernel, lam_init=lam_init),
        grid=(batch, DIFF_HEADS),
        in_specs=[blk, pl.BlockSpec((None, HEAD_WIDTH, seq), lambda b, h: (b, h, 0)),
                  pl.BlockSpec((_head_view_rows(seq, DIFF_HEADS, HEAD_WIDTH), LANES), lambda b, h: (b, 0)),
                  vec, vec, vec, vec, pl.BlockSpec((1, HEAD_WIDTH), lambda b, h: (0, 0))],
        out_specs=blk,
        out_shape=jax.ShapeDtypeStruct((batch * seq, DIFF_WIDTH), BF16),
        scratch_shapes=[pltpu.VMEM((HEAD_WIDTH + LANES, seq), BF16), pltpu.VMEM((seq, 2 * HEAD_WIDTH), BF16)],
        compiler_params=_params("arbitrary", "arbitrary"),
        name="attn_prompt",
    )(q, k_t, v, *lam_vecs, subln_g)


def _attn_sample_kernel(q_ref, kn_ref, vn_ref, kc_ref, vc_ref, lq1_ref, lk1_ref, lq2_ref, lk2_ref, sg_ref, o_ref,
                        *, lam_init):
    t = q_ref.shape[0]
    past = kc_ref.shape[1]
    far_pos = lax.broadcasted_iota(jnp.int32, (1, past), 1).astype(F32)
    for h in range(DIFF_HEADS):
        cols = slice(h * HEAD_WIDTH, (h + 1) * HEAD_WIDTH)
        slope, lam = _head_scalars(jnp.int32(h), lq1_ref, lk1_ref, lq2_ref, lk2_ref, lam_init)
        qs = _stack_maps(q_ref[:, cols])
        near = _near_bias(slope, t, past, past, t)
        parts = [_dot_nt(qs, kn_ref[:, cols].astype(BF16)) + jnp.concatenate([near, near], axis=0),
                 _dot(qs, kc_ref[cols, :].astype(BF16)) + slope * far_pos]
        values = [_with_ones(_head_load(vn_ref, h, DIFF_HEADS, t, HEAD_WIDTH).astype(BF16)),
                  _with_ones(_head_load(vc_ref, h, DIFF_HEADS, past, HEAD_WIDTH).astype(BF16))]
        out = _diff_combine(parts, values, t, lam)
        o_ref[:, cols] = _sub_norm(out, sg_ref[...], lam_init).astype(BF16)


def _attn_sample(q, k_new, v_new, k_cache_t, v_cache, lam_vecs, subln_g, batch, seq, lam_init):
    past = k_cache_t.shape[2]
    new = pl.BlockSpec((seq, DIFF_WIDTH), lambda b: (b, 0))
    vec = pl.BlockSpec((1, HEAD_DIM), lambda b: (0, 0))
    return pl.pallas_call(
        functools.partial(_attn_sample_kernel, lam_init=lam_init),
        grid=(batch,),
        in_specs=[new, new, _head_rows(seq, DIFF_HEADS, HEAD_WIDTH),
                  pl.BlockSpec((None, DIFF_WIDTH, past), lambda b: (b, 0, 0)),
                  _head_rows(past, DIFF_HEADS, HEAD_WIDTH),
                  vec, vec, vec, vec, pl.BlockSpec((1, HEAD_WIDTH), lambda b: (0, 0))],
        out_specs=new,
        out_shape=jax.ShapeDtypeStruct((batch * seq, DIFF_WIDTH), BF16),
        compiler_params=_params("parallel"),
        name="attn_sample",
    )(q, k_new, v_new, k_cache_t, v_cache, *lam_vecs, subln_g)


def _outproj_kernel(a_ref, g_ref, x_ref, wo_ref, lg_ref, lb_ref, wq_ref, x2_ref, q2_ref, *, alpha):
    tile = x_ref.shape[0]
    sub = min(FFN_SUB_TILE, tile)
    subs = [slice(r * sub, (r + 1) * sub) for r in range(tile // sub)]
    mixes = [_dot(jnp.concatenate([a_ref[rows, :], g_ref[rows, :]], axis=1), wo_ref[...]) for rows in subs]
    for rows, mix in zip(subs, mixes):
        x2 = _layer_norm(alpha * x_ref[rows, :] + mix, lg_ref[...], lb_ref[...])
        x2_ref[rows, :] = x2
        q2_ref[rows, :] = _dot(x2.astype(BF16), wq_ref[...]).astype(BF16)


def _outproj(a, g, x, w_out, ln_g, ln_b, wq, alpha):
    n, d = x.shape
    tile = min(ROW_TILE, n)
    return pl.pallas_call(
        functools.partial(_outproj_kernel, alpha=alpha),
        grid=(n // tile,),
        in_specs=[_rows(tile, a.shape[1]), _rows(tile, g.shape[1]), _rows(tile, d), _resident(w_out.shape),
                  _resident((1, d)), _resident((1, d)), _resident(wq.shape)],
        out_specs=[_rows(tile, d), _rows(tile, d)],
        out_shape=[jax.ShapeDtypeStruct((n, d), F32), jax.ShapeDtypeStruct((n, d), BF16)],
        compiler_params=_params("parallel"),
        name="outproj",
    )(a, g, x, w_out, ln_g, ln_b, wq)


def _cross_kernel(q_ref, mk_ref, mv_ref, o_ref, *, n_mem):
    hd = q_ref.shape[1] // MEM_HEADS
    scale = hd ** -0.5
    for h in range(MEM_HEADS):
        cols = slice(h * hd, (h + 1) * hd)
        mk = _head_load(mk_ref, h, MEM_HEADS, n_mem, hd).astype(BF16)
        mv = _head_load(mv_ref, h, MEM_HEADS, n_mem, hd).astype(BF16)
        s = _dot_nt(q_ref[:, cols], mk) * scale
        p = jnp.exp(s - jnp.max(s, axis=-1, keepdims=True))
        o = _dot(p.astype(BF16), mv)
        o_ref[:, cols] = (o * (1.0 / jnp.sum(p, axis=-1, keepdims=True))).astype(BF16)


def _cross(q2, mem_k, mem_v, batch, seq, n_mem):
    d = q2.shape[1]
    tq = min(CROSS_Q_TILE, seq)
    per = seq // tq
    qblk = pl.BlockSpec((tq, d), lambda b, i: (b * per + i, 0))
    mblk = pl.BlockSpec((_head_view_rows(n_mem, MEM_HEADS, d // MEM_HEADS), LANES), lambda b, i: (b, 0))
    return pl.pallas_call(
        functools.partial(_cross_kernel, n_mem=n_mem),
        grid=(batch, per),
        in_specs=[qblk, mblk, mblk],
        out_specs=qblk,
        out_shape=jax.ShapeDtypeStruct(q2.shape, BF16),
        compiler_params=_params("parallel", "parallel"),
        name="cross",
    )(q2, mem_k, mem_v)


def _memkv_kernel(m_ref, wk_ref, wv_ref, k_ref, v_ref):
    mb = m_ref[...].astype(BF16)
    k = _dot(mb, wk_ref[...])
    v = _dot(mb, wv_ref[...])
    hd = k.shape[1] // MEM_HEADS
    for h in range(MEM_HEADS):
        _head_store(k_ref, h, MEM_HEADS, k[:, h * hd:(h + 1) * hd])
        _head_store(v_ref, h, MEM_HEADS, v[:, h * hd:(h + 1) * hd])


def _memkv(mem, wk, wv):
    n, d = mem.shape
    hd = d // MEM_HEADS
    tile = min(ROW_TILE, n)
    return pl.pallas_call(
        _memkv_kernel,
        grid=(n // tile,),
        in_specs=[_rows(tile, d), _resident(wk.shape), _resident(wv.shape)],
        out_specs=[_head_rows(tile, MEM_HEADS, hd)] * 2,
        out_shape=[_head_shape(n, MEM_HEADS, hd)] * 2,
        compiler_params=_params("parallel"),
        name="memkv",
    )(mem, wk, wv)


def _gating_weights(ws, bs, seq):
    n = min(seq, GMLP_CHUNK)
    rep = GMLP_CHUNK // n
    w = ws[:, :n, :n]
    if rep > 1:
        eye = jnp.eye(rep, dtype=ws.dtype)
        w = jnp.einsum('ab,gts->gatbs', eye, w).reshape(GMLP_GROUPS, GMLP_CHUNK, GMLP_CHUNK)
    return w, jnp.tile(bs[:, :n], (1, rep)).T


def _trunk(x, batch, seq, mem_k, mem_v, cache, p, layer_idx, alpha, want_vrows):
    lam_init = 0.8 - 0.6 * math.exp(-0.3 * layer_idx)
    ws, bs_t = _gating_weights(p['gmlp_ws'], p['gmlp_bs'], seq)
    lam_vecs = (p['lambda_q1'], p['lambda_k1'], p['lambda_q2'], p['lambda_k2'])

    x1 = _ffn(x, p['ffn1_w_gu'], p['ffn1_w_down'], p['ln1_g'], p['ln1_b'], alpha)
    keys_on_lanes = cache is None
    outs = _inproj(x1, p['w_in_rest'], p['w_in_keys_t'] if keys_on_lanes else p['w_in_keys'], p['gmlp_ln_g'],
                   p['gmlp_ln_b'], ws, bs_t, batch, seq, keys_on_lanes, want_vrows)
    q, k, v, gated = outs[:4]
    if cache is None:
        a = _attn_prompt(q, k, v, lam_vecs, p['subln_g'], batch, seq, lam_init)
    else:
        a = _attn_sample(q, k, v, cache[0], cache[1], lam_vecs, p['subln_g'], batch, seq, lam_init)
    x2, q2 = _outproj(a, gated, x1, p['w_out'], p['ln2_g'], p['ln2_b'], p['cross_wq'], alpha)
    n_mem = mem_k.size // (batch * x.shape[1])
    o = _cross(q2, mem_k, mem_v, batch, seq, n_mem)
    y = _ffn(x2, p['ffn2_w_gu'], p['ffn2_w_down'], p['ln4_g'], p['ln4_b'], alpha,
             proj=(o, p['cross_wo'], p['ln3_g'], p['ln3_b']))
    return y, k, v, (outs[4] if want_vrows else None)


_MATRICES = ('ffn1_w_gu', 'ffn1_w_down', 'w_in', 'w_out', 'cross_wq', 'cross_wk', 'cross_wv', 'cross_wo',
             'ffn2_w_gu', 'ffn2_w_down')
_ROW_VECTORS = ('ln1_g', 'ln1_b', 'lambda_q1', 'lambda_k1', 'lambda_q2', 'lambda_k2', 'subln_g', 'gmlp_ln_g',
                'gmlp_ln_b', 'ln2_g', 'ln2_b', 'ln3_g', 'ln3_b', 'ln4_g', 'ln4_b')


def kernel(x_prompt, x_sample, cache_k, cache_v, cache_mem_k, cache_mem_v, mem_prompt, ffn1_w_gu, ffn1_w_down, ln1_g, ln1_b, w_in, lambda_q1, lambda_k1, lambda_q2, lambda_k2, subln_g, gmlp_ln_g, gmlp_ln_b, gmlp_ws, gmlp_bs, w_out, ln2_g, ln2_b, cross_wq, cross_wk, cross_wv, cross_wo, ln3_g, ln3_b, ffn2_w_gu, ffn2_w_down, ln4_g, ln4_b):
    weights = dict(ffn1_w_gu=ffn1_w_gu, ffn1_w_down=ffn1_w_down, ln1_g=ln1_g, ln1_b=ln1_b, w_in=w_in,
                   lambda_q1=lambda_q1, lambda_k1=lambda_k1, lambda_q2=lambda_q2, lambda_k2=lambda_k2,
                   subln_g=subln_g, gmlp_ln_g=gmlp_ln_g, gmlp_ln_b=gmlp_ln_b, gmlp_ws=gmlp_ws, gmlp_bs=gmlp_bs,
                   w_out=w_out, ln2_g=ln2_g, ln2_b=ln2_b, cross_wq=cross_wq, cross_wk=cross_wk, cross_wv=cross_wv,
                   cross_wo=cross_wo, ln3_g=ln3_g, ln3_b=ln3_b, ffn2_w_gu=ffn2_w_gu, ffn2_w_down=ffn2_w_down,
                   ln4_g=ln4_g, ln4_b=ln4_b)
    depth = w_in.shape[0]
    alpha = (2 * depth) ** 0.25
    bp, tp, d = x_prompt.shape
    bs_, ts, _ = x_sample.shape
    n_mem = mem_prompt.shape[1]
    past = cache_k.shape[2]
    mem_hd = d // MEM_HEADS

    xp =x_prompt.reshape(bp * tp, d)
    xs = x_sample.reshape(bs_ * ts, d)
    mem2d = mem_prompt.reshape(bp * n_mem, d)
    per_layer = [[] for _ in range(7)]
    for l in range(depth):
        p = {}
        for name, w in weights.items():
            w = w[l]
            if name in _MATRICES:
                w = w.astype(BF16)
            elif name in _ROW_VECTORS:
                w = w.reshape(1, -1)
            p[name] = w
        w_in_l = p.pop('w_in')
        p['w_in_rest'] = jnp.concatenate([w_in_l[:, :DIFF_WIDTH], w_in_l[:, 2 * DIFF_WIDTH:]], axis=1)
        p['w_in_keys'] = w_in_l[:, DIFF_WIDTH:2 * DIFF_WIDTH]
        p['w_in_keys_t'] = p['w_in_keys'].T
        mem_k, mem_v = _memkv(mem2d, p['cross_wk'], p['cross_wv'])
        xp, kp_t, vp, _ = _trunk(xp, bp, tp, mem_k, mem_v, None, p, l, alpha, False)
        cache = (jnp.transpose(cache_k[l], (0, 2, 3, 4, 1)).reshape(bs_, DIFF_WIDTH, past),
                 _to_head_view(cache_v[l]))
        xs, ks, vs, gvs = _trunk(xs, bs_, ts, _to_head_view(cache_mem_k[l]), _to_head_view(cache_mem_v[l]),
                                 cache, p, l, alpha, True)
        kp = jnp.transpose(kp_t.reshape(bp, DIFF_HEADS, 2, HEAD_DIM, tp), (0, 4, 1, 2, 3))
        vals = (kp, _from_head_view(vp, (bp, tp), DIFF_HEADS, HEAD_WIDTH),
                _from_head_view(mem_k, (bp, n_mem), MEM_HEADS, mem_hd),
                _from_head_view(mem_v, (bp, n_mem), MEM_HEADS, mem_hd), ks,
                _from_head_view(vs, (bs_, ts), DIFF_HEADS, HEAD_WIDTH),
                _from_head_view(gvs, (bs_, ts), GMLP_GROUPS, GMLP_GROUP_DIM))
        for acc, val in zip(per_layer, vals):
            acc.append(val)

    kp, vp, mk, mv, ks, vs, gvs = (vals[0][None] if depth == 1 else jnp.stack(vals) for vals in per_layer)
    return (xp.reshape(bp, tp, d), xs.reshape(bs_, ts, d),
            kp.reshape(depth, bp, tp, DIFF_HEADS, 2, HEAD_DIM),
            vp.reshape(depth, bp, tp, DIFF_HEADS, HEAD_WIDTH),
            mk.reshape(depth, bp, n_mem, MEM_HEADS, mem_hd),
            mv.reshape(depth, bp, n_mem, MEM_HEADS, mem_hd),
            ks.reshape(depth, bs_, ts, DIFF_HEADS, 2, HEAD_DIM),
            vs.reshape(depth, bs_, ts, DIFF_HEADS, HEAD_WIDTH),
            gvs.reshape(depth, bs_, ts, GMLP_GROUPS, GMLP_GROUP_DIM))
```

```python
import functools
import math

import jax
import jax.numpy as jnp
from jax import lax
from jax.experimental import pallas as pl
from jax.experimental.pallas import tpu as pltpu

F32 = jnp.float32
BF16 = jnp.bfloat16

CHUNK = 64
DIFF_HEADS = 4
HEAD_DIM = 64
HEAD_WIDTH = 2 * HEAD_DIM
DIFF_WIDTH = DIFF_HEADS * HEAD_WIDTH
GMLP_GROUPS = 4
GMLP_CHUNK = 128
GMLP_GROUP_DIM = 128
GMLP_WIDTH = GMLP_GROUPS * GMLP_GROUP_DIM
MEM_HEADS = 4
LN_EPS = 1e-5
MASKED = -1e30
LANES = 128
LOG2E = math.log2(math.e)
Q_SCALE = HEAD_DIM ** -0.5 * LOG2E
POS_SPLIT = 64
BIAS_TERMS = 3

ROW_TILE = 1024
FF_CHUNK = 256
FFN_SUB_TILE = 256
ATTN_Q_TILE = 256
CROSS_STEP_ROWS = 128
CROSS_SUB_TILE = 512
VMEM_LIMIT_BYTES = 56 * 1024 * 1024


def _params(*semantics):
    return pltpu.CompilerParams(dimension_semantics=semantics, vmem_limit_bytes=VMEM_LIMIT_BYTES)


def _resident(shape):
    return pl.BlockSpec(shape, lambda *_: (0,) * len(shape), pipeline_mode=pl.Buffered(1))


def _rows(tile, width):
    return pl.BlockSpec((tile, width), lambda i: (i, 0))


def _layer_norm(h, g, b):
    mu = jnp.mean(h, axis=-1, keepdims=True)
    d = h - mu
    var = jnp.mean(d * d, axis=-1, keepdims=True)
    return d * lax.rsqrt(var + LN_EPS) * g + b


def _dot(a, b):
    return jnp.dot(a, b, preferred_element_type=F32)


def _dot_nt(a, b):
    return lax.dot_general(a, b, (((1,), (1,)), ((), ())), preferred_element_type=F32)


def _head_view_rows(rows, heads, width):
    return rows * heads * (width // LANES)


def _head_load(ref, h, heads, rows, width, row0=0):
    tiles = width // LANES
    parts = [ref[pl.ds((row0 * tiles + j) * heads + h, rows, stride=heads * tiles), :] for j in range(tiles)]
    return parts[0] if tiles == 1 else jnp.concatenate(parts, axis=1)


def _head_store(ref, h, heads, value, row0=0):
    rows, width = value.shape
    tiles = width // LANES
    for j in range(tiles):
        start = (row0 * tiles + j) * heads + h
        ref[pl.ds(start, rows, stride=heads * tiles), :] = value[:, j * LANES:(j + 1) * LANES]


def _to_head_view(x):
    *lead, heads, width = x.shape
    tiles = width // LANES
    x = x.reshape(*lead, heads, tiles, LANES)
    x = jnp.swapaxes(x, -3, -2)
    return x.reshape(-1, LANES)


def _from_head_view(x, lead, heads, width):
    tiles = width // LANES
    x = x.reshape(*lead, tiles, heads, LANES)
    x = jnp.swapaxes(x, -3, -2)
    return x.reshape(*lead, heads, width)


def _ffn_kernel(*refs, alpha, d_ff, with_proj):
    if with_proj:
        (o_ref, xin_ref, wo_ref, lpg_ref, lpb_ref,
         wgu_ref, wd_ref, g_ref, b_ref, out_ref, act_ref) = refs
    else:
        xin_ref, wgu_ref, wd_ref, g_ref, b_ref, out_ref, act_ref = refs
    tile = xin_ref.shape[0]
    sub = min(FFN_SUB_TILE, tile)
    subs = [slice(r * sub, (r + 1) * sub) for r in range(tile // sub)]
    xs = []
    for rows in subs:
        x = xin_ref[rows, :]
        if with_proj:
            x = _layer_norm(alpha * x + _dot(o_ref[rows, :], wo_ref[...]), lpg_ref[...], lpb_ref[...])
        xs.append(x)
    for rows, x in zip(subs, xs):
        xb = x.astype(BF16)
        for c in range(d_ff // FF_CHUNK):
            lo = c * FF_CHUNK
            gate = _dot(xb, wgu_ref[:, lo:lo + FF_CHUNK])
            up = _dot(xb, wgu_ref[:, d_ff + lo:d_ff + lo + FF_CHUNK])
            act_ref[rows, lo:lo + FF_CHUNK] = (gate * jax.nn.sigmoid(gate) * up).astype(BF16)
    ys = [_dot(act_ref[rows, :], wd_ref[...]) for rows in subs]
    for rows, x, y in zip(subs, xs, ys):
        out_ref[rows, :] = _layer_norm(alpha * x + 0.5 * y, g_ref[...], b_ref[...])


def _ffn(x, w_gu, w_down, ln_g, ln_b, alpha, proj=None):
    n, d = x.shape
    d_ff = w_down.shape[0]
    tile = min(ROW_TILE, n)
    in_specs, args = [], []
    if proj is not None:
        o, wo, pg, pb = proj
        in_specs += [_rows(tile, o.shape[1])]
        args += [o]
    in_specs += [_rows(tile, d)]
    args += [x]
    if proj is not None:
        in_specs += [_resident(wo.shape), _resident((1, d)), _resident((1, d))]
        args += [wo, pg, pb]
    in_specs += [_resident(w_gu.shape), _resident(w_down.shape), _resident((1, d)), _resident((1, d))]
    args += [w_gu, w_down, ln_g, ln_b]
    return pl.pallas_call(
        functools.partial(_ffn_kernel, alpha=alpha, d_ff=d_ff, with_proj=proj is not None),
        grid=(n // tile,),
        in_specs=in_specs,
        out_specs=_rows(tile, d),
        out_shape=jax.ShapeDtypeStruct((n, d), F32),
        scratch_shapes=[pltpu.VMEM((tile, d_ff), BF16)],
        compiler_params=_params("parallel"),
        name="ffn_proj" if proj is not None else "ffn",
    )(*args)


def _inproj_kernel(x_ref, w_ref, wk_ref, lng_ref, lnb_ref, ws_ref, bst_ref, q_ref, k_ref, v_ref, g_ref, *vrows_ref,
                   keys_on_lanes):
    tile = x_ref.shape[0]
    sub = min(FFN_SUB_TILE, tile)
    row = lax.broadcasted_iota(jnp.int32, (GMLP_CHUNK, GMLP_CHUNK), 0)
    col = lax.broadcasted_iota(jnp.int32, (GMLP_CHUNK, GMLP_CHUNK), 1)
    mix_w = [jnp.where(row >= col, ws_ref[g], 0.0).astype(BF16) for g in range(GMLP_GROUPS)]

    def project(r0):
        xb = x_ref[r0:r0 + sub, :].astype(BF16)
        z = _dot(xb, w_ref[:, :2 * DIFF_WIDTH])
        q_ref[r0:r0 + sub, :] = (z[:, :DIFF_WIDTH] * Q_SCALE).astype(BF16)
        if keys_on_lanes:
            k_ref[:, r0:r0 + sub] = _dot_nt(wk_ref[...], xb)
        else:
            k_ref[r0:r0 + sub, :] = _dot(xb, wk_ref[...])
        for h in range(DIFF_HEADS):
            _head_store(v_ref, h, DIFF_HEADS, z[:, DIFF_WIDTH + h * HEAD_WIDTH:DIFF_WIDTH + (h + 1) * HEAD_WIDTH], r0)

    def gate(r0, zg):
        u = jax.nn.gelu(zg[:, :GMLP_WIDTH])
        vn = _layer_norm(jax.nn.gelu(zg[:, GMLP_WIDTH:]), lng_ref[...], lnb_ref[...])
        if vrows_ref:
            for g in range(GMLP_GROUPS):
                _head_store(vrows_ref[0], g, GMLP_GROUPS, vn[:, g * GMLP_GROUP_DIM:(g + 1) * GMLP_GROUP_DIM], r0)
        vnb = vn.astype(BF16)
        for g in range(GMLP_GROUPS):
            cols = slice(g * GMLP_GROUP_DIM, (g + 1) * GMLP_GROUP_DIM)
            bias = bst_ref[:, g:g + 1]
            for c in range(sub // GMLP_CHUNK):
                rows = slice(c * GMLP_CHUNK, (c + 1) * GMLP_CHUNK)
                mixed = _dot(mix_w[g], vnb[rows, cols]) + bias
                g_ref[r0 + rows.start:r0 + rows.stop, cols] = (u[rows, cols] * mixed).astype(BF16)

    starts = list(range(0, tile, sub))
    zgs = {}
    for i, r0 in enumerate(starts):
        zgs[r0] = _dot(x_ref[r0:r0 + sub, :].astype(BF16), w_ref[:, 2 * DIFF_WIDTH:])
        if i:
            gate(starts[i - 1], zgs.pop(starts[i - 1]))
        project(r0)
    gate(starts[-1], zgs.pop(starts[-1]))


def _head_rows(tile, heads, width):
    return _rows(_head_view_rows(tile, heads, width), LANES)


def _head_shape(rows, heads, width):
    return jax.ShapeDtypeStruct((_head_view_rows(rows, heads, width), LANES), F32)


def _inproj(x, w_rest, w_keys, ln_g, ln_b, ws, bs_t, batch, seq, keys_on_lanes, want_vrows):
    n, d = x.shape
    tile = min(ROW_TILE, n)
    if keys_on_lanes:
        per = seq // tile
        k_shape = jax.ShapeDtypeStruct((batch, DIFF_WIDTH, seq), F32)
        k_spec = pl.BlockSpec((None, DIFF_WIDTH, tile), lambda i: (i // per, 0, i % per))
    else:
        k_shape = jax.ShapeDtypeStruct((n, DIFF_WIDTH), F32)
        k_spec = _rows(tile, DIFF_WIDTH)
    out_shape = [jax.ShapeDtypeStruct((n, DIFF_WIDTH), BF16), k_shape,
                 _head_shape(n, DIFF_HEADS, HEAD_WIDTH),
                 jax.ShapeDtypeStruct((n, GMLP_WIDTH), BF16)]
    out_specs = [_rows(tile, DIFF_WIDTH), k_spec, _head_rows(tile, DIFF_HEADS, HEAD_WIDTH), _rows(tile, GMLP_WIDTH)]
    if want_vrows:
        out_shape.append(_head_shape(n, GMLP_GROUPS, GMLP_GROUP_DIM))
        out_specs.append(_head_rows(tile, GMLP_GROUPS, GMLP_GROUP_DIM))
    return pl.pallas_call(
        functools.partial(_inproj_kernel, keys_on_lanes=keys_on_lanes),
        grid=(n // tile,),
        in_specs=[_rows(tile, d), _resident(w_rest.shape), _resident(w_keys.shape), _resident((1, GMLP_WIDTH)),
                  _resident((1, GMLP_WIDTH)), _resident(ws.shape), _resident(bs_t.shape)],
        out_specs=out_specs,
        out_shape=out_shape,
        compiler_params=_params("parallel"),
        name="inproj",
    )(x, w_rest, w_keys, ln_g, ln_b, ws, bs_t)


def _head_scalars(head, lq1_ref, lk1_ref, lq2_ref, lk2_ref, lam_init):
    slope = jnp.exp2(jnp.zeros((1, 1), F32) - 8.0 * (head + 1).astype(F32) / DIFF_HEADS) * LOG2E
    lam = (jnp.exp(jnp.sum(lq1_ref[...] * lk1_ref[...], axis=-1, keepdims=True))
           - jnp.exp(jnp.sum(lq2_ref[...] * lk2_ref[...], axis=-1, keepdims=True)) + lam_init)
    return slope, lam


def _bias_lanes(slope, rows):
    lane = lax.broadcasted_iota(jnp.int32, (1, LANES), 1)
    out = jnp.zeros((1, LANES), F32)
    rest = slope
    for i in range(BIAS_TERMS):
        term = rest.astype(BF16).astype(F32)
        out = jnp.where(lane == 2 * i, term * POS_SPLIT, jnp.where(lane == 2 * i + 1, term, out))
        rest = rest - term
    return jnp.broadcast_to(out, (rows, LANES)).astype(BF16)


def _position_rows(n):
    assert n <= POS_SPLIT * 256
    row = lax.broadcasted_iota(jnp.int32, (LANES, n), 0)
    pos = lax.broadcasted_iota(jnp.int32, (LANES, n), 1)
    val = jnp.where(row % 2 == 0, pos // POS_SPLIT, pos % POS_SPLIT)
    return jnp.where(row < 2 * BIAS_TERMS, val, 0).astype(F32).astype(BF16)


def _stack_maps(q):
    lane = lax.broadcasted_iota(jnp.int32, q.shape, 1)
    zero = jnp.zeros_like(q)
    return jnp.concatenate([jnp.where(lane < HEAD_DIM, q, zero), jnp.where(lane >= HEAD_DIM, q, zero)], axis=0)


def _near_bias(slope, t, base_q, base_k, nk):
    r = lax.broadcasted_iota(jnp.int32, (t, nk), 0) + base_q
    c = lax.broadcasted_iota(jnp.int32, (t, nk), 1) + base_k
    bias = slope * (r - jnp.abs(r - c)).astype(F32)
    return jnp.where(c // CHUNK <= r // CHUNK, bias, MASKED)


def _diff_combine(parts, values, t, lam):
    return _weighted_values(_softmax_weights(parts), values, t, lam)


def _softmax_weights(parts):
    m = functools.reduce(jnp.maximum, [jnp.max(s, axis=-1, keepdims=True) for s in parts])
    return [jnp.exp2(s - m).astype(BF16) for s in parts]


def _weighted_values(weights, values, t, lam):
    res = functools.reduce(jnp.add, [_dot(p, v) for p, v in zip(weights, values)])
    out, denom = res[:, :HEAD_WIDTH], res[:, HEAD_WIDTH:HEAD_WIDTH + 1]
    return out[:t] * (1.0 / denom[:t]) - out[t:] * (lam / denom[t:])


def _with_ones(v):
    lane = lax.broadcasted_iota(jnp.int32, v.shape, 1)
    return jnp.concatenate([v, jnp.where(lane == 0, 1.0, 0.0).astype(v.dtype)], axis=1)


def _sub_norm(o, g, lam_init):
    return o * lax.rsqrt(jnp.mean(o * o, axis=-1, keepdims=True) + LN_EPS) * g * (1.0 - lam_init)


def _attn_prompt_kernel(q_ref, k_ref, v_ref, lq1_ref, lk1_ref, lq2_ref, lk2_ref, sg_ref, o_ref,
                        kb_ref, vb_ref, *, lam_init):
    seq = q_ref.shape[0]
    tq = ATTN_Q_TILE
    head = pl.program_id(1)
    slope, lam = _head_scalars(head, lq1_ref, lk1_ref, lq2_ref, lk2_ref, lam_init)
    kb_ref[:HEAD_WIDTH, :] = k_ref[...].astype(BF16)

    @pl.when((pl.program_id(0) == 0) & (head == 0))
    def _():
        kb_ref[HEAD_WIDTH:, :] = _position_rows(seq)
        vb_ref[...] = _with_ones(jnp.zeros((seq, HEAD_WIDTH), BF16))

    for h in range(DIFF_HEADS):
        @pl.when(head == h)
        def _():
            vb_ref[:, :HEAD_WIDTH] = _head_load(v_ref, h, DIFF_HEADS, seq, HEAD_WIDTH).astype(BF16)
    q_bias = _bias_lanes(slope, 2 * tq)
    r = lax.broadcasted_iota(jnp.int32, (tq, tq), 0)
    c = lax.broadcasted_iota(jnp.int32, (tq, tq), 1)
    diag = jnp.where(c // CHUNK <= r // CHUNK, -2.0 * slope * jnp.maximum(c - r, 0).astype(F32), MASKED)
    diag = jnp.concatenate([diag, diag], axis=0)
    def scores(lo):
        qs = jnp.concatenate([_stack_maps(q_ref[lo:lo + tq, :]), q_bias], axis=1)
        parts = [_dot(qs, kb_ref[:, lo:lo + tq]) + diag]
        if lo:
            parts.append(_dot(qs, kb_ref[:, 0:lo]))
        return parts

    def finish(lo, weights):
        values = [vb_ref[lo:lo + tq, :]] + ([vb_ref[0:lo, :]] if lo else [])
        out = _weighted_values(weights, values, tq, lam)
        o_ref[lo:lo + tq, :] = _sub_norm(out, sg_ref[...], lam_init).astype(BF16)

    starts = list(range(0, seq, tq))
    parts = scores(starts[0])
    for i, lo in enumerate(starts):
        next_parts = scores(starts[i + 1]) if i + 1 < len(starts) else None
        finish(lo, _softmax_weights(parts))
        parts = next_parts


def _attn_prompt(q, k_t, v, lam_vecs, subln_g, batch, seq, lam_init):
    blk = pl.BlockSpec((seq, HEAD_WIDTH), lambda b, h: (b, h))
    vec = pl.BlockSpec((1, HEAD_DIM), lambda b, h: (0, 0))
    return pl.pallas_call(
        functools.partial(_attn_prompt_kernel, lam_init=lam_init),
        grid=(batch, DIFF_HEADS),
        in_specs=[blk, pl.BlockSpec((None, HEAD_WIDTH, seq), lambda b, h: (b, h, 0)),
                  pl.BlockSpec((_head_view_rows(seq, DIFF_HEADS, HEAD_WIDTH), LANES), lambda b, h: (b, 0)),
                  vec, vec, vec, vec, pl.BlockSpec((1, HEAD_WIDTH), lambda b, h: (0, 0))],
        out_specs=blk,
        out_shape=jax.ShapeDtypeStruct((batch * seq, DIFF_WIDTH), BF16),
        scratch_shapes=[pltpu.VMEM((HEAD_WIDTH + LANES, seq), BF16), pltpu.VMEM((seq, 2 * HEAD_WIDTH), BF16)],
        compiler_params=_params("arbitrary", "arbitrary"),
        name="attn_prompt",
    )(q, k_t, v, *lam_vecs, subln_g)


def _attn_sample_kernel(q_ref, kn_ref, vn_ref, kc_ref, vc_ref, lq1_ref, lk1_ref, lq2_ref, lk2_ref, sg_ref, o_ref,
                        *, lam_init):
    t = q_ref.shape[0]
    past = kc_ref.shape[1]
    far_pos = lax.broadcasted_iota(jnp.int32, (1, past), 1).astype(F32)
    for h in range(DIFF_HEADS):
        cols = slice(h * HEAD_WIDTH, (h + 1) * HEAD_WIDTH)
        slope, lam = _head_scalars(jnp.int32(h), lq1_ref, lk1_ref, lq2_ref, lk2_ref, lam_init)
        qs = _stack_maps(q_ref[:, cols])
        near = _near_bias(slope, t, past, past, t)
        parts = [_dot_nt(qs, kn_ref[:, cols].astype(BF16)) + jnp.concatenate([near, near], axis=0),
                 _dot(qs, kc_ref[cols, :].astype(BF16)) + slope * far_pos]
        values = [_with_ones(_head_load(vn_ref, h, DIFF_HEADS, t, HEAD_WIDTH).astype(BF16)),
                  _with_ones(_head_load(vc_ref, h, DIFF_HEADS, past, HEAD_WIDTH).astype(BF16))]
        out = _diff_combine(parts, values, t, lam)
        o_ref[:, cols] = _sub_norm(out, sg_ref[...], lam_init).astype(BF16)


def _attn_sample(q, k_new, v_new, k_cache_t, v_cache, lam_vecs, subln_g, batch, seq, lam_init):
    past = k_cache_t.shape[2]
    new = pl.BlockSpec((seq, DIFF_WIDTH), lambda b: (b, 0))
    vec = pl.BlockSpec((1, HEAD_DIM), lambda b: (0, 0))
    return pl.pallas_call(
        functools.partial(_attn_sample_kernel, lam_init=lam_init),
        grid=(batch,),
        in_specs=[new, new, _head_rows(seq, DIFF_HEADS, HEAD_WIDTH),
                  pl.BlockSpec((None, DIFF_WIDTH, past), lambda b: (b, 0, 0)),
                  _head_rows(past, DIFF_HEADS, HEAD_WIDTH),
                  vec, vec, vec, vec, pl.BlockSpec((1, HEAD_WIDTH), lambda b: (0, 0))],
        out_specs=new,
        out_shape=jax.ShapeDtypeStruct((batch * seq, DIFF_WIDTH), BF16),
        compiler_params=_params("parallel"),
        name="attn_sample",
    )(q, k_new, v_new, k_cache_t, v_cache, *lam_vecs, subln_g)


def _outproj_kernel(a_ref, g_ref, x_ref, wo_ref, lg_ref, lb_ref, wq_ref, x2_ref, q2_ref, *, alpha):
    tile = x_ref.shape[0]
    sub = min(FFN_SUB_TILE, tile)
    subs = [slice(r * sub, (r + 1) * sub) for r in range(tile // sub)]
    mixes = [_dot(jnp.concatenate([a_ref[rows, :], g_ref[rows, :]], axis=1), wo_ref[...]) for rows in subs]
    for rows, mix in zip(subs, mixes):
        x2 = _layer_norm(alpha * x_ref[rows, :] + mix, lg_ref[...], lb_ref[...])
        x2_ref[rows, :] = x2
        q2_ref[rows, :] = _dot(x2.astype(BF16), wq_ref[...]).astype(BF16)


def _outproj(a, g, x, w_out, ln_g, ln_b, wq, alpha):
    n, d = x.shape
    tile = min(ROW_TILE, n)
    return pl.pallas_call(
        functools.partial(_outproj_kernel, alpha=alpha),
        grid=(n // tile,),
        in_specs=[_rows(tile, a.shape[1]), _rows(tile, g.shape[1]), _rows(tile, d), _resident(w_out.shape),
                  _resident((1, d)), _resident((1, d)), _resident(wq.shape)],
        out_specs=[_rows(tile, d), _rows(tile, d)],
        out_shape=[jax.ShapeDtypeStruct((n, d), F32), jax.ShapeDtypeStruct((n, d), BF16)],
        compiler_params=_params("parallel"),
        name="outproj",
    )(a, g, x, w_out, ln_g, ln_b, wq)


def _cross_kernel(q_ref, mk_ref, mv_ref, o_ref, *, n_mem, seq_rows, sub):
    hd = q_ref.shape[1] // MEM_HEADS
    log2_scale = hd ** -0.5 * LOG2E
    chains = []
    for b in range(q_ref.shape[0] // seq_rows):
        for h in range(MEM_HEADS):
            mk = _head_load(mk_ref, h, MEM_HEADS, n_mem, hd, b * n_mem).astype(BF16)
            mv = _head_load(mv_ref, h, MEM_HEADS, n_mem, hd, b * n_mem).astype(BF16)
            for r0 in range(b * seq_rows, (b + 1) * seq_rows, sub):
                chains.append((slice(r0, r0 + sub), slice(h * hd, (h + 1) * hd), mk, mv))

    def scores(chain):
        rows, cols, mk, _ = chain
        return _dot_nt(q_ref[rows, cols], mk)

    s = scores(chains[0])
    for i, (rows, cols, _, mv) in enumerate(chains):
        s_next = scores(chains[i + 1]) if i + 1 < len(chains) else None
        p = jnp.exp2((s - jnp.max(s, axis=-1, keepdims=True)) * log2_scale)
        o = _dot(p.astype(BF16), mv)
        o_ref[rows, cols] = (o * (1.0 / jnp.sum(p, axis=-1, keepdims=True))).astype(BF16)
        s = s_next


def _cross(q2, mem_k, mem_v, batch, seq, n_mem):
    d = q2.shape[1]
    per_step = max(1, CROSS_STEP_ROWS // seq)
    qblk = pl.BlockSpec((per_step * seq, d), lambda i: (i, 0))
    mblk = pl.BlockSpec((per_step * _head_view_rows(n_mem, MEM_HEADS, d // MEM_HEADS), LANES), lambda i: (i, 0))
    return pl.pallas_call(
        functools.partial(_cross_kernel, n_mem=n_mem, seq_rows=seq, sub=min(CROSS_SUB_TILE, seq)),
        grid=(batch // per_step,),
        in_specs=[qblk, mblk, mblk],
        out_specs=qblk,
        out_shape=jax.ShapeDtypeStruct(q2.shape, BF16),
        compiler_params=_params("parallel"),
        name="cross",
    )(q2, mem_k, mem_v)


def _memkv_kernel(m_ref, wk_ref, wv_ref, k_ref, v_ref):
    mb = m_ref[...].astype(BF16)
    k = _dot(mb, wk_ref[...])
    v = _dot(mb, wv_ref[...])
    hd = k.shape[1] // MEM_HEADS
    for h in range(MEM_HEADS):
        _head_store(k_ref, h, MEM_HEADS, k[:, h * hd:(h + 1) * hd])
        _head_store(v_ref, h, MEM_HEADS, v[:, h * hd:(h + 1) * hd])


def _memkv(mem, wk, wv):
    n, d = mem.shape
    hd = d // MEM_HEADS
    tile = min(ROW_TILE, n)
    return pl.pallas_call(
        _memkv_kernel,
        grid=(n // tile,),
        in_specs=[_rows(tile, d), _resident(wk.shape), _resident(wv.shape)],
        out_specs=[_head_rows(tile, MEM_HEADS, hd)] * 2,
        out_shape=[_head_shape(n, MEM_HEADS, hd)] * 2,
        compiler_params=_params("parallel"),
        name="memkv",
    )(mem, wk, wv)


def _gating_weights(ws, bs, seq):
    n = min(seq, GMLP_CHUNK)
    rep = GMLP_CHUNK // n
    w = ws[:, :n, :n]
    if rep > 1:
        eye = jnp.eye(rep, dtype=ws.dtype)
        w = jnp.einsum('ab,gts->gatbs', eye, w).reshape(GMLP_GROUPS, GMLP_CHUNK, GMLP_CHUNK)
    return w, jnp.tile(bs[:, :n], (1, rep)).T


def _trunk(x, batch, seq, mem_k, mem_v, cache, p, layer_idx, alpha, want_vrows):
    lam_init = 0.8 - 0.6 * math.exp(-0.3 * layer_idx)
    ws, bs_t = _gating_weights(p['gmlp_ws'], p['gmlp_bs'], seq)
    lam_vecs = (p['lambda_q1'], p['lambda_k1'], p['lambda_q2'], p['lambda_k2'])

    x1 = _ffn(x, p['ffn1_w_gu'], p['ffn1_w_down'], p['ln1_g'], p['ln1_b'], alpha)
    keys_on_lanes = cache is None
    outs = _inproj(x1, p['w_in_rest'], p['w_in_keys_t'] if keys_on_lanes else p['w_in_keys'], p['gmlp_ln_g'],
                   p['gmlp_ln_b'], ws, bs_t, batch, seq, keys_on_lanes, want_vrows)
    q, k, v, gated = outs[:4]
    if cache is None:
        a = _attn_prompt(q, k, v, lam_vecs, p['subln_g'], batch, seq, lam_init)
    else:
        a = _attn_sample(q, k, v, cache[0], cache[1], lam_vecs, p['subln_g'], batch, seq, lam_init)
    x2, q2 = _outproj(a, gated, x1, p['w_out'], p['ln2_g'], p['ln2_b'], p['cross_wq'], alpha)
    n_mem = mem_k.size // (batch * x.shape[1])
    o = _cross(q2, mem_k, mem_v, batch, seq, n_mem)
    y = _ffn(x2, p['ffn2_w_gu'], p['ffn2_w_down'], p['ln4_g'], p['ln4_b'], alpha,
             proj=(o, p['cross_wo'], p['ln3_g'], p['ln3_b']))
    return y, k, v, (outs[4] if want_vrows else None)


_MATRICES = ('ffn1_w_gu', 'ffn1_w_down', 'w_in', 'w_out', 'cross_wq', 'cross_wk', 'cross_wv', 'cross_wo',
             'ffn2_w_gu', 'ffn2_w_down')
_ROW_VECTORS = ('ln1_g', 'ln1_b', 'lambda_q1', 'lambda_k1', 'lambda_q2', 'lambda_k2', 'subln_g', 'gmlp_ln_g',
                'gmlp_ln_b', 'ln2_g', 'ln2_b', 'ln3_g', 'ln3_b', 'ln4_g', 'ln4_b')


def kernel(x_prompt, x_sample, cache_k, cache_v, cache_mem_k, cache_mem_v, mem_prompt, ffn1_w_gu, ffn1_w_down, ln1_g, ln1_b, w_in, lambda_q1, lambda_k1, lambda_q2, lambda_k2, subln_g, gmlp_ln_g, gmlp_ln_b, gmlp_ws, gmlp_bs, w_out, ln2_g, ln2_b, cross_wq, cross_wk, cross_wv, cross_wo, ln3_g, ln3_b, ffn2_w_gu, ffn2_w_down, ln4_g, ln4_b):
    weights = dict(ffn1_w_gu=ffn1_w_gu, ffn1_w_down=ffn1_w_down, ln1_g=ln1_g, ln1_b=ln1_b, w_in=w_in,
                   lambda_q1=lambda_q1, lambda_k1=lambda_k1, lambda_q2=lambda_q2, lambda_k2=lambda_k2,
                   subln_g=subln_g, gmlp_ln_g=gmlp_ln_g, gmlp_ln_b=gmlp_ln_b, gmlp_ws=gmlp_ws, gmlp_bs=gmlp_bs,
                   w_out=w_out, ln2_g=ln2_g, ln2_b=ln2_b, cross_wq=cross_wq, cross_wk=cross_wk, cross_wv=cross_wv,
                   cross_wo=cross_wo, ln3_g=ln3_g, ln3_b=ln3_b, ffn2_w_gu=ffn2_w_gu, ffn2_w_down=ffn2_w_down,
                   ln4_g=ln4_g, ln4_b=ln4_b)
    depth = w_in.shape[0]
    alpha = (2 * depth) ** 0.25
    bp, tp, d = x_prompt.shape
    bs_, ts, _ = x_sample.shape
    n_mem = mem_prompt.shape[1]
    past = cache_k.shape[2]
    mem_hd = d // MEM_HEADS

    xp =x_prompt.reshape(bp * tp, d)
    xs = x_sample.reshape(bs_ * ts, d)
    mem2d = mem_prompt.reshape(bp * n_mem, d)
    per_layer = [[] for _ in range(7)]
    for l in range(depth):
        p = {}
        for name, w in weights.items():
            w = w[l]
            if name in _MATRICES:
                w = w.astype(BF16)
            elif name in _ROW_VECTORS:
                w = w.reshape(1, -1)
            p[name] = w
        w_in_l = p.pop('w_in')
        p['w_in_rest'] = jnp.concatenate([w_in_l[:, :DIFF_WIDTH], w_in_l[:, 2 * DIFF_WIDTH:]], axis=1)
        p['w_in_keys'] = w_in_l[:, DIFF_WIDTH:2 * DIFF_WIDTH]
        p['w_in_keys_t'] = p['w_in_keys'].T
        mem_k, mem_v = _memkv(mem2d, p['cross_wk'], p['cross_wv'])
        xp, kp_t, vp, _ = _trunk(xp, bp, tp, mem_k, mem_v, None, p, l, alpha, False)
        cache = (jnp.transpose(cache_k[l], (0, 2, 3, 4, 1)).reshape(bs_, DIFF_WIDTH, past),
                 _to_head_view(cache_v[l]))
        xs, ks, vs, gvs = _trunk(xs, bs_, ts, _to_head_view(cache_mem_k[l]), _to_head_view(cache_mem_v[l]),
                                 cache, p, l, alpha, True)
        kp = jnp.transpose(kp_t.reshape(bp, DIFF_HEADS, 2, HEAD_DIM, tp), (0, 4, 1, 2, 3))
        vals = (kp, _from_head_view(vp, (bp, tp), DIFF_HEADS, HEAD_WIDTH),
                _from_head_view(mem_k, (bp, n_mem), MEM_HEADS, mem_hd),
                _from_head_view(mem_v, (bp, n_mem), MEM_HEADS, mem_hd), ks,
                _from_head_view(vs, (bs_, ts), DIFF_HEADS, HEAD_WIDTH),
                _from_head_view(gvs, (bs_, ts), GMLP_GROUPS, GMLP_GROUP_DIM))
        for acc, val in zip(per_layer, vals):
            acc.append(val)

    kp, vp, mk, mv, ks, vs, gvs = (vals[0][None] if depth == 1 else jnp.stack(vals) for vals in per_layer)
    return (xp.reshape(bp, tp, d), xs.reshape(bs_, ts, d),
            kp.reshape(depth, bp, tp, DIFF_HEADS, 2, HEAD_DIM),
            vp.reshape(depth, bp, tp, DIFF_HEADS, HEAD_WIDTH),
            mk.reshape(depth, bp, n_mem, MEM_HEADS, mem_hd),
            mv.reshape(depth, bp, n_mem, MEM_HEADS, mem_hd),
            ks.reshape(depth, bs_, ts, DIFF_HEADS, 2, HEAD_DIM),
            vs.reshape(depth, bs_, ts, DIFF_HEADS, HEAD_WIDTH),
            gvs.reshape(depth, bs_, ts, GMLP_GROUPS, GMLP_GROUP_DIM))
```

```python
import functools
import math

import jax
import jax.numpy as jnp
from jax import lax
from jax.experimental import pallas as pl
from jax.experimental.pallas import tpu as pltpu

F32 = jnp.float32
BF16 = jnp.bfloat16

CHUNK = 64
DIFF_HEADS = 4
HEAD_DIM = 64
HEAD_WIDTH = 2 * HEAD_DIM
DIFF_WIDTH = DIFF_HEADS * HEAD_WIDTH
GMLP_GROUPS = 4
GMLP_CHUNK = 128
GMLP_GROUP_DIM = 128
GMLP_WIDTH = GMLP_GROUPS * GMLP_GROUP_DIM
MEM_HEADS = 4
LN_EPS = 1e-5
MASKED = -1e30
LANES = 128
BF16_SUBLANES = 16
LOG2E = math.log2(math.e)
Q_SCALE = HEAD_DIM ** -0.5 * LOG2E
POS_SPLIT = 64
BIAS_TERMS = 3

ROW_TILE = 1024
FF_CHUNK = 256
FFN_SUB_TILE = 256
MEMKV_TILE = 512
ATTN_Q_TILE = 256
CROSS_STEP_ROWS = 128
CROSS_SUB_TILE = 512
VMEM_LIMIT_BYTES = 56 * 1024 * 1024


def _params(*semantics):
    return pltpu.CompilerParams(dimension_semantics=semantics, vmem_limit_bytes=VMEM_LIMIT_BYTES)


def _resident(shape):
    return pl.BlockSpec(shape, lambda *_: (0,) * len(shape), pipeline_mode=pl.Buffered(1))


def _rows(tile, width):
    return pl.BlockSpec((tile, width), lambda i: (i, 0))


def _layer_norm(h, g, b):
    mu = jnp.mean(h, axis=-1, keepdims=True)
    d = h - mu
    var = jnp.mean(d * d, axis=-1, keepdims=True)
    return d * lax.rsqrt(var + LN_EPS) * g + b


def _dot(a, b):
    return jnp.dot(a, b, preferred_element_type=F32)


def _dot_nt(a, b):
    return lax.dot_general(a, b, (((1,), (1,)), ((), ())), preferred_element_type=F32)


def _head_view_rows(rows, heads, width):
    return rows * heads * (width // LANES)


def _head_load(ref, h, heads, rows, width, row0=0):
    tiles = width // LANES
    parts = [ref[pl.ds((row0 * tiles + j) * heads + h, rows, stride=heads * tiles), :] for j in range(tiles)]
    return parts[0] if tiles == 1 else jnp.concatenate(parts, axis=1)


def _head_store(ref, h, heads, value, row0=0):
    rows, width = value.shape
    tiles = width // LANES
    for j in range(tiles):
        start = (row0 * tiles + j) * heads + h
        ref[pl.ds(start, rows, stride=heads * tiles), :] = value[:, j * LANES:(j + 1) * LANES]


def _to_head_view(x):
    *lead, heads, width = x.shape
    tiles = width // LANES
    x = x.reshape(*lead, heads, tiles, LANES)
    x = jnp.swapaxes(x, -3, -2)
    return x.reshape(-1, LANES)


def _from_head_view(x, lead, heads, width):
    tiles = width // LANES
    x = x.reshape(*lead, tiles, heads, LANES)
    x = jnp.swapaxes(x, -3, -2)
    return x.reshape(*lead, heads, width)


def _side_cast_specs(weights, steps):
    in_specs, out_specs, out_shapes = [], [], []
    for w in weights:
        rows, cols = w.shape
        units = rows // BF16_SUBLANES
        assert units * BF16_SUBLANES == rows
        blocks = max(b for b in range(1, min(steps, units) + 1) if units % b == 0)
        spec = pl.BlockSpec((rows // blocks, cols), lambda i, last=blocks - 1: (jnp.minimum(i, last), 0))
        in_specs.append(spec)
        out_specs.append(spec)
        out_shapes.append(jax.ShapeDtypeStruct((rows, cols), BF16))
    return in_specs, out_specs, out_shapes


def _side_cast(src_refs, dst_refs):
    for src, dst in zip(src_refs, dst_refs):
        dst[...] = src[...].astype(BF16)


def _ffn_kernel(*refs, alpha, d_ff, with_proj, n_side):
    n_in = 9 if with_proj else 5
    side_in, side_out = refs[n_in:n_in + n_side], refs[n_in + n_side + 1:n_in + 2 * n_side + 1]
    out_ref, act_ref = refs[n_in + n_side], refs[-1]
    if with_proj:
        o_ref, xin_ref, wo_ref, lpg_ref, lpb_ref, wgu_ref, wd_ref, g_ref, b_ref = refs[:n_in]
    else:
        xin_ref, wgu_ref, wd_ref, g_ref, b_ref = refs[:n_in]
    _side_cast(side_in, side_out)
    tile = xin_ref.shape[0]
    sub = min(FFN_SUB_TILE, tile)
    subs = [slice(r * sub, (r + 1) * sub) for r in range(tile // sub)]
    xs = []
    for rows in subs:
        x = xin_ref[rows, :]
        if with_proj:
            x = _layer_norm(alpha * x + _dot(o_ref[rows, :], wo_ref[...]), lpg_ref[...], lpb_ref[...])
        xs.append(x)
    for rows, x in zip(subs, xs):
        xb = x.astype(BF16)
        for c in range(d_ff // FF_CHUNK):
            lo = c * FF_CHUNK
            gate = _dot(xb, wgu_ref[:, lo:lo + FF_CHUNK])
            up = _dot(xb, wgu_ref[:, d_ff + lo:d_ff + lo + FF_CHUNK])
            act_ref[rows, lo:lo + FF_CHUNK] = (gate * jax.nn.sigmoid(gate) * up).astype(BF16)
    ys = [_dot(act_ref[rows, :], wd_ref[...]) for rows in subs]
    for rows, x, y in zip(subs, xs, ys):
        out_ref[rows, :] = _layer_norm(alpha * x + 0.5 * y, g_ref[...], b_ref[...])


def _ffn(x, w_gu, w_down, ln_g, ln_b, alpha, proj=None, side=()):
    n, d = x.shape
    d_ff = w_down.shape[0]
    tile = min(ROW_TILE, n)
    side_in, side_out, side_shapes = _side_cast_specs(side, n // tile)
    in_specs, args = [], []
    if proj is not None:
        o, wo, pg, pb = proj
        in_specs += [_rows(tile, o.shape[1])]
        args += [o]
    in_specs += [_rows(tile, d)]
    args += [x]
    if proj is not None:
        in_specs += [_resident(wo.shape), _resident((1, d)), _resident((1, d))]
        args += [wo, pg, pb]
    in_specs += [_resident(w_gu.shape), _resident(w_down.shape), _resident((1, d)), _resident((1, d))]
    args += [w_gu, w_down, ln_g, ln_b]
    outs = pl.pallas_call(
        functools.partial(_ffn_kernel, alpha=alpha, d_ff=d_ff, with_proj=proj is not None, n_side=len(side)),
        grid=(n // tile,),
        in_specs=in_specs + side_in,
        out_specs=[_rows(tile, d)] + side_out,
        out_shape=[jax.ShapeDtypeStruct((n, d), F32)] + side_shapes,
        scratch_shapes=[pltpu.VMEM((tile, d_ff), BF16)],
        compiler_params=_params("parallel"),
        name="ffn_proj" if proj is not None else "ffn",
    )(*args, *side)
    return outs[0], tuple(outs[1:])


def _inproj_kernel(x_ref, w_ref, wk_ref, lng_ref, lnb_ref, ws_ref, bst_ref, q_ref, k_ref, v_ref, g_ref, *vrows_ref,
                   keys_on_lanes):
    tile = x_ref.shape[0]
    sub = min(FFN_SUB_TILE, tile)
    row = lax.broadcasted_iota(jnp.int32, (GMLP_CHUNK, GMLP_CHUNK), 0)
    col = lax.broadcasted_iota(jnp.int32, (GMLP_CHUNK, GMLP_CHUNK), 1)
    mix_w = [jnp.where(row >= col, ws_ref[g], 0.0).astype(BF16) for g in range(GMLP_GROUPS)]

    def project(r0):
        xb = x_ref[r0:r0 + sub, :].astype(BF16)
        z = _dot(xb, w_ref[:, :2 * DIFF_WIDTH])
        q_ref[r0:r0 + sub, :] = (z[:, :DIFF_WIDTH] * Q_SCALE).astype(BF16)
        if keys_on_lanes:
            k_ref[:, r0:r0 + sub] = _dot_nt(wk_ref[...], xb)
        else:
            k_ref[r0:r0 + sub, :] = _dot(xb, wk_ref[...])
        for h in range(DIFF_HEADS):
            _head_store(v_ref, h, DIFF_HEADS, z[:, DIFF_WIDTH + h * HEAD_WIDTH:DIFF_WIDTH + (h + 1) * HEAD_WIDTH], r0)

    def gate(r0, zg):
        u = jax.nn.gelu(zg[:, :GMLP_WIDTH])
        vn = _layer_norm(jax.nn.gelu(zg[:, GMLP_WIDTH:]), lng_ref[...], lnb_ref[...])
        if vrows_ref:
            for g in range(GMLP_GROUPS):
                _head_store(vrows_ref[0], g, GMLP_GROUPS, vn[:, g * GMLP_GROUP_DIM:(g + 1) * GMLP_GROUP_DIM], r0)
        vnb = vn.astype(BF16)
        for g in range(GMLP_GROUPS):
            cols = slice(g * GMLP_GROUP_DIM, (g + 1) * GMLP_GROUP_DIM)
            bias = bst_ref[:, g:g + 1]
            for c in range(sub // GMLP_CHUNK):
                rows = slice(c * GMLP_CHUNK, (c + 1) * GMLP_CHUNK)
                mixed = _dot(mix_w[g], vnb[rows, cols]) + bias
                g_ref[r0 + rows.start:r0 + rows.stop, cols] = (u[rows, cols] * mixed).astype(BF16)

    starts = list(range(0, tile, sub))
    zgs = {}
    for i, r0 in enumerate(starts):
        zgs[r0] = _dot(x_ref[r0:r0 + sub, :].astype(BF16), w_ref[:, 2 * DIFF_WIDTH:])
        if i:
            gate(starts[i - 1], zgs.pop(starts[i - 1]))
        project(r0)
    gate(starts[-1], zgs.pop(starts[-1]))


def _head_rows(tile, heads, width):
    return _rows(_head_view_rows(tile, heads, width), LANES)


def _head_shape(rows, heads, width):
    return jax.ShapeDtypeStruct((_head_view_rows(rows, heads, width), LANES), F32)


def _inproj(x, w_rest, w_keys, ln_g, ln_b, ws, bs_t, batch, seq, keys_on_lanes, want_vrows):
    n, d = x.shape
    tile = min(ROW_TILE, n)
    if keys_on_lanes:
        per = seq // tile
        k_shape = jax.ShapeDtypeStruct((batch, DIFF_WIDTH, seq), F32)
        k_spec = pl.BlockSpec((None, DIFF_WIDTH, tile), lambda i: (i // per, 0, i % per))
    else:
        k_shape = jax.ShapeDtypeStruct((n, DIFF_WIDTH), F32)
        k_spec = _rows(tile, DIFF_WIDTH)
    out_shape = [jax.ShapeDtypeStruct((n, DIFF_WIDTH), BF16), k_shape,
                 _head_shape(n, DIFF_HEADS, HEAD_WIDTH),
                 jax.ShapeDtypeStruct((n, GMLP_WIDTH), BF16)]
    out_specs = [_rows(tile, DIFF_WIDTH), k_spec, _head_rows(tile, DIFF_HEADS, HEAD_WIDTH), _rows(tile, GMLP_WIDTH)]
    if want_vrows:
        out_shape.append(_head_shape(n, GMLP_GROUPS, GMLP_GROUP_DIM))
        out_specs.append(_head_rows(tile, GMLP_GROUPS, GMLP_GROUP_DIM))
    return pl.pallas_call(
        functools.partial(_inproj_kernel, keys_on_lanes=keys_on_lanes),
        grid=(n // tile,),
        in_specs=[_rows(tile, d), _resident(w_rest.shape), _resident(w_keys.shape), _resident((1, GMLP_WIDTH)),
                  _resident((1, GMLP_WIDTH)), _resident(ws.shape), _resident(bs_t.shape)],
        out_specs=out_specs,
        out_shape=out_shape,
        compiler_params=_params("parallel"),
        name="inproj",
    )(x, w_rest, w_keys, ln_g, ln_b, ws, bs_t)


def _head_scalars(head, lq1_ref, lk1_ref, lq2_ref, lk2_ref, lam_init):
    slope = jnp.exp2(jnp.zeros((1, 1), F32) - 8.0 * (head + 1).astype(F32) / DIFF_HEADS) * LOG2E
    lam = (jnp.exp(jnp.sum(lq1_ref[...] * lk1_ref[...], axis=-1, keepdims=True))
           - jnp.exp(jnp.sum(lq2_ref[...] * lk2_ref[...], axis=-1, keepdims=True)) + lam_init)
    return slope, lam


def _bias_lanes(slope, rows):
    lane = lax.broadcasted_iota(jnp.int32, (1, LANES), 1)
    out = jnp.zeros((1, LANES), F32)
    rest = slope
    for i in range(BIAS_TERMS):
        term = rest.astype(BF16).astype(F32)
        out = jnp.where(lane == 2 * i, term * POS_SPLIT, jnp.where(lane == 2 * i + 1, term, out))
        rest = rest - term
    return jnp.broadcast_to(out, (rows, LANES)).astype(BF16)


def _position_rows(n):
    assert n <= POS_SPLIT * 256
    row = lax.broadcasted_iota(jnp.int32, (LANES, n), 0)
    pos = lax.broadcasted_iota(jnp.int32, (LANES, n), 1)
    val = jnp.where(row % 2 == 0, pos // POS_SPLIT, pos % POS_SPLIT)
    return jnp.where(row < 2 * BIAS_TERMS, val, 0).astype(F32).astype(BF16)


def _stack_maps(q):
    lane = lax.broadcasted_iota(jnp.int32, q.shape, 1)
    zero = jnp.zeros_like(q)
    return jnp.concatenate([jnp.where(lane < HEAD_DIM, q, zero), jnp.where(lane >= HEAD_DIM, q, zero)], axis=0)


def _near_bias(slope, t, base_q, base_k, nk):
    r = lax.broadcasted_iota(jnp.int32, (t, nk), 0) + base_q
    c = lax.broadcasted_iota(jnp.int32, (t, nk), 1) + base_k
    bias = slope * (r - jnp.abs(r - c)).astype(F32)
    return jnp.where(c // CHUNK <= r // CHUNK, bias, MASKED)


def _diff_combine(parts, values, t, lam):
    return _weighted_values(_softmax_weights(parts), values, t, lam)


def _softmax_weights(parts):
    m = functools.reduce(jnp.maximum, [jnp.max(s, axis=-1, keepdims=True) for s in parts])
    return [jnp.exp2(s - m).astype(BF16) for s in parts]


def _weighted_values(weights, values, t, lam):
    res = functools.reduce(jnp.add, [_dot(p, v) for p, v in zip(weights, values)])
    out, denom = res[:, :HEAD_WIDTH], res[:, HEAD_WIDTH:HEAD_WIDTH + 1]
    return out[:t] * (1.0 / denom[:t]) - out[t:] * (lam / denom[t:])


def _with_ones(v):
    lane = lax.broadcasted_iota(jnp.int32, v.shape, 1)
    return jnp.concatenate([v, jnp.where(lane == 0, 1.0, 0.0).astype(v.dtype)], axis=1)


def _sub_norm(o, g, lam_init):
    return o * lax.rsqrt(jnp.mean(o * o, axis=-1, keepdims=True) + LN_EPS) * g * (1.0 - lam_init)


def _attn_prompt_kernel(q_ref, k_ref, v_ref, lq1_ref, lk1_ref, lq2_ref, lk2_ref, sg_ref, o_ref,
                        kb_ref, vb_ref, *, lam_init):
    seq = q_ref.shape[0]
    tq = ATTN_Q_TILE
    head = pl.program_id(1)
    slope, lam = _head_scalars(head, lq1_ref, lk1_ref, lq2_ref, lk2_ref, lam_init)
    kb_ref[:HEAD_WIDTH, :] = k_ref[...].astype(BF16)

    @pl.when((pl.program_id(0) == 0) & (head == 0))
    def _():
        kb_ref[HEAD_WIDTH:, :] = _position_rows(seq)
        vb_ref[...] = _with_ones(jnp.zeros((seq, HEAD_WIDTH), BF16))

    for h in range(DIFF_HEADS):
        @pl.when(head == h)
        def _():
            vb_ref[:, :HEAD_WIDTH] = _head_load(v_ref, h, DIFF_HEADS, seq, HEAD_WIDTH).astype(BF16)
    q_bias = _bias_lanes(slope, 2 * tq)
    r = lax.broadcasted_iota(jnp.int32, (tq, tq), 0)
    c = lax.broadcasted_iota(jnp.int32, (tq, tq), 1)
    diag = jnp.where(c // CHUNK <= r // CHUNK, -2.0 * slope * jnp.maximum(c - r, 0).astype(F32), MASKED)
    diag = jnp.concatenate([diag, diag], axis=0)
    def scores(lo):
        qs = jnp.concatenate([_stack_maps(q_ref[lo:lo + tq, :]), q_bias], axis=1)
        parts = [_dot(qs, kb_ref[:, lo:lo + tq]) + diag]
        if lo:
            parts.append(_dot(qs, kb_ref[:, 0:lo]))
        return parts

    def finish(lo, weights):
        values = [vb_ref[lo:lo + tq, :]] + ([vb_ref[0:lo, :]] if lo else [])
        out = _weighted_values(weights, values, tq, lam)
        o_ref[lo:lo + tq, :] = _sub_norm(out, sg_ref[...], lam_init).astype(BF16)

    starts = list(range(0, seq, tq))
    parts = scores(starts[0])
    for i, lo in enumerate(starts):
        next_parts = scores(starts[i + 1]) if i + 1 < len(starts) else None
        finish(lo, _softmax_weights(parts))
        parts = next_parts


def _attn_prompt(q, k_t, v, lam_vecs, subln_g, batch, seq, lam_init):
    blk = pl.BlockSpec((seq, HEAD_WIDTH), lambda b, h: (b, h))
    vec = pl.BlockSpec((1, HEAD_DIM), lambda b, h: (0, 0))
    return pl.pallas_call(
        functools.partial(_attn_prompt_kernel, lam_init=lam_init),
        grid=(batch, DIFF_HEADS),
        in_specs=[blk, pl.BlockSpec((None, HEAD_WIDTH, seq), lambda b, h: (b, h, 0)),
                  pl.BlockSpec((_head_view_rows(seq, DIFF_HEADS, HEAD_WIDTH), LANES), lambda b, h: (b, 0)),
                  vec, vec, vec, vec, pl.BlockSpec((1, HEAD_WIDTH), lambda b, h: (0, 0))],
        out_specs=blk,
        out_shape=jax.ShapeDtypeStruct((batch * seq, DIFF_WIDTH), BF16),
        scratch_shapes=[pltpu.VMEM((HEAD_WIDTH + LANES, seq), BF16), pltpu.VMEM((seq, 2 * HEAD_WIDTH), BF16)],
        compiler_params=_params("arbitrary", "arbitrary"),
        name="attn_prompt",
    )(q, k_t, v, *lam_vecs, subln_g)


def _attn_sample_kernel(q_ref, kn_ref, vn_ref, kc_ref, vc_ref, lq1_ref, lk1_ref, lq2_ref, lk2_ref, sg_ref, o_ref,
                        *, lam_init):
    t = q_ref.shape[0]
    past = kc_ref.shape[1]
    far_pos = lax.broadcasted_iota(jnp.int32, (1, past), 1).astype(F32)
    for h in range(DIFF_HEADS):
        cols = slice(h * HEAD_WIDTH, (h + 1) * HEAD_WIDTH)
        slope, lam = _head_scalars(jnp.int32(h), lq1_ref, lk1_ref, lq2_ref, lk2_ref, lam_init)
        qs = _stack_maps(q_ref[:, cols])
        near = _near_bias(slope, t, past, past, t)
        parts = [_dot_nt(qs, kn_ref[:, cols].astype(BF16)) + jnp.concatenate([near, near], axis=0),
                 _dot(qs, kc_ref[cols, :].astype(BF16)) + slope * far_pos]
        values = [_with_ones(_head_load(vn_ref, h, DIFF_HEADS, t, HEAD_WIDTH).astype(BF16)),
                  _with_ones(_head_load(vc_ref, h, DIFF_HEADS, past, HEAD_WIDTH).astype(BF16))]
        out = _diff_combine(parts, values, t, lam)
        o_ref[:, cols] = _sub_norm(out, sg_ref[...], lam_init).astype(BF16)


def _attn_sample(q, k_new, v_new, k_cache_t, v_cache, lam_vecs, subln_g, batch, seq, lam_init):
    past = k_cache_t.shape[2]
    new = pl.BlockSpec((seq, DIFF_WIDTH), lambda b: (b, 0))
    vec = pl.BlockSpec((1, HEAD_DIM), lambda b: (0, 0))
    return pl.pallas_call(
        functools.partial(_attn_sample_kernel, lam_init=lam_init),
        grid=(batch,),
        in_specs=[new, new, _head_rows(seq, DIFF_HEADS, HEAD_WIDTH),
                  pl.BlockSpec((None, DIFF_WIDTH, past), lambda b: (b, 0, 0)),
                  _head_rows(past, DIFF_HEADS, HEAD_WIDTH),
                  vec, vec, vec, vec, pl.BlockSpec((1, HEAD_WIDTH), lambda b: (0, 0))],
        out_specs=new,
        out_shape=jax.ShapeDtypeStruct((batch * seq, DIFF_WIDTH), BF16),
        compiler_params=_params("parallel"),
        name="attn_sample",
    )(q, k_new, v_new, k_cache_t, v_cache, *lam_vecs, subln_g)


def _outproj_kernel(a_ref, g_ref, x_ref, wo_ref, lg_ref, lb_ref, wq_ref, x2_ref, q2_ref, *, alpha):
    tile = x_ref.shape[0]
    sub = min(FFN_SUB_TILE, tile)
    subs = [slice(r * sub, (r + 1) * sub) for r in range(tile // sub)]
    mixes = [_dot(jnp.concatenate([a_ref[rows, :], g_ref[rows, :]], axis=1), wo_ref[...]) for rows in subs]
    for rows, mix in zip(subs, mixes):
        x2 = _layer_norm(alpha * x_ref[rows, :] + mix, lg_ref[...], lb_ref[...])
        x2_ref[rows, :] = x2
        q2_ref[rows, :] = _dot(x2.astype(BF16), wq_ref[...]).astype(BF16)


def _outproj(a, g, x, w_out, ln_g, ln_b, wq, alpha):
    n, d = x.shape
    tile = min(ROW_TILE, n)
    return pl.pallas_call(
        functools.partial(_outproj_kernel, alpha=alpha),
        grid=(n // tile,),
        in_specs=[_rows(tile, a.shape[1]), _rows(tile, g.shape[1]), _rows(tile, d), _resident(w_out.shape),
                  _resident((1, d)), _resident((1, d)), _resident(wq.shape)],
        out_specs=[_rows(tile, d), _rows(tile, d)],
        out_shape=[jax.ShapeDtypeStruct((n, d), F32), jax.ShapeDtypeStruct((n, d), BF16)],
        compiler_params=_params("parallel"),
        name="outproj",
    )(a, g, x, w_out, ln_g, ln_b, wq)


def _cross_kernel(q_ref, mk_ref, mv_ref, o_ref, *, n_mem, seq_rows, sub):
    hd = q_ref.shape[1] // MEM_HEADS
    log2_scale = hd ** -0.5 * LOG2E
    chains = []
    for b in range(q_ref.shape[0] // seq_rows):
        for h in range(MEM_HEADS):
            mk = _head_load(mk_ref, h, MEM_HEADS, n_mem, hd, b * n_mem).astype(BF16)
            mv = _head_load(mv_ref, h, MEM_HEADS, n_mem, hd, b * n_mem).astype(BF16)
            for r0 in range(b * seq_rows, (b + 1) * seq_rows, sub):
                chains.append((slice(r0, r0 + sub), slice(h * hd, (h + 1) * hd), mk, mv))

    def scores(chain):
        rows, cols, mk, _ = chain
        return _dot_nt(q_ref[rows, cols], mk)

    s = scores(chains[0])
    for i, (rows, cols, _, mv) in enumerate(chains):
        s_next = scores(chains[i + 1]) if i + 1 < len(chains) else None
        p = jnp.exp2((s - jnp.max(s, axis=-1, keepdims=True)) * log2_scale)
        o = _dot(p.astype(BF16), mv)
        o_ref[rows, cols] = (o * (1.0 / jnp.sum(p, axis=-1, keepdims=True))).astype(BF16)
        s = s_next


def _cross(q2, mem_k, mem_v, batch, seq, n_mem):
    d = q2.shape[1]
    per_step = max(1, CROSS_STEP_ROWS // seq)
    qblk = pl.BlockSpec((per_step * seq, d), lambda i: (i, 0))
    mblk = pl.BlockSpec((per_step * _head_view_rows(n_mem, MEM_HEADS, d // MEM_HEADS), LANES), lambda i: (i, 0))
    return pl.pallas_call(
        functools.partial(_cross_kernel, n_mem=n_mem, seq_rows=seq, sub=min(CROSS_SUB_TILE, seq)),
        grid=(batch // per_step,),
        in_specs=[qblk, mblk, mblk],
        out_specs=qblk,
        out_shape=jax.ShapeDtypeStruct(q2.shape, BF16),
        compiler_params=_params("parallel"),
        name="cross",
    )(q2, mem_k, mem_v)


def _memkv_kernel(*refs, n_side):
    m_ref, wk_ref, wv_ref = refs[:3]
    side_in, (k_ref, v_ref), side_out = refs[3:3 + n_side], refs[3 + n_side:5 + n_side], refs[5 + n_side:]
    _side_cast(side_in, side_out)
    mb = m_ref[...].astype(BF16)
    k = _dot(mb, wk_ref[...].astype(BF16))
    v = _dot(mb, wv_ref[...].astype(BF16))
    hd = k.shape[1] // MEM_HEADS
    for h in range(MEM_HEADS):
        _head_store(k_ref, h, MEM_HEADS, k[:, h * hd:(h + 1) * hd])
        _head_store(v_ref, h, MEM_HEADS, v[:, h * hd:(h + 1) * hd])


def _memkv(mem, wk, wv, side=()):
    n, d = mem.shape
    hd = d // MEM_HEADS
    tile = min(MEMKV_TILE, n)
    side_in, side_out, side_shapes = _side_cast_specs(side, n // tile)
    outs = pl.pallas_call(
        functools.partial(_memkv_kernel, n_side=len(side)),
        grid=(n // tile,),
        in_specs=[_rows(tile, d), _resident(wk.shape), _resident(wv.shape)] + side_in,
        out_specs=[_head_rows(tile, MEM_HEADS, hd)] * 2 + side_out,
        out_shape=[_head_shape(n, MEM_HEADS, hd)] * 2 + side_shapes,
        compiler_params=_params("parallel"),
        name="memkv",
    )(mem, wk, wv, *side)
    return outs[0], outs[1], tuple(outs[2:])


def _gating_weights(ws, bs, seq):
    n = min(seq, GMLP_CHUNK)
    rep = GMLP_CHUNK // n
    w = ws[:, :n, :n]
    if rep > 1:
        eye = jnp.eye(rep, dtype=ws.dtype)
        w = jnp.einsum('ab,gts->gatbs', eye, w).reshape(GMLP_GROUPS, GMLP_CHUNK, GMLP_CHUNK)
    return w, jnp.tile(bs[:, :n], (1, rep)).T


def _trunk(x, batch, seq, mem_k, mem_v, cache, p, layer_idx, alpha, want_vrows):
    lam_init = 0.8 - 0.6 * math.exp(-0.3 * layer_idx)
    ws, bs_t = _gating_weights(p['gmlp_ws'], p['gmlp_bs'], seq)
    lam_vecs = (p['lambda_q1'], p['lambda_k1'], p['lambda_q2'], p['lambda_k2'])

    if 'ffn2_w_gu' in p:
        x1, _ = _ffn(x, p['ffn1_w_gu'], p['ffn1_w_down'], p['ln1_g'], p['ln1_b'], alpha)
    else:
        x1, (p['ffn2_w_gu'], p['ffn2_w_down']) = _ffn(x, p['ffn1_w_gu'], p['ffn1_w_down'], p['ln1_g'], p['ln1_b'],
                                                      alpha, side=(p['ffn2_w_gu_f32'], p['ffn2_w_down_f32']))
    keys_on_lanes = cache is None
    outs = _inproj(x1, p['w_in_rest'], p['w_in_keys_t'] if keys_on_lanes else p['w_in_keys'], p['gmlp_ln_g'],
                   p['gmlp_ln_b'], ws, bs_t, batch, seq, keys_on_lanes, want_vrows)
    q, k, v, gated = outs[:4]
    if cache is None:
        a = _attn_prompt(q, k, v, lam_vecs, p['subln_g'], batch, seq, lam_init)
    else:
        a = _attn_sample(q, k, v, cache[0], cache[1], lam_vecs, p['subln_g'], batch, seq, lam_init)
    x2, q2 = _outproj(a, gated, x1, p['w_out'], p['ln2_g'], p['ln2_b'], p['cross_wq'], alpha)
    n_mem = mem_k.size // (batch * x.shape[1])
    o = _cross(q2, mem_k, mem_v, batch, seq, n_mem)
    y, _ = _ffn(x2, p['ffn2_w_gu'], p['ffn2_w_down'], p['ln4_g'], p['ln4_b'], alpha,
                proj=(o, p['cross_wo'], p['ln3_g'], p['ln3_b']))
    return y, k, v, (outs[4] if want_vrows else None)


_MATRICES = ('w_in', 'w_out', 'cross_wq', 'cross_wo')
_SIDE_CAST_MATRICES = ('ffn1_w_gu', 'ffn1_w_down', 'ffn2_w_gu', 'ffn2_w_down')
_ROW_VECTORS = ('ln1_g', 'ln1_b', 'lambda_q1', 'lambda_k1', 'lambda_q2', 'lambda_k2', 'subln_g', 'gmlp_ln_g',
                'gmlp_ln_b', 'ln2_g', 'ln2_b', 'ln3_g', 'ln3_b', 'ln4_g', 'ln4_b')


def kernel(x_prompt, x_sample, cache_k, cache_v, cache_mem_k, cache_mem_v, mem_prompt, ffn1_w_gu, ffn1_w_down, ln1_g, ln1_b, w_in, lambda_q1, lambda_k1, lambda_q2, lambda_k2, subln_g, gmlp_ln_g, gmlp_ln_b, gmlp_ws, gmlp_bs, w_out, ln2_g, ln2_b, cross_wq, cross_wk, cross_wv, cross_wo, ln3_g, ln3_b, ffn2_w_gu, ffn2_w_down, ln4_g, ln4_b):
    weights = dict(ffn1_w_gu=ffn1_w_gu, ffn1_w_down=ffn1_w_down, ln1_g=ln1_g, ln1_b=ln1_b, w_in=w_in,
                   lambda_q1=lambda_q1, lambda_k1=lambda_k1, lambda_q2=lambda_q2, lambda_k2=lambda_k2,
                   subln_g=subln_g, gmlp_ln_g=gmlp_ln_g, gmlp_ln_b=gmlp_ln_b, gmlp_ws=gmlp_ws, gmlp_bs=gmlp_bs,
                   w_out=w_out, ln2_g=ln2_g, ln2_b=ln2_b, cross_wq=cross_wq, cross_wk=cross_wk, cross_wv=cross_wv,
                   cross_wo=cross_wo, ln3_g=ln3_g, ln3_b=ln3_b, ffn2_w_gu=ffn2_w_gu, ffn2_w_down=ffn2_w_down,
                   ln4_g=ln4_g, ln4_b=ln4_b)
    depth = w_in.shape[0]
    alpha = (2 * depth) ** 0.25
    bp, tp, d = x_prompt.shape
    bs_, ts, _ = x_sample.shape
    n_mem = mem_prompt.shape[1]
    past = cache_k.shape[2]
    mem_hd = d // MEM_HEADS

    xp =x_prompt.reshape(bp * tp, d)
    xs = x_sample.reshape(bs_ * ts, d)
    mem2d = mem_prompt.reshape(bp * n_mem, d)
    per_layer = [[] for _ in range(7)]
    for l in range(depth):
        p = {}
        for name, w in weights.items():
            w = w[l]
            if name in _MATRICES:
                w = w.astype(BF16)
            elif name in _SIDE_CAST_MATRICES:
                name = name + '_f32'
            elif name in _ROW_VECTORS:
                w = w.reshape(1, -1)
            p[name] = w
        w_in_l = p.pop('w_in')
        p['w_in_rest'] = jnp.concatenate([w_in_l[:, :DIFF_WIDTH], w_in_l[:, 2 * DIFF_WIDTH:]], axis=1)
        p['w_in_keys'] = w_in_l[:, DIFF_WIDTH:2 * DIFF_WIDTH]
        p['w_in_keys_t'] = p['w_in_keys'].T
        mem_k, mem_v, (p['ffn1_w_gu'], p['ffn1_w_down']) = _memkv(
            mem2d, p['cross_wk'], p['cross_wv'], side=(p['ffn1_w_gu_f32'], p['ffn1_w_down_f32']))
        xp, kp_t, vp, _ = _trunk(xp, bp, tp, mem_k, mem_v, None, p, l, alpha, False)
        cache = (jnp.transpose(cache_k[l], (0, 2, 3, 4, 1)).reshape(bs_, DIFF_WIDTH, past),
                 _to_head_view(cache_v[l]))
        xs, ks, vs, gvs = _trunk(xs, bs_, ts, _to_head_view(cache_mem_k[l]), _to_head_view(cache_mem_v[l]),
                                 cache, p, l, alpha, True)
        kp = jnp.transpose(kp_t.reshape(bp, DIFF_HEADS, 2, HEAD_DIM, tp), (0, 4, 1, 2, 3))
        vals = (kp, _from_head_view(vp, (bp, tp), DIFF_HEADS, HEAD_WIDTH),
                _from_head_view(mem_k, (bp, n_mem), MEM_HEADS, mem_hd),
                _from_head_view(mem_v, (bp, n_mem), MEM_HEADS, mem_hd), ks,
                _from_head_view(vs, (bs_, ts), DIFF_HEADS, HEAD_WIDTH),
                _from_head_view(gvs, (bs_, ts), GMLP_GROUPS, GMLP_GROUP_DIM))
        for acc, val in zip(per_layer, vals):
            acc.append(val)

    kp, vp, mk, mv, ks, vs, gvs = (vals[0][None] if depth == 1 else jnp.stack(vals) for vals in per_layer)
    return (xp.reshape(bp, tp, d), xs.reshape(bs_, ts, d),
            kp.reshape(depth, bp, tp, DIFF_HEADS, 2, HEAD_DIM),
            vp.reshape(depth, bp, tp, DIFF_HEADS, HEAD_WIDTH),
            mk.reshape(depth, bp, n_mem, MEM_HEADS, mem_hd),
            mv.reshape(depth, bp, n_mem, MEM_HEADS, mem_hd),
            ks.reshape(depth, bs_, ts, DIFF_HEADS, 2, HEAD_DIM),
            vs.reshape(depth, bs_, ts, DIFF_HEADS, HEAD_WIDTH),
            gvs.reshape(depth, bs_, ts, GMLP_GROUPS, GMLP_GROUP_DIM))
```

```python
import functools
import math

import jax
import jax.numpy as jnp
from jax import lax
from jax.experimental import pallas as pl
from jax.experimental.pallas import tpu as pltpu

F32 = jnp.float32
BF16 = jnp.bfloat16

CHUNK = 64
DIFF_HEADS = 4
HEAD_DIM = 64
HEAD_WIDTH = 2 * HEAD_DIM
DIFF_WIDTH = DIFF_HEADS * HEAD_WIDTH
GMLP_GROUPS = 4
GMLP_CHUNK = 128
GMLP_GROUP_DIM = 128
GMLP_WIDTH = GMLP_GROUPS * GMLP_GROUP_DIM
MEM_HEADS = 4
LN_EPS = 1e-5
MASKED = -1e30
LANES = 128
BF16_SUBLANES = 16
LOG2E = math.log2(math.e)
Q_SCALE = HEAD_DIM ** -0.5 * LOG2E
POS_SPLIT = 64
BIAS_TERMS = 3

ROW_TILE = 1024
FF_CHUNK = 256
FFN_SUB_TILE = 256
MEMKV_TILE = 512
ATTN_Q_TILE = 256
ATTN_HEADS_PER_STEP = 4
CROSS_STEP_ROWS = 128
CROSS_SUB_TILE = 512
VMEM_LIMIT_BYTES = 56 * 1024 * 1024


def _params(*semantics):
    return pltpu.CompilerParams(dimension_semantics=semantics, vmem_limit_bytes=VMEM_LIMIT_BYTES)


def _resident(shape):
    return pl.BlockSpec(shape, lambda *_: (0,) * len(shape), pipeline_mode=pl.Buffered(1))


def _rows(tile, width):
    return pl.BlockSpec((tile, width), lambda i: (i, 0))


def _layer_norm(h, g, b):
    mu = jnp.mean(h, axis=-1, keepdims=True)
    d = h - mu
    var = jnp.mean(d * d, axis=-1, keepdims=True)
    return d * lax.rsqrt(var + LN_EPS) * g + b


def _dot(a, b):
    return jnp.dot(a, b, preferred_element_type=F32)


def _dot_nt(a, b):
    return lax.dot_general(a, b, (((1,), (1,)), ((), ())), preferred_element_type=F32)


def _head_view_rows(rows, heads, width):
    return rows * heads * (width // LANES)


def _head_load(ref, h, heads, rows, width, row0=0):
    tiles = width // LANES
    parts = [ref[pl.ds((row0 * tiles + j) * heads + h, rows, stride=heads * tiles), :] for j in range(tiles)]
    return parts[0] if tiles == 1 else jnp.concatenate(parts, axis=1)


def _head_store(ref, h, heads, value, row0=0):
    rows, width = value.shape
    tiles = width // LANES
    for j in range(tiles):
        start = (row0 * tiles + j) * heads + h
        ref[pl.ds(start, rows, stride=heads * tiles), :] = value[:, j * LANES:(j + 1) * LANES]


def _to_head_view(x):
    *lead, heads, width = x.shape
    tiles = width // LANES
    x = x.reshape(*lead, heads, tiles, LANES)
    x = jnp.swapaxes(x, -3, -2)
    return x.reshape(-1, LANES)


def _from_head_view(x, lead, heads, width):
    tiles = width // LANES
    x = x.reshape(*lead, tiles, heads, LANES)
    x = jnp.swapaxes(x, -3, -2)
    return x.reshape(*lead, heads, width)


def _side_cast_specs(weights, steps):
    in_specs, out_specs, out_shapes = [], [], []
    for w in weights:
        rows, cols = w.shape
        units = rows // BF16_SUBLANES
        assert units * BF16_SUBLANES == rows
        blocks = max(b for b in range(1, min(steps, units) + 1) if units % b == 0)
        spec = pl.BlockSpec((rows // blocks, cols), lambda i, last=blocks - 1: (jnp.minimum(i, last), 0))
        in_specs.append(spec)
        out_specs.append(spec)
        out_shapes.append(jax.ShapeDtypeStruct((rows, cols), BF16))
    return in_specs, out_specs, out_shapes


def _side_cast(src_refs, dst_refs):
    for src, dst in zip(src_refs, dst_refs):
        dst[...] = src[...].astype(BF16)


def _ffn_kernel(*refs, alpha, d_ff, with_proj, n_side):
    n_in = 9 if with_proj else 5
    side_in, side_out = refs[n_in:n_in + n_side], refs[n_in + n_side + 1:n_in + 2 * n_side + 1]
    out_ref, act_ref = refs[n_in + n_side], refs[-1]
    if with_proj:
        o_ref, xin_ref, wo_ref, lpg_ref, lpb_ref, wgu_ref, wd_ref, g_ref, b_ref = refs[:n_in]
    else:
        xin_ref, wgu_ref, wd_ref, g_ref, b_ref = refs[:n_in]
    _side_cast(side_in, side_out)
    tile = xin_ref.shape[0]
    sub = min(FFN_SUB_TILE, tile)
    subs = [slice(r * sub, (r + 1) * sub) for r in range(tile // sub)]
    xs = []
    for rows in subs:
        x = xin_ref[rows, :]
        if with_proj:
            x = _layer_norm(alpha * x + _dot(o_ref[rows, :], wo_ref[...]), lpg_ref[...], lpb_ref[...])
        xs.append(x)
    for rows, x in zip(subs, xs):
        xb = x.astype(BF16)
        for c in range(d_ff // FF_CHUNK):
            lo = c * FF_CHUNK
            gate = _dot(xb, wgu_ref[:, lo:lo + FF_CHUNK])
            up = _dot(xb, wgu_ref[:, d_ff + lo:d_ff + lo + FF_CHUNK])
            act_ref[rows, lo:lo + FF_CHUNK] = (gate * jax.nn.sigmoid(gate) * up).astype(BF16)
    ys = [_dot(act_ref[rows, :], wd_ref[...]) for rows in subs]
    for rows, x, y in zip(subs, xs, ys):
        out_ref[rows, :] = _layer_norm(alpha * x + 0.5 * y, g_ref[...], b_ref[...])


def _ffn(x, w_gu, w_down, ln_g, ln_b, alpha, proj=None, side=()):
    n, d = x.shape
    d_ff = w_down.shape[0]
    tile = min(ROW_TILE, n)
    side_in, side_out, side_shapes = _side_cast_specs(side, n // tile)
    in_specs, args = [], []
    if proj is not None:
        o, wo, pg, pb = proj
        in_specs += [_rows(tile, o.shape[1])]
        args += [o]
    in_specs += [_rows(tile, d)]
    args += [x]
    if proj is not None:
        in_specs += [_resident(wo.shape), _resident((1, d)), _resident((1, d))]
        args += [wo, pg, pb]
    in_specs += [_resident(w_gu.shape), _resident(w_down.shape), _resident((1, d)), _resident((1, d))]
    args += [w_gu, w_down, ln_g, ln_b]
    outs = pl.pallas_call(
        functools.partial(_ffn_kernel, alpha=alpha, d_ff=d_ff, with_proj=proj is not None, n_side=len(side)),
        grid=(n // tile,),
        in_specs=in_specs + side_in,
        out_specs=[_rows(tile, d)] + side_out,
        out_shape=[jax.ShapeDtypeStruct((n, d), F32)] + side_shapes,
        scratch_shapes=[pltpu.VMEM((tile, d_ff), BF16)],
        compiler_params=_params("parallel"),
        name="ffn_proj" if proj is not None else "ffn",
    )(*args, *side)
    return outs[0], tuple(outs[1:])


def _inproj_kernel(x_ref, w_ref, wk_ref, lng_ref, lnb_ref, ws_ref, bst_ref, q_ref, k_ref, v_ref, g_ref, *vrows_ref,
                   keys_on_lanes):
    tile = x_ref.shape[0]
    sub = min(FFN_SUB_TILE, tile)
    row = lax.broadcasted_iota(jnp.int32, (GMLP_CHUNK, GMLP_CHUNK), 0)
    col = lax.broadcasted_iota(jnp.int32, (GMLP_CHUNK, GMLP_CHUNK), 1)
    mix_w = [jnp.where(row >= col, ws_ref[g], 0.0).astype(BF16) for g in range(GMLP_GROUPS)]

    def project(r0):
        xb = x_ref[r0:r0 + sub, :].astype(BF16)
        z = _dot(xb, w_ref[:, :2 * DIFF_WIDTH])
        q_ref[r0:r0 + sub, :] = (z[:, :DIFF_WIDTH] * Q_SCALE).astype(BF16)
        if keys_on_lanes:
            k_ref[:, r0:r0 + sub] = _dot_nt(wk_ref[...], xb)
        else:
            k_ref[r0:r0 + sub, :] = _dot(xb, wk_ref[...])
        for h in range(DIFF_HEADS):
            _head_store(v_ref, h, DIFF_HEADS, z[:, DIFF_WIDTH + h * HEAD_WIDTH:DIFF_WIDTH + (h + 1) * HEAD_WIDTH], r0)

    def gate(r0, zg):
        u = jax.nn.gelu(zg[:, :GMLP_WIDTH])
        vn = _layer_norm(jax.nn.gelu(zg[:, GMLP_WIDTH:]), lng_ref[...], lnb_ref[...])
        if vrows_ref:
            for g in range(GMLP_GROUPS):
                _head_store(vrows_ref[0], g, GMLP_GROUPS, vn[:, g * GMLP_GROUP_DIM:(g + 1) * GMLP_GROUP_DIM], r0)
        vnb = vn.astype(BF16)
        for g in range(GMLP_GROUPS):
            cols = slice(g * GMLP_GROUP_DIM, (g + 1) * GMLP_GROUP_DIM)
            bias = bst_ref[:, g:g + 1]
            for c in range(sub // GMLP_CHUNK):
                rows = slice(c * GMLP_CHUNK, (c + 1) * GMLP_CHUNK)
                mixed = _dot(mix_w[g], vnb[rows, cols]) + bias
                g_ref[r0 + rows.start:r0 + rows.stop, cols] = (u[rows, cols] * mixed).astype(BF16)

    starts = list(range(0, tile, sub))
    zgs = {}
    for i, r0 in enumerate(starts):
        zgs[r0] = _dot(x_ref[r0:r0 + sub, :].astype(BF16), w_ref[:, 2 * DIFF_WIDTH:])
        if i:
            gate(starts[i - 1], zgs.pop(starts[i - 1]))
        project(r0)
    gate(starts[-1], zgs.pop(starts[-1]))


def _head_rows(tile, heads, width):
    return _rows(_head_view_rows(tile, heads, width), LANES)


def _head_shape(rows, heads, width):
    return jax.ShapeDtypeStruct((_head_view_rows(rows, heads, width), LANES), F32)


def _inproj(x, w_rest, w_keys, ln_g, ln_b, ws, bs_t, batch, seq, keys_on_lanes, want_vrows):
    n, d = x.shape
    tile = min(ROW_TILE, n)
    if keys_on_lanes:
        per = seq // tile
        k_shape = jax.ShapeDtypeStruct((batch, DIFF_WIDTH, seq), F32)
        k_spec = pl.BlockSpec((None, DIFF_WIDTH, tile), lambda i: (i // per, 0, i % per))
    else:
        k_shape = jax.ShapeDtypeStruct((n, DIFF_WIDTH), F32)
        k_spec = _rows(tile, DIFF_WIDTH)
    out_shape = [jax.ShapeDtypeStruct((n, DIFF_WIDTH), BF16), k_shape,
                 _head_shape(n, DIFF_HEADS, HEAD_WIDTH),
                 jax.ShapeDtypeStruct((n, GMLP_WIDTH), BF16)]
    out_specs = [_rows(tile, DIFF_WIDTH), k_spec, _head_rows(tile, DIFF_HEADS, HEAD_WIDTH), _rows(tile, GMLP_WIDTH)]
    if want_vrows:
        out_shape.append(_head_shape(n, GMLP_GROUPS, GMLP_GROUP_DIM))
        out_specs.append(_head_rows(tile, GMLP_GROUPS, GMLP_GROUP_DIM))
    return pl.pallas_call(
        functools.partial(_inproj_kernel, keys_on_lanes=keys_on_lanes),
        grid=(n // tile,),
        in_specs=[_rows(tile, d), _resident(w_rest.shape), _resident(w_keys.shape), _resident((1, GMLP_WIDTH)),
                  _resident((1, GMLP_WIDTH)), _resident(ws.shape), _resident(bs_t.shape)],
        out_specs=out_specs,
        out_shape=out_shape,
        compiler_params=_params("parallel"),
        name="inproj",
    )(x, w_rest, w_keys, ln_g, ln_b, ws, bs_t)


def _head_scalars(head, lq1_ref, lk1_ref, lq2_ref, lk2_ref, lam_init):
    slope = jnp.exp2(jnp.zeros((1, 1), F32) - 8.0 * (head + 1).astype(F32) / DIFF_HEADS) * LOG2E
    lam = (jnp.exp(jnp.sum(lq1_ref[...] * lk1_ref[...], axis=-1, keepdims=True))
           - jnp.exp(jnp.sum(lq2_ref[...] * lk2_ref[...], axis=-1, keepdims=True)) + lam_init)
    return slope, lam


def _bias_lanes(slope, rows):
    lane = lax.broadcasted_iota(jnp.int32, (1, LANES), 1)
    out = jnp.zeros((1, LANES), F32)
    rest = slope
    for i in range(BIAS_TERMS):
        term = rest.astype(BF16).astype(F32)
        out = jnp.where(lane == 2 * i, term * POS_SPLIT, jnp.where(lane == 2 * i + 1, term, out))
        rest = rest - term
    return jnp.broadcast_to(out, (rows, LANES)).astype(BF16)


def _position_rows(n):
    assert n <= POS_SPLIT * 256
    row = lax.broadcasted_iota(jnp.int32, (LANES, n), 0)
    pos = lax.broadcasted_iota(jnp.int32, (LANES, n), 1)
    val = jnp.where(row % 2 == 0, pos // POS_SPLIT, pos % POS_SPLIT)
    return jnp.where(row < 2 * BIAS_TERMS, val, 0).astype(F32).astype(BF16)


def _stack_maps(q):
    lane = lax.broadcasted_iota(jnp.int32, q.shape, 1)
    zero = jnp.zeros_like(q)
    return jnp.concatenate([jnp.where(lane < HEAD_DIM, q, zero), jnp.where(lane >= HEAD_DIM, q, zero)], axis=0)


def _near_bias(slope, t, base_q, base_k, nk):
    r = lax.broadcasted_iota(jnp.int32, (t, nk), 0) + base_q
    c = lax.broadcasted_iota(jnp.int32, (t, nk), 1) + base_k
    bias = slope * (r - jnp.abs(r - c)).astype(F32)
    return jnp.where(c // CHUNK <= r // CHUNK, bias, MASKED)


def _diff_combine(parts, values, t, lam):
    return _weighted_values(_softmax_weights(parts), values, t, lam)


def _softmax_weights(parts):
    m = functools.reduce(jnp.maximum, [jnp.max(s, axis=-1, keepdims=True) for s in parts])
    return [jnp.exp2(s - m).astype(BF16) for s in parts]


def _weighted_values(weights, values, t, lam):
    res = functools.reduce(jnp.add, [_dot(p, v) for p, v in zip(weights, values)])
    out, denom = res[:, :HEAD_WIDTH], res[:, HEAD_WIDTH:HEAD_WIDTH + 1]
    return out[:t] * (1.0 / denom[:t]) - out[t:] * (lam / denom[t:])


def _with_ones(v):
    lane = lax.broadcasted_iota(jnp.int32, v.shape, 1)
    return jnp.concatenate([v, jnp.where(lane == 0, 1.0, 0.0).astype(v.dtype)], axis=1)


def _sub_norm(o, g, lam_init):
    return o * lax.rsqrt(jnp.mean(o * o, axis=-1, keepdims=True) + LN_EPS) * g * (1.0 - lam_init)


def _attn_prompt_kernel(q_ref, k_ref, v_ref, lq1_ref, lk1_ref, lq2_ref, lk2_ref, sg_ref, o_ref,
                        kb_ref, vb_ref, *, lam_init, group):
    seq = q_ref.shape[0]
    tq = ATTN_Q_TILE
    step = pl.program_id(1)

    @pl.when((pl.program_id(0) == 0) & (step == 0))
    def _():
        for j in range(group):
            kb_ref[j, HEAD_WIDTH:, :] = _position_rows(seq)
            vb_ref[j] = _with_ones(jnp.zeros((seq, HEAD_WIDTH), BF16))

    for j in range(group):
        kb_ref[j, :HEAD_WIDTH, :] = k_ref[j * HEAD_WIDTH:(j + 1) * HEAD_WIDTH, :].astype(BF16)
    for g in range(DIFF_HEADS // group):
        @pl.when(step == g)
        def _():
            for j in range(group):
                vb_ref[j, :, :HEAD_WIDTH] = _head_load(v_ref, g * group + j, DIFF_HEADS, seq, HEAD_WIDTH).astype(BF16)

    r = lax.broadcasted_iota(jnp.int32, (tq, tq), 0)
    c = lax.broadcasted_iota(jnp.int32, (tq, tq), 1)
    allowed = c // CHUNK <= r // CHUNK
    lead = -2.0 * jnp.maximum(c - r, 0).astype(F32)
    q_bias, diag = [], []
    for j in range(group):
        slope, lam = _head_scalars(step * group + j, lq1_ref, lk1_ref, lq2_ref, lk2_ref, lam_init)
        q_bias.append(_bias_lanes(slope, 2 * tq))
        d = jnp.where(allowed, slope * lead, MASKED)
        diag.append(jnp.concatenate([d, d], axis=0))

    def scores(j, lo):
        cols = slice(j * HEAD_WIDTH, (j + 1) * HEAD_WIDTH)
        qs = jnp.concatenate([_stack_maps(q_ref[lo:lo + tq, cols]), q_bias[j]], axis=1)
        parts = [_dot(qs, kb_ref[j, :, lo:lo + tq]) + diag[j]]
        if lo:
            parts.append(_dot(qs, kb_ref[j, :, 0:lo]))
        return parts

    def finish(j, lo, weights):
        values = [vb_ref[j, lo:lo + tq, :]] + ([vb_ref[j, 0:lo, :]] if lo else [])
        out = _weighted_values(weights, values, tq, lam)
        o_ref[lo:lo + tq, j * HEAD_WIDTH:(j + 1) * HEAD_WIDTH] = _sub_norm(out, sg_ref[...], lam_init).astype(BF16)

    chains = [(j, lo) for lo in range(0, seq, tq) for j in range(group)]
    parts = scores(*chains[0])
    for i, chain in enumerate(chains):
        next_parts = scores(*chains[i + 1]) if i + 1 < len(chains) else None
        finish(*chain, _softmax_weights(parts))
        parts = next_parts


def _attn_prompt(q, k_t, v, lam_vecs, subln_g, batch, seq, lam_init):
    group = ATTN_HEADS_PER_STEP
    blk = pl.BlockSpec((seq, group * HEAD_WIDTH), lambda b, g: (b, g))
    vec = pl.BlockSpec((1, HEAD_DIM), lambda b, g: (0, 0))
    return pl.pallas_call(
        functools.partial(_attn_prompt_kernel, lam_init=lam_init, group=group),
        grid=(batch, DIFF_HEADS // group),
        in_specs=[blk, pl.BlockSpec((None, group * HEAD_WIDTH, seq), lambda b, g: (b, g, 0)),
                  pl.BlockSpec((_head_view_rows(seq, DIFF_HEADS, HEAD_WIDTH), LANES), lambda b, g: (b, 0)),
                  vec, vec, vec, vec, pl.BlockSpec((1, HEAD_WIDTH), lambda b, g: (0, 0))],
        out_specs=blk,
        out_shape=jax.ShapeDtypeStruct((batch * seq, DIFF_WIDTH), BF16),
        scratch_shapes=[pltpu.VMEM((group, HEAD_WIDTH + LANES, seq), BF16),
                        pltpu.VMEM((group, seq, 2 * HEAD_WIDTH), BF16)],
        compiler_params=_params("arbitrary", "arbitrary"),
        name="attn_prompt",
    )(q, k_t, v, *lam_vecs, subln_g)


def _attn_sample_kernel(q_ref, kn_ref, vn_ref, kc_ref, vc_ref, lq1_ref, lk1_ref, lq2_ref, lk2_ref, sg_ref, o_ref,
                        *, lam_init):
    t = q_ref.shape[0]
    past = kc_ref.shape[1]
    far_pos = lax.broadcasted_iota(jnp.int32, (1, past), 1).astype(F32)
    for h in range(DIFF_HEADS):
        cols = slice(h * HEAD_WIDTH, (h + 1) * HEAD_WIDTH)
        slope, lam = _head_scalars(jnp.int32(h), lq1_ref, lk1_ref, lq2_ref, lk2_ref, lam_init)
        qs = _stack_maps(q_ref[:, cols])
        near = _near_bias(slope, t, past, past, t)
        parts = [_dot_nt(qs, kn_ref[:, cols].astype(BF16)) + jnp.concatenate([near, near], axis=0),
                 _dot(qs, kc_ref[cols, :].astype(BF16)) + slope * far_pos]
        values = [_with_ones(_head_load(vn_ref, h, DIFF_HEADS, t, HEAD_WIDTH).astype(BF16)),
                  _with_ones(_head_load(vc_ref, h, DIFF_HEADS, past, HEAD_WIDTH).astype(BF16))]
        out = _diff_combine(parts, values, t, lam)
        o_ref[:, cols] = _sub_norm(out, sg_ref[...], lam_init).astype(BF16)


def _attn_sample(q, k_new, v_new, k_cache_t, v_cache, lam_vecs, subln_g, batch, seq, lam_init):
    past = k_cache_t.shape[2]
    new = pl.BlockSpec((seq, DIFF_WIDTH), lambda b: (b, 0))
    vec = pl.BlockSpec((1, HEAD_DIM), lambda b: (0, 0))
    return pl.pallas_call(
        functools.partial(_attn_sample_kernel, lam_init=lam_init),
        grid=(batch,),
        in_specs=[new, new, _head_rows(seq, DIFF_HEADS, HEAD_WIDTH),
                  pl.BlockSpec((None, DIFF_WIDTH, past), lambda b: (b, 0, 0)),
                  _head_rows(past, DIFF_HEADS, HEAD_WIDTH),
                  vec, vec, vec, vec, pl.BlockSpec((1, HEAD_WIDTH), lambda b: (0, 0))],
        out_specs=new,
        out_shape=jax.ShapeDtypeStruct((batch * seq, DIFF_WIDTH), BF16),
        compiler_params=_params("parallel"),
        name="attn_sample",
    )(q, k_new, v_new, k_cache_t, v_cache, *lam_vecs, subln_g)


def _outproj_kernel(a_ref, g_ref, x_ref, wo_ref, lg_ref, lb_ref, wq_ref, x2_ref, q2_ref, *, alpha):
    tile = x_ref.shape[0]
    sub = min(FFN_SUB_TILE, tile)
    subs = [slice(r * sub, (r + 1) * sub) for r in range(tile // sub)]
    mixes = [_dot(jnp.concatenate([a_ref[rows, :], g_ref[rows, :]], axis=1), wo_ref[...]) for rows in subs]
    for rows, mix in zip(subs, mixes):
        x2 = _layer_norm(alpha * x_ref[rows, :] + mix, lg_ref[...], lb_ref[...])
        x2_ref[rows, :] = x2
        q2_ref[rows, :] = _dot(x2.astype(BF16), wq_ref[...]).astype(BF16)


def _outproj(a, g, x, w_out, ln_g, ln_b, wq, alpha):
    n, d = x.shape
    tile = min(ROW_TILE, n)
    return pl.pallas_call(
        functools.partial(_outproj_kernel, alpha=alpha),
        grid=(n // tile,),
        in_specs=[_rows(tile, a.shape[1]), _rows(tile, g.shape[1]), _rows(tile, d), _resident(w_out.shape),
                  _resident((1, d)), _resident((1, d)), _resident(wq.shape)],
        out_specs=[_rows(tile, d), _rows(tile, d)],
        out_shape=[jax.ShapeDtypeStruct((n, d), F32), jax.ShapeDtypeStruct((n, d), BF16)],
        compiler_params=_params("parallel"),
        name="outproj",
    )(a, g, x, w_out, ln_g, ln_b, wq)


def _cross_kernel(q_ref, mk_ref, mv_ref, o_ref, *, n_mem, seq_rows, sub):
    hd = q_ref.shape[1] // MEM_HEADS
    log2_scale = hd ** -0.5 * LOG2E
    chains = []
    for b in range(q_ref.shape[0] // seq_rows):
        for h in range(MEM_HEADS):
            mk = _head_load(mk_ref, h, MEM_HEADS, n_mem, hd, b * n_mem).astype(BF16)
            mv = _head_load(mv_ref, h, MEM_HEADS, n_mem, hd, b * n_mem).astype(BF16)
            for r0 in range(b * seq_rows, (b + 1) * seq_rows, sub):
                chains.append((slice(r0, r0 + sub), slice(h * hd, (h + 1) * hd), mk, mv))

    def scores(chain):
        rows, cols, mk, _ = chain
        return _dot_nt(q_ref[rows, cols], mk)

    s = scores(chains[0])
    for i, (rows, cols, _, mv) in enumerate(chains):
        s_next = scores(chains[i + 1]) if i + 1 < len(chains) else None
        p = jnp.exp2((s - jnp.max(s, axis=-1, keepdims=True)) * log2_scale)
        o = _dot(p.astype(BF16), mv)
        o_ref[rows, cols] = (o * (1.0 / jnp.sum(p, axis=-1, keepdims=True))).astype(BF16)
        s = s_next


def _cross(q2, mem_k, mem_v, batch, seq, n_mem):
    d = q2.shape[1]
    per_step = max(1, CROSS_STEP_ROWS // seq)
    qblk = pl.BlockSpec((per_step * seq, d), lambda i: (i, 0))
    mblk = pl.BlockSpec((per_step * _head_view_rows(n_mem, MEM_HEADS, d // MEM_HEADS), LANES), lambda i: (i, 0))
    return pl.pallas_call(
        functools.partial(_cross_kernel, n_mem=n_mem, seq_rows=seq, sub=min(CROSS_SUB_TILE, seq)),
        grid=(batch // per_step,),
        in_specs=[qblk, mblk, mblk],
        out_specs=qblk,
        out_shape=jax.ShapeDtypeStruct(q2.shape, BF16),
        compiler_params=_params("parallel"),
        name="cross",
    )(q2, mem_k, mem_v)


def _memkv_kernel(*refs, n_side):
    m_ref, wk_ref, wv_ref = refs[:3]
    side_in, (k_ref, v_ref), side_out = refs[3:3 + n_side], refs[3 + n_side:5 + n_side], refs[5 + n_side:]
    _side_cast(side_in, side_out)
    mb = m_ref[...].astype(BF16)
    k = _dot(mb, wk_ref[...].astype(BF16))
    v = _dot(mb, wv_ref[...].astype(BF16))
    hd = k.shape[1] // MEM_HEADS
    for h in range(MEM_HEADS):
        _head_store(k_ref, h, MEM_HEADS, k[:, h * hd:(h + 1) * hd])
        _head_store(v_ref, h, MEM_HEADS, v[:, h * hd:(h + 1) * hd])


def _memkv(mem, wk, wv, side=()):
    n, d = mem.shape
    hd = d // MEM_HEADS
    tile = min(MEMKV_TILE, n)
    side_in, side_out, side_shapes = _side_cast_specs(side, n // tile)
    outs = pl.pallas_call(
        functools.partial(_memkv_kernel, n_side=len(side)),
        grid=(n // tile,),
        in_specs=[_rows(tile, d), _resident(wk.shape), _resident(wv.shape)] + side_in,
        out_specs=[_head_rows(tile, MEM_HEADS, hd)] * 2 + side_out,
        out_shape=[_head_shape(n, MEM_HEADS, hd)] * 2 + side_shapes,
        compiler_params=_params("parallel"),
        name="memkv",
    )(mem, wk, wv, *side)
    return outs[0], outs[1], tuple(outs[2:])


def _gating_weights(ws, bs, seq):
    n = min(seq, GMLP_CHUNK)
    rep = GMLP_CHUNK // n
    w = ws[:, :n, :n]
    if rep > 1:
        eye = jnp.eye(rep, dtype=ws.dtype)
        w = jnp.einsum('ab,gts->gatbs', eye, w).reshape(GMLP_GROUPS, GMLP_CHUNK, GMLP_CHUNK)
    return w, jnp.tile(bs[:, :n], (1, rep)).T


def _trunk(x, batch, seq, mem_k, mem_v, cache, p, layer_idx, alpha, want_vrows):
    lam_init = 0.8 - 0.6 * math.exp(-0.3 * layer_idx)
    ws, bs_t = _gating_weights(p['gmlp_ws'], p['gmlp_bs'], seq)
    lam_vecs = (p['lambda_q1'], p['lambda_k1'], p['lambda_q2'], p['lambda_k2'])

    if 'ffn2_w_gu' in p:
        x1, _ = _ffn(x, p['ffn1_w_gu'], p['ffn1_w_down'], p['ln1_g'], p['ln1_b'], alpha)
    else:
        x1, (p['ffn2_w_gu'], p['ffn2_w_down']) = _ffn(x, p['ffn1_w_gu'], p['ffn1_w_down'], p['ln1_g'], p['ln1_b'],
                                                      alpha, side=(p['ffn2_w_gu_f32'], p['ffn2_w_down_f32']))
    keys_on_lanes = cache is None
    outs = _inproj(x1, p['w_in_rest'], p['w_in_keys_t'] if keys_on_lanes else p['w_in_keys'], p['gmlp_ln_g'],
                   p['gmlp_ln_b'], ws, bs_t, batch, seq, keys_on_lanes, want_vrows)
    q, k, v, gated = outs[:4]
    if cache is None:
        a = _attn_prompt(q, k, v, lam_vecs, p['subln_g'], batch, seq, lam_init)
    else:
        a = _attn_sample(q, k, v, cache[0], cache[1], lam_vecs, p['subln_g'], batch, seq, lam_init)
    x2, q2 = _outproj(a, gated, x1, p['w_out'], p['ln2_g'], p['ln2_b'], p['cross_wq'], alpha)
    n_mem = mem_k.size // (batch * x.shape[1])
    o = _cross(q2, mem_k, mem_v, batch, seq, n_mem)
    y, _ = _ffn(x2, p['ffn2_w_gu'], p['ffn2_w_down'], p['ln4_g'], p['ln4_b'], alpha,
                proj=(o, p['cross_wo'], p['ln3_g'], p['ln3_b']))
    return y, k, v, (outs[4] if want_vrows else None)


_MATRICES = ('w_in', 'w_out', 'cross_wq', 'cross_wo')
_SIDE_CAST_MATRICES = ('ffn1_w_gu', 'ffn1_w_down', 'ffn2_w_gu', 'ffn2_w_down')
_ROW_VECTORS = ('ln1_g', 'ln1_b', 'lambda_q1', 'lambda_k1', 'lambda_q2', 'lambda_k2', 'subln_g', 'gmlp_ln_g',
                'gmlp_ln_b', 'ln2_g', 'ln2_b', 'ln3_g', 'ln3_b', 'ln4_g', 'ln4_b')


def kernel(x_prompt, x_sample, cache_k, cache_v, cache_mem_k, cache_mem_v, mem_prompt, ffn1_w_gu, ffn1_w_down, ln1_g, ln1_b, w_in, lambda_q1, lambda_k1, lambda_q2, lambda_k2, subln_g, gmlp_ln_g, gmlp_ln_b, gmlp_ws, gmlp_bs, w_out, ln2_g, ln2_b, cross_wq, cross_wk, cross_wv, cross_wo, ln3_g, ln3_b, ffn2_w_gu, ffn2_w_down, ln4_g, ln4_b):
    weights = dict(ffn1_w_gu=ffn1_w_gu, ffn1_w_down=ffn1_w_down, ln1_g=ln1_g, ln1_b=ln1_b, w_in=w_in,
                   lambda_q1=lambda_q1, lambda_k1=lambda_k1, lambda_q2=lambda_q2, lambda_k2=lambda_k2,
                   subln_g=subln_g, gmlp_ln_g=gmlp_ln_g, gmlp_ln_b=gmlp_ln_b, gmlp_ws=gmlp_ws, gmlp_bs=gmlp_bs,
                   w_out=w_out, ln2_g=ln2_g, ln2_b=ln2_b, cross_wq=cross_wq, cross_wk=cross_wk, cross_wv=cross_wv,
                   cross_wo=cross_wo, ln3_g=ln3_g, ln3_b=ln3_b, ffn2_w_gu=ffn2_w_gu, ffn2_w_down=ffn2_w_down,
                   ln4_g=ln4_g, ln4_b=ln4_b)
    depth = w_in.shape[0]
    alpha = (2 * depth) ** 0.25
    bp, tp, d = x_prompt.shape
    bs_, ts, _ = x_sample.shape
    n_mem = mem_prompt.shape[1]
    past = cache_k.shape[2]
    mem_hd = d // MEM_HEADS

    xp =x_prompt.reshape(bp * tp, d)
    xs = x_sample.reshape(bs_ * ts, d)
    mem2d = mem_prompt.reshape(bp * n_mem, d)
    per_layer = [[] for _ in range(7)]
    for l in range(depth):
        p = {}
        for name, w in weights.items():
            w = w[l]
            if name in _MATRICES:
                w = w.astype(BF16)
            elif name in _SIDE_CAST_MATRICES:
                name = name + '_f32'
            elif name in _ROW_VECTORS:
                w = w.reshape(1, -1)
            p[name] = w
        w_in_l = p.pop('w_in')
        p['w_in_rest'] = jnp.concatenate([w_in_l[:, :DIFF_WIDTH], w_in_l[:, 2 * DIFF_WIDTH:]], axis=1)
        p['w_in_keys'] = w_in_l[:, DIFF_WIDTH:2 * DIFF_WIDTH]
        p['w_in_keys_t'] = p['w_in_keys'].T
        mem_k, mem_v, (p['ffn1_w_gu'], p['ffn1_w_down']) = _memkv(
            mem2d, p['cross_wk'], p['cross_wv'], side=(p['ffn1_w_gu_f32'], p['ffn1_w_down_f32']))
        xp, kp_t, vp, _ = _trunk(xp, bp, tp, mem_k, mem_v, None, p, l, alpha, False)
        cache = (jnp.transpose(cache_k[l], (0, 2, 3, 4, 1)).reshape(bs_, DIFF_WIDTH, past),
                 _to_head_view(cache_v[l]))
        xs, ks, vs, gvs = _trunk(xs, bs_, ts, _to_head_view(cache_mem_k[l]), _to_head_view(cache_mem_v[l]),
                                 cache, p, l, alpha, True)
        kp = jnp.transpose(kp_t.reshape(bp, DIFF_HEADS, 2, HEAD_DIM, tp), (0, 4, 1, 2, 3))
        vals = (kp, _from_head_view(vp, (bp, tp), DIFF_HEADS, HEAD_WIDTH),
                _from_head_view(mem_k, (bp, n_mem), MEM_HEADS, mem_hd),
                _from_head_view(mem_v, (bp, n_mem), MEM_HEADS, mem_hd), ks,
                _from_head_view(vs, (bs_, ts), DIFF_HEADS, HEAD_WIDTH),
                _from_head_view(gvs, (bs_, ts), GMLP_GROUPS, GMLP_GROUP_DIM))
        for acc, val in zip(per_layer, vals):
            acc.append(val)

    kp, vp, mk, mv, ks, vs, gvs = (vals[0][None] if depth == 1 else jnp.stack(vals) for vals in per_layer)
    return (xp.reshape(bp, tp, d), xs.reshape(bs_, ts, d),
            kp.reshape(depth, bp, tp, DIFF_HEADS, 2, HEAD_DIM),
            vp.reshape(depth, bp, tp, DIFF_HEADS, HEAD_WIDTH),
            mk.reshape(depth, bp, n_mem, MEM_HEADS, mem_hd),
            mv.reshape(depth, bp, n_mem, MEM_HEADS, mem_hd),
            ks.reshape(depth, bs_, ts, DIFF_HEADS, 2, HEAD_DIM),
            vs.reshape(depth, bs_, ts, DIFF_HEADS, HEAD_WIDTH),
            gvs.reshape(depth, bs_, ts, GMLP_GROUPS, GMLP_GROUP_DIM))
```

```python
import functools
import math

import jax
import jax.numpy as jnp
from jax import lax
from jax.experimental import pallas as pl
from jax.experimental.pallas import tpu as pltpu

F32 = jnp.float32
BF16 = jnp.bfloat16

CHUNK = 64
DIFF_HEADS = 4
HEAD_DIM = 64
HEAD_WIDTH = 2 * HEAD_DIM
DIFF_WIDTH = DIFF_HEADS * HEAD_WIDTH
GMLP_GROUPS = 4
GMLP_CHUNK = 128
GMLP_GROUP_DIM = 128
GMLP_WIDTH = GMLP_GROUPS * GMLP_GROUP_DIM
MEM_HEADS = 4
LN_EPS = 1e-5
MASKED = -1e30
LANES = 128
BF16_SUBLANES = 16
LOG2E = math.log2(math.e)
Q_SCALE = HEAD_DIM ** -0.5 * LOG2E
POS_SPLIT = 64
BIAS_TERMS = 3

ROW_TILE = 1024
FF_CHUNK = 256
FFN_SUB_TILE = 256
MEMKV_TILE = 512
INPROJ_TILE = 2048
ATTN_Q_TILE = 256
ATTN_HEADS_PER_STEP = 4
CROSS_STEP_ROWS = 128
CROSS_SUB_TILE = 512
VMEM_LIMIT_BYTES = 56 * 1024 * 1024


def _params(*semantics):
    return pltpu.CompilerParams(dimension_semantics=semantics, vmem_limit_bytes=VMEM_LIMIT_BYTES)


def _resident(shape):
    return pl.BlockSpec(shape, lambda *_: (0,) * len(shape), pipeline_mode=pl.Buffered(1))


def _rows(tile, width):
    return pl.BlockSpec((tile, width), lambda i: (i, 0))


def _layer_norm(h, g, b):
    mu = jnp.mean(h, axis=-1, keepdims=True)
    d = h - mu
    var = jnp.mean(d * d, axis=-1, keepdims=True)
    return d * lax.rsqrt(var + LN_EPS) * g + b


def _dot(a, b):
    return jnp.dot(a, b, preferred_element_type=F32)


def _dot_nt(a, b):
    return lax.dot_general(a, b, (((1,), (1,)), ((), ())), preferred_element_type=F32)


def _head_view_rows(rows, heads, width):
    return rows * heads * (width // LANES)


def _head_load(ref, h, heads, rows, width, row0=0):
    tiles = width // LANES
    parts = [ref[pl.ds((row0 * tiles + j) * heads + h, rows, stride=heads * tiles), :] for j in range(tiles)]
    return parts[0] if tiles == 1 else jnp.concatenate(parts, axis=1)


def _head_store(ref, h, heads, value, row0=0):
    rows, width = value.shape
    tiles = width // LANES
    for j in range(tiles):
        start = (row0 * tiles + j) * heads + h
        ref[pl.ds(start, rows, stride=heads * tiles), :] = value[:, j * LANES:(j + 1) * LANES]


def _to_head_view(x):
    *lead, heads, width = x.shape
    tiles = width // LANES
    x = x.reshape(*lead, heads, tiles, LANES)
    x = jnp.swapaxes(x, -3, -2)
    return x.reshape(-1, LANES)


def _from_head_view(x, lead, heads, width):
    tiles = width // LANES
    x = x.reshape(*lead, tiles, heads, LANES)
    x = jnp.swapaxes(x, -3, -2)
    return x.reshape(*lead, heads, width)


def _side_cast_specs(weights, steps):
    in_specs, out_specs, out_shapes = [], [], []
    for w in weights:
        rows, cols = w.shape
        units = rows // BF16_SUBLANES
        assert units * BF16_SUBLANES == rows
        blocks = max(b for b in range(1, min(steps, units) + 1) if units % b == 0)
        spec = pl.BlockSpec((rows // blocks, cols), lambda i, last=blocks - 1: (jnp.minimum(i, last), 0))
        in_specs.append(spec)
        out_specs.append(spec)
        out_shapes.append(jax.ShapeDtypeStruct((rows, cols), BF16))
    return in_specs, out_specs, out_shapes


def _side_cast(src_refs, dst_refs):
    for src, dst in zip(src_refs, dst_refs):
        dst[...] = src[...].astype(BF16)


def _ffn_kernel(*refs, alpha, d_ff, with_proj, n_side):
    n_in = 9 if with_proj else 5
    side_in, side_out = refs[n_in:n_in + n_side], refs[n_in + n_side + 1:n_in + 2 * n_side + 1]
    out_ref, act_ref = refs[n_in + n_side], refs[-1]
    if with_proj:
        o_ref, xin_ref, wo_ref, lpg_ref, lpb_ref, wgu_ref, wd_ref, g_ref, b_ref = refs[:n_in]
    else:
        xin_ref, wgu_ref, wd_ref, g_ref, b_ref = refs[:n_in]
    _side_cast(side_in, side_out)
    tile = xin_ref.shape[0]
    sub = min(FFN_SUB_TILE, tile)
    subs = [slice(r * sub, (r + 1) * sub) for r in range(tile // sub)]
    xs = []
    for rows in subs:
        x = xin_ref[rows, :]
        if with_proj:
            x = _layer_norm(alpha * x + _dot(o_ref[rows, :], wo_ref[...]), lpg_ref[...], lpb_ref[...])
        xs.append(x)
    for rows, x in zip(subs, xs):
        xb = x.astype(BF16)
        for c in range(d_ff // FF_CHUNK):
            lo = c * FF_CHUNK
            gate = _dot(xb, wgu_ref[:, lo:lo + FF_CHUNK])
            up = _dot(xb, wgu_ref[:, d_ff + lo:d_ff + lo + FF_CHUNK])
            act_ref[rows, lo:lo + FF_CHUNK] = (gate * jax.nn.sigmoid(gate) * up).astype(BF16)
    ys = [_dot(act_ref[rows, :], wd_ref[...]) for rows in subs]
    for rows, x, y in zip(subs, xs, ys):
        out_ref[rows, :] = _layer_norm(alpha * x + 0.5 * y, g_ref[...], b_ref[...])


def _ffn(x, w_gu, w_down, ln_g, ln_b, alpha, proj=None, side=()):
    n, d = x.shape
    d_ff = w_down.shape[0]
    tile = min(ROW_TILE, n)
    side_in, side_out, side_shapes = _side_cast_specs(side, n // tile)
    in_specs, args = [], []
    if proj is not None:
        o, wo, pg, pb = proj
        in_specs += [_rows(tile, o.shape[1])]
        args += [o]
    in_specs += [_rows(tile, d)]
    args += [x]
    if proj is not None:
        in_specs += [_resident(wo.shape), _resident((1, d)), _resident((1, d))]
        args += [wo, pg, pb]
    in_specs += [_resident(w_gu.shape), _resident(w_down.shape), _resident((1, d)), _resident((1, d))]
    args += [w_gu, w_down, ln_g, ln_b]
    outs = pl.pallas_call(
        functools.partial(_ffn_kernel, alpha=alpha, d_ff=d_ff, with_proj=proj is not None, n_side=len(side)),
        grid=(n // tile,),
        in_specs=in_specs + side_in,
        out_specs=[_rows(tile, d)] + side_out,
        out_shape=[jax.ShapeDtypeStruct((n, d), F32)] + side_shapes,
        scratch_shapes=[pltpu.VMEM((tile, d_ff), BF16)],
        compiler_params=_params("parallel"),
        name="ffn_proj" if proj is not None else "ffn",
    )(*args, *side)
    return outs[0], tuple(outs[1:])


def _inproj_kernel(x_ref, w_ref, wk_ref, lng_ref, lnb_ref, ws_ref, bst_ref, q_ref, k_ref, v_ref, g_ref, *vrows_ref,
                   keys_on_lanes):
    tile = x_ref.shape[0]
    sub = min(FFN_SUB_TILE, tile)
    row = lax.broadcasted_iota(jnp.int32, (GMLP_CHUNK, GMLP_CHUNK), 0)
    col = lax.broadcasted_iota(jnp.int32, (GMLP_CHUNK, GMLP_CHUNK), 1)
    mix_w = [jnp.where(row >= col, ws_ref[g], 0.0).astype(BF16) for g in range(GMLP_GROUPS)]

    def project(r0):
        xb = x_ref[r0:r0 + sub, :].astype(BF16)
        z = _dot(xb, w_ref[:, :2 * DIFF_WIDTH])
        q_ref[r0:r0 + sub, :] = (z[:, :DIFF_WIDTH] * Q_SCALE).astype(BF16)
        if keys_on_lanes:
            k_ref[:, r0:r0 + sub] = _dot_nt(wk_ref[...], xb)
        else:
            k_ref[r0:r0 + sub, :] = _dot(xb, wk_ref[...])
        for h in range(DIFF_HEADS):
            _head_store(v_ref, h, DIFF_HEADS, z[:, DIFF_WIDTH + h * HEAD_WIDTH:DIFF_WIDTH + (h + 1) * HEAD_WIDTH], r0)

    def gate(r0, zg):
        u = jax.nn.gelu(zg[:, :GMLP_WIDTH])
        vn = _layer_norm(jax.nn.gelu(zg[:, GMLP_WIDTH:]), lng_ref[...], lnb_ref[...])
        if vrows_ref:
            for g in range(GMLP_GROUPS):
                _head_store(vrows_ref[0], g, GMLP_GROUPS, vn[:, g * GMLP_GROUP_DIM:(g + 1) * GMLP_GROUP_DIM], r0)
        vnb = vn.astype(BF16)
        for g in range(GMLP_GROUPS):
            cols = slice(g * GMLP_GROUP_DIM, (g + 1) * GMLP_GROUP_DIM)
            bias = bst_ref[:, g:g + 1]
            for c in range(sub // GMLP_CHUNK):
                rows = slice(c * GMLP_CHUNK, (c + 1) * GMLP_CHUNK)
                mixed = _dot(mix_w[g], vnb[rows, cols]) + bias
                g_ref[r0 + rows.start:r0 + rows.stop, cols] = (u[rows, cols] * mixed).astype(BF16)

    starts = list(range(0, tile, sub))
    zgs = {}
    for i, r0 in enumerate(starts):
        zgs[r0] = _dot(x_ref[r0:r0 + sub, :].astype(BF16), w_ref[:, 2 * DIFF_WIDTH:])
        if i:
            gate(starts[i - 1], zgs.pop(starts[i - 1]))
        project(r0)
    gate(starts[-1], zgs.pop(starts[-1]))


def _head_rows(tile, heads, width):
    return _rows(_head_view_rows(tile, heads, width), LANES)


def _head_shape(rows, heads, width):
    return jax.ShapeDtypeStruct((_head_view_rows(rows, heads, width), LANES), F32)


def _inproj(x, w_rest, w_keys, ln_g, ln_b, ws, bs_t, batch, seq, keys_on_lanes, want_vrows):
    n, d = x.shape
    tile = min(INPROJ_TILE, n)
    if keys_on_lanes:
        per = seq // tile
        k_shape = jax.ShapeDtypeStruct((batch, DIFF_WIDTH, seq), F32)
        k_spec = pl.BlockSpec((None, DIFF_WIDTH, tile), lambda i: (i // per, 0, i % per))
    else:
        k_shape = jax.ShapeDtypeStruct((n, DIFF_WIDTH), F32)
        k_spec = _rows(tile, DIFF_WIDTH)
    out_shape = [jax.ShapeDtypeStruct((n, DIFF_WIDTH), BF16), k_shape,
                 _head_shape(n, DIFF_HEADS, HEAD_WIDTH),
                 jax.ShapeDtypeStruct((n, GMLP_WIDTH), BF16)]
    out_specs = [_rows(tile, DIFF_WIDTH), k_spec, _head_rows(tile, DIFF_HEADS, HEAD_WIDTH), _rows(tile, GMLP_WIDTH)]
    if want_vrows:
        out_shape.append(_head_shape(n, GMLP_GROUPS, GMLP_GROUP_DIM))
        out_specs.append(_head_rows(tile, GMLP_GROUPS, GMLP_GROUP_DIM))
    return pl.pallas_call(
        functools.partial(_inproj_kernel, keys_on_lanes=keys_on_lanes),
        grid=(n // tile,),
        in_specs=[_rows(tile, d), _resident(w_rest.shape), _resident(w_keys.shape), _resident((1, GMLP_WIDTH)),
                  _resident((1, GMLP_WIDTH)), _resident(ws.shape), _resident(bs_t.shape)],
        out_specs=out_specs,
        out_shape=out_shape,
        compiler_params=_params("parallel"),
        name="inproj",
    )(x, w_rest, w_keys, ln_g, ln_b, ws, bs_t)


def _head_scalars(head, lq1_ref, lk1_ref, lq2_ref, lk2_ref, lam_init):
    slope = jnp.exp2(jnp.zeros((1, 1), F32) - 8.0 * (head + 1).astype(F32) / DIFF_HEADS) * LOG2E
    lam = (jnp.exp(jnp.sum(lq1_ref[...] * lk1_ref[...], axis=-1, keepdims=True))
           - jnp.exp(jnp.sum(lq2_ref[...] * lk2_ref[...], axis=-1, keepdims=True)) + lam_init)
    return slope, lam


def _bias_lanes(slope, rows):
    lane = lax.broadcasted_iota(jnp.int32, (1, LANES), 1)
    out = jnp.zeros((1, LANES), F32)
    rest = slope
    for i in range(BIAS_TERMS):
        term = rest.astype(BF16).astype(F32)
        out = jnp.where(lane == 2 * i, term * POS_SPLIT, jnp.where(lane == 2 * i + 1, term, out))
        rest = rest - term
    return jnp.broadcast_to(out, (rows, LANES)).astype(BF16)


def _position_rows(n):
    assert n <= POS_SPLIT * 256
    row = lax.broadcasted_iota(jnp.int32, (LANES, n), 0)
    pos = lax.broadcasted_iota(jnp.int32, (LANES, n), 1)
    val = jnp.where(row % 2 == 0, pos // POS_SPLIT, pos % POS_SPLIT)
    return jnp.where(row < 2 * BIAS_TERMS, val, 0).astype(F32).astype(BF16)


def _stack_maps(q):
    lane = lax.broadcasted_iota(jnp.int32, q.shape, 1)
    zero = jnp.zeros_like(q)
    return jnp.concatenate([jnp.where(lane < HEAD_DIM, q, zero), jnp.where(lane >= HEAD_DIM, q, zero)], axis=0)


def _near_bias(slope, t, base_q, base_k, nk):
    r = lax.broadcasted_iota(jnp.int32, (t, nk), 0) + base_q
    c = lax.broadcasted_iota(jnp.int32, (t, nk), 1) + base_k
    bias = slope * (r - jnp.abs(r - c)).astype(F32)
    return jnp.where(c // CHUNK <= r // CHUNK, bias, MASKED)


def _diff_combine(parts, values, t, lam):
    return _weighted_values(_softmax_weights(parts), values, t, lam)


def _softmax_weights(parts):
    m = functools.reduce(jnp.maximum, [jnp.max(s, axis=-1, keepdims=True) for s in parts])
    return [jnp.exp2(s - m).astype(BF16) for s in parts]


def _weighted_values(weights, values, t, lam):
    res = functools.reduce(jnp.add, [_dot(p, v) for p, v in zip(weights, values)])
    out, denom = res[:, :HEAD_WIDTH], res[:, HEAD_WIDTH:HEAD_WIDTH + 1]
    return out[:t] * (1.0 / denom[:t]) - out[t:] * (lam / denom[t:])


def _with_ones(v):
    lane = lax.broadcasted_iota(jnp.int32, v.shape, 1)
    return jnp.concatenate([v, jnp.where(lane == 0, 1.0, 0.0).astype(v.dtype)], axis=1)


def _sub_norm(o, g, lam_init):
    return o * lax.rsqrt(jnp.mean(o * o, axis=-1, keepdims=True) + LN_EPS) * g * (1.0 - lam_init)


def _attn_prompt_kernel(q_ref, k_ref, v_ref, lq1_ref, lk1_ref, lq2_ref, lk2_ref, sg_ref, o_ref,
                        kb_ref, vb_ref, *, lam_init, group):
    seq = q_ref.shape[0]
    tq = ATTN_Q_TILE
    step = pl.program_id(1)

    @pl.when((pl.program_id(0) == 0) & (step == 0))
    def _():
        for j in range(group):
            kb_ref[j, HEAD_WIDTH:, :] = _position_rows(seq)
            vb_ref[j] = _with_ones(jnp.zeros((seq, HEAD_WIDTH), BF16))

    for j in range(group):
        kb_ref[j, :HEAD_WIDTH, :] = k_ref[j * HEAD_WIDTH:(j + 1) * HEAD_WIDTH, :].astype(BF16)
    for g in range(DIFF_HEADS // group):
        @pl.when(step == g)
        def _():
            for j in range(group):
                vb_ref[j, :, :HEAD_WIDTH] = _head_load(v_ref, g * group + j, DIFF_HEADS, seq, HEAD_WIDTH).astype(BF16)

    r = lax.broadcasted_iota(jnp.int32, (tq, tq), 0)
    c = lax.broadcasted_iota(jnp.int32, (tq, tq), 1)
    allowed = c // CHUNK <= r // CHUNK
    lead = -2.0 * jnp.maximum(c - r, 0).astype(F32)
    q_bias, diag = [], []
    for j in range(group):
        slope, lam = _head_scalars(step * group + j, lq1_ref, lk1_ref, lq2_ref, lk2_ref, lam_init)
        q_bias.append(_bias_lanes(slope, 2 * tq))
        d = jnp.where(allowed, slope * lead, MASKED)
        diag.append(jnp.concatenate([d, d], axis=0))

    def scores(j, lo):
        cols = slice(j * HEAD_WIDTH, (j + 1) * HEAD_WIDTH)
        qs = jnp.concatenate([_stack_maps(q_ref[lo:lo + tq, cols]), q_bias[j]], axis=1)
        parts = [_dot(qs, kb_ref[j, :, lo:lo + tq]) + diag[j]]
        if lo:
            parts.append(_dot(qs, kb_ref[j, :, 0:lo]))
        return parts

    def finish(j, lo, weights):
        values = [vb_ref[j, lo:lo + tq, :]] + ([vb_ref[j, 0:lo, :]] if lo else [])
        out = _weighted_values(weights, values, tq, lam)
        o_ref[lo:lo + tq, j * HEAD_WIDTH:(j + 1) * HEAD_WIDTH] = _sub_norm(out, sg_ref[...], lam_init).astype(BF16)

    chains = [(j, lo) for lo in range(0, seq, tq) for j in range(group)]
    parts = scores(*chains[0])
    for i, chain in enumerate(chains):
        next_parts = scores(*chains[i + 1]) if i + 1 < len(chains) else None
        finish(*chain, _softmax_weights(parts))
        parts = next_parts


def _attn_prompt(q, k_t, v, lam_vecs, subln_g, batch, seq, lam_init):
    group = ATTN_HEADS_PER_STEP
    blk = pl.BlockSpec((seq, group * HEAD_WIDTH), lambda b, g: (b, g))
    vec = pl.BlockSpec((1, HEAD_DIM), lambda b, g: (0, 0))
    return pl.pallas_call(
        functools.partial(_attn_prompt_kernel, lam_init=lam_init, group=group),
        grid=(batch, DIFF_HEADS // group),
        in_specs=[blk, pl.BlockSpec((None, group * HEAD_WIDTH, seq), lambda b, g: (b, g, 0)),
                  pl.BlockSpec((_head_view_rows(seq, DIFF_HEADS, HEAD_WIDTH), LANES), lambda b, g: (b, 0)),
                  vec, vec, vec, vec, pl.BlockSpec((1, HEAD_WIDTH), lambda b, g: (0, 0))],
        out_specs=blk,
        out_shape=jax.ShapeDtypeStruct((batch * seq, DIFF_WIDTH), BF16),
        scratch_shapes=[pltpu.VMEM((group, HEAD_WIDTH + LANES, seq), BF16),
                        pltpu.VMEM((group, seq, 2 * HEAD_WIDTH), BF16)],
        compiler_params=_params("arbitrary", "arbitrary"),
        name="attn_prompt",
    )(q, k_t, v, *lam_vecs, subln_g)


def _attn_sample_kernel(q_ref, kn_ref, vn_ref, kc_ref, vc_ref, lq1_ref, lk1_ref, lq2_ref, lk2_ref, sg_ref, o_ref,
                        *, lam_init):
    t = q_ref.shape[0]
    past = kc_ref.shape[1]
    far_pos = lax.broadcasted_iota(jnp.int32, (1, past), 1).astype(F32)
    for h in range(DIFF_HEADS):
        cols = slice(h * HEAD_WIDTH, (h + 1) * HEAD_WIDTH)
        slope, lam = _head_scalars(jnp.int32(h), lq1_ref, lk1_ref, lq2_ref, lk2_ref, lam_init)
        qs = _stack_maps(q_ref[:, cols])
        near = _near_bias(slope, t, past, past, t)
        parts = [_dot_nt(qs, kn_ref[:, cols].astype(BF16)) + jnp.concatenate([near, near], axis=0),
                 _dot(qs, kc_ref[cols, :].astype(BF16)) + slope * far_pos]
        values = [_with_ones(_head_load(vn_ref, h, DIFF_HEADS, t, HEAD_WIDTH).astype(BF16)),
                  _with_ones(_head_load(vc_ref, h, DIFF_HEADS, past, HEAD_WIDTH).astype(BF16))]
        out = _diff_combine(parts, values, t, lam)
        o_ref[:, cols] = _sub_norm(out, sg_ref[...], lam_init).astype(BF16)


def _attn_sample(q, k_new, v_new, k_cache_t, v_cache, lam_vecs, subln_g, batch, seq, lam_init):
    past = k_cache_t.shape[2]
    new = pl.BlockSpec((seq, DIFF_WIDTH), lambda b: (b, 0))
    vec = pl.BlockSpec((1, HEAD_DIM), lambda b: (0, 0))
    return pl.pallas_call(
        functools.partial(_attn_sample_kernel, lam_init=lam_init),
        grid=(batch,),
        in_specs=[new, new, _head_rows(seq, DIFF_HEADS, HEAD_WIDTH),
                  pl.BlockSpec((None, DIFF_WIDTH, past), lambda b: (b, 0, 0)),
                  _head_rows(past, DIFF_HEADS, HEAD_WIDTH),
                  vec, vec, vec, vec, pl.BlockSpec((1, HEAD_WIDTH), lambda b: (0, 0))],
        out_specs=new,
        out_shape=jax.ShapeDtypeStruct((batch * seq, DIFF_WIDTH), BF16),
        compiler_params=_params("parallel"),
        name="attn_sample",
    )(q, k_new, v_new, k_cache_t, v_cache, *lam_vecs, subln_g)


def _outproj_kernel(a_ref, g_ref, x_ref, wo_ref, lg_ref, lb_ref, wq_ref, x2_ref, q2_ref, *, alpha):
    tile = x_ref.shape[0]
    sub = min(FFN_SUB_TILE, tile)
    subs = [slice(r * sub, (r + 1) * sub) for r in range(tile // sub)]
    mixes = [_dot(jnp.concatenate([a_ref[rows, :], g_ref[rows, :]], axis=1), wo_ref[...]) for rows in subs]
    for rows, mix in zip(subs, mixes):
        x2 = _layer_norm(alpha * x_ref[rows, :] + mix, lg_ref[...], lb_ref[...])
        x2_ref[rows, :] = x2
        q2_ref[rows, :] = _dot(x2.astype(BF16), wq_ref[...]).astype(BF16)


def _outproj(a, g, x, w_out, ln_g, ln_b, wq, alpha):
    n, d = x.shape
    tile = min(ROW_TILE, n)
    return pl.pallas_call(
        functools.partial(_outproj_kernel, alpha=alpha),
        grid=(n // tile,),
        in_specs=[_rows(tile, a.shape[1]), _rows(tile, g.shape[1]), _rows(tile, d), _resident(w_out.shape),
                  _resident((1, d)), _resident((1, d)), _resident(wq.shape)],
        out_specs=[_rows(tile, d), _rows(tile, d)],
        out_shape=[jax.ShapeDtypeStruct((n, d), F32), jax.ShapeDtypeStruct((n, d), BF16)],
        compiler_params=_params("parallel"),
        name="outproj",
    )(a, g, x, w_out, ln_g, ln_b, wq)


def _cross_kernel(q_ref, mk_ref, mv_ref, o_ref, *, n_mem, seq_rows, sub):
    hd = q_ref.shape[1] // MEM_HEADS
    log2_scale = hd ** -0.5 * LOG2E
    chains = []
    for b in range(q_ref.shape[0] // seq_rows):
        for h in range(MEM_HEADS):
            mk = _head_load(mk_ref, h, MEM_HEADS, n_mem, hd, b * n_mem).astype(BF16)
            mv = _head_load(mv_ref, h, MEM_HEADS, n_mem, hd, b * n_mem).astype(BF16)
            for r0 in range(b * seq_rows, (b + 1) * seq_rows, sub):
                chains.append((slice(r0, r0 + sub), slice(h * hd, (h + 1) * hd), mk, mv))

    def scores(chain):
        rows, cols, mk, _ = chain
        return _dot_nt(q_ref[rows, cols], mk)

    s = scores(chains[0])
    for i, (rows, cols, _, mv) in enumerate(chains):
        s_next = scores(chains[i + 1]) if i + 1 < len(chains) else None
        p = jnp.exp2((s - jnp.max(s, axis=-1, keepdims=True)) * log2_scale)
        o = _dot(p.astype(BF16), mv)
        o_ref[rows, cols] = (o * (1.0 / jnp.sum(p, axis=-1, keepdims=True))).astype(BF16)
        s = s_next


def _cross(q2, mem_k, mem_v, batch, seq, n_mem):
    d = q2.shape[1]
    per_step = max(1, CROSS_STEP_ROWS // seq)
    qblk = pl.BlockSpec((per_step * seq, d), lambda i: (i, 0))
    mblk = pl.BlockSpec((per_step * _head_view_rows(n_mem, MEM_HEADS, d // MEM_HEADS), LANES), lambda i: (i, 0))
    return pl.pallas_call(
        functools.partial(_cross_kernel, n_mem=n_mem, seq_rows=seq, sub=min(CROSS_SUB_TILE, seq)),
        grid=(batch // per_step,),
        in_specs=[qblk, mblk, mblk],
        out_specs=qblk,
        out_shape=jax.ShapeDtypeStruct(q2.shape, BF16),
        compiler_params=_params("parallel"),
        name="cross",
    )(q2, mem_k, mem_v)


def _memkv_kernel(*refs, n_side):
    m_ref, wk_ref, wv_ref = refs[:3]
    side_in, (k_ref, v_ref), side_out = refs[3:3 + n_side], refs[3 + n_side:5 + n_side], refs[5 + n_side:]
    _side_cast(side_in, side_out)
    mb = m_ref[...].astype(BF16)
    k = _dot(mb, wk_ref[...].astype(BF16))
    v = _dot(mb, wv_ref[...].astype(BF16))
    hd = k.shape[1] // MEM_HEADS
    for h in range(MEM_HEADS):
        _head_store(k_ref, h, MEM_HEADS, k[:, h * hd:(h + 1) * hd])
        _head_store(v_ref, h, MEM_HEADS, v[:, h * hd:(h + 1) * hd])


def _memkv(mem, wk, wv, side=()):
    n, d = mem.shape
    hd = d // MEM_HEADS
    tile = min(MEMKV_TILE, n)
    side_in, side_out, side_shapes = _side_cast_specs(side, n // tile)
    outs = pl.pallas_call(
        functools.partial(_memkv_kernel, n_side=len(side)),
        grid=(n // tile,),
        in_specs=[_rows(tile, d), _resident(wk.shape), _resident(wv.shape)] + side_in,
        out_specs=[_head_rows(tile, MEM_HEADS, hd)] * 2 + side_out,
        out_shape=[_head_shape(n, MEM_HEADS, hd)] * 2 + side_shapes,
        compiler_params=_params("parallel"),
        name="memkv",
    )(mem, wk, wv, *side)
    return outs[0], outs[1], tuple(outs[2:])


def _gating_weights(ws, bs, seq):
    n = min(seq, GMLP_CHUNK)
    rep = GMLP_CHUNK // n
    w = ws[:, :n, :n]
    if rep > 1:
        eye = jnp.eye(rep, dtype=ws.dtype)
        w = jnp.einsum('ab,gts->gatbs', eye, w).reshape(GMLP_GROUPS, GMLP_CHUNK, GMLP_CHUNK)
    return w, jnp.tile(bs[:, :n], (1, rep)).T


def _trunk(x, batch, seq, mem_k, mem_v, cache, p, layer_idx, alpha, want_vrows):
    lam_init = 0.8 - 0.6 * math.exp(-0.3 * layer_idx)
    ws, bs_t = _gating_weights(p['gmlp_ws'], p['gmlp_bs'], seq)
    lam_vecs = (p['lambda_q1'], p['lambda_k1'], p['lambda_q2'], p['lambda_k2'])

    if 'ffn2_w_gu' in p:
        x1, _ = _ffn(x, p['ffn1_w_gu'], p['ffn1_w_down'], p['ln1_g'], p['ln1_b'], alpha)
    else:
        x1, converted = _ffn(x, p['ffn1_w_gu'], p['ffn1_w_down'], p['ln1_g'], p['ln1_b'], alpha,
                             side=tuple(p[name + '_f32'] for name in _FFN1_SIDE_CASTS))
        p.update(zip(_FFN1_SIDE_CASTS, converted))
    keys_on_lanes = cache is None
    outs = _inproj(x1, p['w_in_rest'], p['w_in_keys_t'] if keys_on_lanes else p['w_in_keys'], p['gmlp_ln_g'],
                   p['gmlp_ln_b'], ws, bs_t, batch, seq, keys_on_lanes, want_vrows)
    q, k, v, gated = outs[:4]
    if cache is None:
        a = _attn_prompt(q, k, v, lam_vecs, p['subln_g'], batch, seq, lam_init)
    else:
        a = _attn_sample(q, k, v, cache[0], cache[1], lam_vecs, p['subln_g'], batch, seq, lam_init)
    x2, q2 = _outproj(a, gated, x1, p['w_out'], p['ln2_g'], p['ln2_b'], p['cross_wq'], alpha)
    n_mem = mem_k.size // (batch * x.shape[1])
    o = _cross(q2, mem_k, mem_v, batch, seq, n_mem)
    y, _ = _ffn(x2, p['ffn2_w_gu'], p['ffn2_w_down'], p['ln4_g'], p['ln4_b'], alpha,
                proj=(o, p['cross_wo'], p['ln3_g'], p['ln3_b']))
    return y, k, v, (outs[4] if want_vrows else None)


_MATRICES = ('w_in',)
_FFN1_SIDE_CASTS = ('ffn2_w_gu', 'ffn2_w_down', 'w_out', 'cross_wq', 'cross_wo')
_SIDE_CAST_MATRICES = ('ffn1_w_gu', 'ffn1_w_down') + _FFN1_SIDE_CASTS
_ROW_VECTORS = ('ln1_g', 'ln1_b', 'lambda_q1', 'lambda_k1', 'lambda_q2', 'lambda_k2', 'subln_g', 'gmlp_ln_g',
                'gmlp_ln_b', 'ln2_g', 'ln2_b', 'ln3_g', 'ln3_b', 'ln4_g', 'ln4_b')


def kernel(x_prompt, x_sample, cache_k, cache_v, cache_mem_k, cache_mem_v, mem_prompt, ffn1_w_gu, ffn1_w_down, ln1_g, ln1_b, w_in, lambda_q1, lambda_k1, lambda_q2, lambda_k2, subln_g, gmlp_ln_g, gmlp_ln_b, gmlp_ws, gmlp_bs, w_out, ln2_g, ln2_b, cross_wq, cross_wk, cross_wv, cross_wo, ln3_g, ln3_b, ffn2_w_gu, ffn2_w_down, ln4_g, ln4_b):
    weights = dict(ffn1_w_gu=ffn1_w_gu, ffn1_w_down=ffn1_w_down, ln1_g=ln1_g, ln1_b=ln1_b, w_in=w_in,
                   lambda_q1=lambda_q1, lambda_k1=lambda_k1, lambda_q2=lambda_q2, lambda_k2=lambda_k2,
                   subln_g=subln_g, gmlp_ln_g=gmlp_ln_g, gmlp_ln_b=gmlp_ln_b, gmlp_ws=gmlp_ws, gmlp_bs=gmlp_bs,
                   w_out=w_out, ln2_g=ln2_g, ln2_b=ln2_b, cross_wq=cross_wq, cross_wk=cross_wk, cross_wv=cross_wv,
                   cross_wo=cross_wo, ln3_g=ln3_g, ln3_b=ln3_b, ffn2_w_gu=ffn2_w_gu, ffn2_w_down=ffn2_w_down,
                   ln4_g=ln4_g, ln4_b=ln4_b)
    depth = w_in.shape[0]
    alpha = (2 * depth) ** 0.25
    bp, tp, d = x_prompt.shape
    bs_, ts, _ = x_sample.shape
    n_mem = mem_prompt.shape[1]
    past = cache_k.shape[2]
    mem_hd = d // MEM_HEADS

    xp =x_prompt.reshape(bp * tp, d)
    xs = x_sample.reshape(bs_ * ts, d)
    mem2d = mem_prompt.reshape(bp * n_mem, d)
    per_layer = [[] for _ in range(7)]
    for l in range(depth):
        p = {}
        for name, w in weights.items():
            w = w[l]
            if name in _MATRICES:
                w = w.astype(BF16)
            elif name in _SIDE_CAST_MATRICES:
                name = name + '_f32'
            elif name in _ROW_VECTORS:
                w = w.reshape(1, -1)
            p[name] = w
        w_in_l = p.pop('w_in')
        p['w_in_rest'] = jnp.concatenate([w_in_l[:, :DIFF_WIDTH], w_in_l[:, 2 * DIFF_WIDTH:]], axis=1)
        p['w_in_keys'] = w_in_l[:, DIFF_WIDTH:2 * DIFF_WIDTH]
        p['w_in_keys_t'] = p['w_in_keys'].T
        mem_k, mem_v, (p['ffn1_w_gu'], p['ffn1_w_down']) = _memkv(
            mem2d, p['cross_wk'], p['cross_wv'], side=(p['ffn1_w_gu_f32'], p['ffn1_w_down_f32']))
        xp, kp_t, vp, _ = _trunk(xp, bp, tp, mem_k, mem_v, None, p, l, alpha, False)
        cache = (jnp.transpose(cache_k[l], (0, 2, 3, 4, 1)).reshape(bs_, DIFF_WIDTH, past),
                 _to_head_view(cache_v[l]))
        xs, ks, vs, gvs = _trunk(xs, bs_, ts, _to_head_view(cache_mem_k[l]), _to_head_view(cache_mem_v[l]),
                                 cache, p, l, alpha, True)
        kp = jnp.transpose(kp_t.reshape(bp, DIFF_HEADS, 2, HEAD_DIM, tp), (0, 4, 1, 2, 3))
        vals = (kp, _from_head_view(vp, (bp, tp), DIFF_HEADS, HEAD_WIDTH),
                _from_head_view(mem_k, (bp, n_mem), MEM_HEADS, mem_hd),
                _from_head_view(mem_v, (bp, n_mem), MEM_HEADS, mem_hd), ks,
                _from_head_view(vs, (bs_, ts), DIFF_HEADS, HEAD_WIDTH),
                _from_head_view(gvs, (bs_, ts), GMLP_GROUPS, GMLP_GROUP_DIM))
        for acc, val in zip(per_layer, vals):
            acc.append(val)

    kp, vp, mk, mv, ks, vs, gvs = (vals[0][None] if depth == 1 else jnp.stack(vals) for vals in per_layer)
    return (xp.reshape(bp, tp, d), xs.reshape(bs_, ts, d),
            kp.reshape(depth, bp, tp, DIFF_HEADS, 2, HEAD_DIM),
            vp.reshape(depth, bp, tp, DIFF_HEADS, HEAD_WIDTH),
            mk.reshape(depth, bp, n_mem, MEM_HEADS, mem_hd),
            mv.reshape(depth, bp, n_mem, MEM_HEADS, mem_hd),
            ks.reshape(depth, bs_, ts, DIFF_HEADS, 2, HEAD_DIM),
            vs.reshape(depth, bs_, ts, DIFF_HEADS, HEAD_WIDTH),
            gvs.reshape(depth, bs_, ts, GMLP_GROUPS, GMLP_GROUP_DIM))
```

```python
import functools
import math

import jax
import jax.numpy as jnp
from jax import lax
from jax.experimental import pallas as pl
from jax.experimental.pallas import tpu as pltpu

F32 = jnp.float32
BF16 = jnp.bfloat16

CHUNK = 64
DIFF_HEADS = 4
HEAD_DIM = 64
HEAD_WIDTH = 2 * HEAD_DIM
DIFF_WIDTH = DIFF_HEADS * HEAD_WIDTH
GMLP_GROUPS = 4
GMLP_CHUNK = 128
GMLP_GROUP_DIM = 128
GMLP_WIDTH = GMLP_GROUPS * GMLP_GROUP_DIM
MEM_HEADS = 4
LN_EPS = 1e-5
MASKED = -1e30
LANES = 128
BF16_SUBLANES = 16
LOG2E = math.log2(math.e)
Q_SCALE = HEAD_DIM ** -0.5 * LOG2E
POS_SPLIT = 64
BIAS_TERMS = 3

ROW_TILE = 1024
FF_CHUNK = 256
FFN_SUB_TILE = 256
MEMKV_TILE = 512
ATTN_Q_TILE = 256
ATTN_HEADS_PER_STEP = 4
CROSS_STEP_ROWS = 128
CROSS_SUB_TILE = 512
VMEM_LIMIT_BYTES = 56 * 1024 * 1024


def _params(*semantics):
    return pltpu.CompilerParams(dimension_semantics=semantics, vmem_limit_bytes=VMEM_LIMIT_BYTES)


def _resident(shape):
    return pl.BlockSpec(shape, lambda *_: (0,) * len(shape), pipeline_mode=pl.Buffered(1))


def _rows(tile, width):
    return pl.BlockSpec((tile, width), lambda i: (i, 0))


def _layer_norm(h, g, b):
    mu = jnp.mean(h, axis=-1, keepdims=True)
    d = h - mu
    var = jnp.mean(d * d, axis=-1, keepdims=True)
    return d * lax.rsqrt(var + LN_EPS) * g + b


def _dot(a, b):
    return jnp.dot(a, b, preferred_element_type=F32)


def _dot_nt(a, b):
    return lax.dot_general(a, b, (((1,), (1,)), ((), ())), preferred_element_type=F32)


def _head_view_rows(rows, heads, width):
    return rows * heads * (width // LANES)


def _head_load(ref, h, heads, rows, width, row0=0):
    tiles = width // LANES
    parts = [ref[pl.ds((row0 * tiles + j) * heads + h, rows, stride=heads * tiles), :] for j in range(tiles)]
    return parts[0] if tiles == 1 else jnp.concatenate(parts, axis=1)


def _head_store(ref, h, heads, value, row0=0):
    rows, width = value.shape
    tiles = width // LANES
    for j in range(tiles):
        start = (row0 * tiles + j) * heads + h
        ref[pl.ds(start, rows, stride=heads * tiles), :] = value[:, j * LANES:(j + 1) * LANES]


def _to_head_view(x):
    *lead, heads, width = x.shape
    tiles = width // LANES
    x = x.reshape(*lead, heads, tiles, LANES)
    x = jnp.swapaxes(x, -3, -2)
    return x.reshape(-1, LANES)


def _from_head_view(x, lead, heads, width):
    tiles = width // LANES
    x = x.reshape(*lead, tiles, heads, LANES)
    x = jnp.swapaxes(x, -3, -2)
    return x.reshape(*lead, heads, width)


def _side_cast_specs(weights, steps):
    in_specs, out_specs, out_shapes = [], [], []
    for w in weights:
        rows, cols = w.shape
        units = rows // BF16_SUBLANES
        assert units * BF16_SUBLANES == rows
        blocks = max(b for b in range(1, min(steps, units) + 1) if units % b == 0)
        spec = pl.BlockSpec((rows // blocks, cols), lambda i, last=blocks - 1: (jnp.minimum(i, last), 0))
        in_specs.append(spec)
        out_specs.append(spec)
        out_shapes.append(jax.ShapeDtypeStruct((rows, cols), BF16))
    return in_specs, out_specs, out_shapes


def _side_cast(src_refs, dst_refs):
    for src, dst in zip(src_refs, dst_refs):
        dst[...] = src[...].astype(BF16)


def _ffn_kernel(*refs, alpha, d_ff, with_proj, n_side):
    n_in = 9 if with_proj else 5
    side_in, side_out = refs[n_in:n_in + n_side], refs[n_in + n_side + 1:n_in + 2 * n_side + 1]
    out_ref, act_ref = refs[n_in + n_side], refs[-1]
    if with_proj:
        o_ref, xin_ref, wo_ref, lpg_ref, lpb_ref, wgu_ref, wd_ref, g_ref, b_ref = refs[:n_in]
    else:
        xin_ref, wgu_ref, wd_ref, g_ref, b_ref = refs[:n_in]
    _side_cast(side_in, side_out)
    tile = xin_ref.shape[0]
    sub = min(FFN_SUB_TILE, tile)
    subs = [slice(r * sub, (r + 1) * sub) for r in range(tile // sub)]
    xs = []
    for rows in subs:
        x = xin_ref[rows, :]
        if with_proj:
            x = _layer_norm(alpha * x + _dot(o_ref[rows, :], wo_ref[...]), lpg_ref[...], lpb_ref[...])
        xs.append(x)
    for rows, x in zip(subs, xs):
        xb = x.astype(BF16)
        for c in range(d_ff // FF_CHUNK):
            lo = c * FF_CHUNK
            gate = _dot(xb, wgu_ref[:, lo:lo + FF_CHUNK])
            up = _dot(xb, wgu_ref[:, d_ff + lo:d_ff + lo + FF_CHUNK])
            act_ref[rows, lo:lo + FF_CHUNK] = (gate * jax.nn.sigmoid(gate) * up).astype(BF16)
    ys = [_dot(act_ref[rows, :], wd_ref[...]) for rows in subs]
    for rows, x, y in zip(subs, xs, ys):
        out_ref[rows, :] = _layer_norm(alpha * x + 0.5 * y, g_ref[...], b_ref[...])


def _ffn(x, w_gu, w_down, ln_g, ln_b, alpha, proj=None, side=()):
    n, d = x.shape
    d_ff = w_down.shape[0]
    tile = min(ROW_TILE, n)
    side_in, side_out, side_shapes = _side_cast_specs(side, n // tile)
    in_specs, args = [], []
    if proj is not None:
        o, wo, pg, pb = proj
        in_specs += [_rows(tile, o.shape[1])]
        args += [o]
    in_specs += [_rows(tile, d)]
    args += [x]
    if proj is not None:
        in_specs += [_resident(wo.shape), _resident((1, d)), _resident((1, d))]
        args += [wo, pg, pb]
    in_specs += [_resident(w_gu.shape), _resident(w_down.shape), _resident((1, d)), _resident((1, d))]
    args += [w_gu, w_down, ln_g, ln_b]
    outs = pl.pallas_call(
        functools.partial(_ffn_kernel, alpha=alpha, d_ff=d_ff, with_proj=proj is not None, n_side=len(side)),
        grid=(n // tile,),
        in_specs=in_specs + side_in,
        out_specs=[_rows(tile, d)] + side_out,
        out_shape=[jax.ShapeDtypeStruct((n, d), F32)] + side_shapes,
        scratch_shapes=[pltpu.VMEM((tile, d_ff), BF16)],
        compiler_params=_params("parallel"),
        name="ffn_proj" if proj is not None else "ffn",
    )(*args, *side)
    return outs[0], tuple(outs[1:])


def _inproj_kernel(x_ref, w_ref, wk_ref, lng_ref, lnb_ref, ws_ref, bst_ref, q_ref, k_ref, v_ref, g_ref, *vrows_ref,
                   keys_on_lanes):
    tile = x_ref.shape[0]
    sub = min(FFN_SUB_TILE, tile)
    row = lax.broadcasted_iota(jnp.int32, (GMLP_CHUNK, GMLP_CHUNK), 0)
    col = lax.broadcasted_iota(jnp.int32, (GMLP_CHUNK, GMLP_CHUNK), 1)
    mix_w = [jnp.where(row >= col, ws_ref[g], 0.0).astype(BF16) for g in range(GMLP_GROUPS)]

    def project(r0):
        xb = x_ref[r0:r0 + sub, :].astype(BF16)
        z = _dot(xb, w_ref[:, :2 * DIFF_WIDTH])
        q_ref[r0:r0 + sub, :] = (z[:, :DIFF_WIDTH] * Q_SCALE).astype(BF16)
        if keys_on_lanes:
            k_ref[:, r0:r0 + sub] = _dot_nt(wk_ref[...], xb)
        else:
            k_ref[r0:r0 + sub, :] = _dot(xb, wk_ref[...])
        for h in range(DIFF_HEADS):
            _head_store(v_ref, h, DIFF_HEADS, z[:, DIFF_WIDTH + h * HEAD_WIDTH:DIFF_WIDTH + (h + 1) * HEAD_WIDTH], r0)

    def gate(r0, zg):
        u = jax.nn.gelu(zg[:, :GMLP_WIDTH])
        vn = _layer_norm(jax.nn.gelu(zg[:, GMLP_WIDTH:]), lng_ref[...], lnb_ref[...])
        if vrows_ref:
            for g in range(GMLP_GROUPS):
                _head_store(vrows_ref[0], g, GMLP_GROUPS, vn[:, g * GMLP_GROUP_DIM:(g + 1) * GMLP_GROUP_DIM], r0)
        vnb = vn.astype(BF16)
        for g in range(GMLP_GROUPS):
            cols = slice(g * GMLP_GROUP_DIM, (g + 1) * GMLP_GROUP_DIM)
            bias = bst_ref[:, g:g + 1]
            for c in range(sub // GMLP_CHUNK):
                rows = slice(c * GMLP_CHUNK, (c + 1) * GMLP_CHUNK)
                mixed = _dot(mix_w[g], vnb[rows, cols]) + bias
                g_ref[r0 + rows.start:r0 + rows.stop, cols] = (u[rows, cols] * mixed).astype(BF16)

    starts = list(range(0, tile, sub))
    zgs = {}
    for i, r0 in enumerate(starts):
        zgs[r0] = _dot(x_ref[r0:r0 + sub, :].astype(BF16), w_ref[:, 2 * DIFF_WIDTH:])
        if i:
            gate(starts[i - 1], zgs.pop(starts[i - 1]))
        project(r0)
    gate(starts[-1], zgs.pop(starts[-1]))


def _head_rows(tile, heads, width):
    return _rows(_head_view_rows(tile, heads, width), LANES)


def _head_shape(rows, heads, width):
    return jax.ShapeDtypeStruct((_head_view_rows(rows, heads, width), LANES), F32)


def _inproj(x, w_rest, w_keys, ln_g, ln_b, ws, bs_t, batch, seq, keys_on_lanes, want_vrows):
    n, d = x.shape
    tile = min(ROW_TILE, n)
    if keys_on_lanes:
        per = seq // tile
        k_shape = jax.ShapeDtypeStruct((batch, DIFF_WIDTH, seq), F32)
        k_spec = pl.BlockSpec((None, DIFF_WIDTH, tile), lambda i: (i // per, 0, i % per))
    else:
        k_shape = jax.ShapeDtypeStruct((n, DIFF_WIDTH), F32)
        k_spec = _rows(tile, DIFF_WIDTH)
    out_shape = [jax.ShapeDtypeStruct((n, DIFF_WIDTH), BF16), k_shape,
                 _head_shape(n, DIFF_HEADS, HEAD_WIDTH),
                 jax.ShapeDtypeStruct((n, GMLP_WIDTH), BF16)]
    out_specs = [_rows(tile, DIFF_WIDTH), k_spec, _head_rows(tile, DIFF_HEADS, HEAD_WIDTH), _rows(tile, GMLP_WIDTH)]
    if want_vrows:
        out_shape.append(_head_shape(n, GMLP_GROUPS, GMLP_GROUP_DIM))
        out_specs.append(_head_rows(tile, GMLP_GROUPS, GMLP_GROUP_DIM))
    return pl.pallas_call(
        functools.partial(_inproj_kernel, keys_on_lanes=keys_on_lanes),
        grid=(n // tile,),
        in_specs=[_rows(tile, d), _resident(w_rest.shape), _resident(w_keys.shape), _resident((1, GMLP_WIDTH)),
                  _resident((1, GMLP_WIDTH)), _resident(ws.shape), _resident(bs_t.shape)],
        out_specs=out_specs,
        out_shape=out_shape,
        compiler_params=_params("parallel"),
        name="inproj",
    )(x, w_rest, w_keys, ln_g, ln_b, ws, bs_t)


def _head_scalars(head, lq1_ref, lk1_ref, lq2_ref, lk2_ref, lam_init):
    slope = jnp.exp2(jnp.zeros((1, 1), F32) - 8.0 * (head + 1).astype(F32) / DIFF_HEADS) * LOG2E
    lam = (jnp.exp(jnp.sum(lq1_ref[...] * lk1_ref[...], axis=-1, keepdims=True))
           - jnp.exp(jnp.sum(lq2_ref[...] * lk2_ref[...], axis=-1, keepdims=True)) + lam_init)
    return slope, lam


def _bias_lanes(slope, rows):
    lane = lax.broadcasted_iota(jnp.int32, (1, LANES), 1)
    out = jnp.zeros((1, LANES), F32)
    rest = slope
    for i in range(BIAS_TERMS):
        term = rest.astype(BF16).astype(F32)
        out = jnp.where(lane == 2 * i, term * POS_SPLIT, jnp.where(lane == 2 * i + 1, term, out))
        rest = rest - term
    return jnp.broadcast_to(out, (rows, LANES)).astype(BF16)


def _position_rows(n):
    assert n <= POS_SPLIT * 256
    row = lax.broadcasted_iota(jnp.int32, (LANES, n), 0)
    pos = lax.broadcasted_iota(jnp.int32, (LANES, n), 1)
    val = jnp.where(row % 2 == 0, pos // POS_SPLIT, pos % POS_SPLIT)
    return jnp.where(row < 2 * BIAS_TERMS, val, 0).astype(F32).astype(BF16)


def _stack_maps(q):
    lane = lax.broadcasted_iota(jnp.int32, q.shape, 1)
    zero = jnp.zeros_like(q)
    return jnp.concatenate([jnp.where(lane < HEAD_DIM, q, zero), jnp.where(lane >= HEAD_DIM, q, zero)], axis=0)


def _near_bias(slope, t, base_q, base_k, nk):
    r = lax.broadcasted_iota(jnp.int32, (t, nk), 0) + base_q
    c = lax.broadcasted_iota(jnp.int32, (t, nk), 1) + base_k
    bias = slope * (r - jnp.abs(r - c)).astype(F32)
    return jnp.where(c // CHUNK <= r // CHUNK, bias, MASKED)


def _diff_combine(parts, values, t, lam):
    return _weighted_values(_softmax_weights(parts), values, t, lam)


def _softmax_weights(parts):
    m = functools.reduce(jnp.maximum, [jnp.max(s, axis=-1, keepdims=True) for s in parts])
    return [jnp.exp2(s - m).astype(BF16) for s in parts]


def _weighted_values(weights, values, t, lam):
    res = functools.reduce(jnp.add, [_dot(p, v) for p, v in zip(weights, values)])
    out, denom = res[:, :HEAD_WIDTH], res[:, HEAD_WIDTH:HEAD_WIDTH + 1]
    return out[:t] * (1.0 / denom[:t]) - out[t:] * (lam / denom[t:])


def _with_ones(v):
    lane = lax.broadcasted_iota(jnp.int32, v.shape, 1)
    return jnp.concatenate([v, jnp.where(lane == 0, 1.0, 0.0).astype(v.dtype)], axis=1)


def _sub_norm(o, g, lam_init):
    return o * lax.rsqrt(jnp.mean(o * o, axis=-1, keepdims=True) + LN_EPS) * g * (1.0 - lam_init)


def _attn_prompt_kernel(q_ref, k_ref, v_ref, lq1_ref, lk1_ref, lq2_ref, lk2_ref, sg_ref, o_ref,
                        kb_ref, vb_ref, *, lam_init, group):
    seq = q_ref.shape[0]
    tq = ATTN_Q_TILE
    step = pl.program_id(1)

    @pl.when((pl.program_id(0) == 0) & (step == 0))
    def _():
        for j in range(group):
            kb_ref[j, HEAD_WIDTH:, :] = _position_rows(seq)
            vb_ref[j] = _with_ones(jnp.zeros((seq, HEAD_WIDTH), BF16))

    def prepare(j):
        kb_ref[j, :HEAD_WIDTH, :] = k_ref[j * HEAD_WIDTH:(j + 1) * HEAD_WIDTH, :].astype(BF16)
        if group == DIFF_HEADS:
            vb_ref[j, :, :HEAD_WIDTH] = _head_load(v_ref, j, DIFF_HEADS, seq, HEAD_WIDTH).astype(BF16)
        else:
            for g in range(DIFF_HEADS // group):
                @pl.when(step == g)
                def _():
                    vb_ref[j, :, :HEAD_WIDTH] = _head_load(v_ref, g * group + j, DIFF_HEADS, seq,
                                                           HEAD_WIDTH).astype(BF16)

    r = lax.broadcasted_iota(jnp.int32, (tq, tq), 0)
    c = lax.broadcasted_iota(jnp.int32, (tq, tq), 1)
    allowed = c // CHUNK <= r // CHUNK
    lead = -2.0 * jnp.maximum(c - r, 0).astype(F32)
    q_bias, diag = [], []
    for j in range(group):
        slope, lam = _head_scalars(step * group + j, lq1_ref, lk1_ref, lq2_ref, lk2_ref, lam_init)
        q_bias.append(_bias_lanes(slope, 2 * tq))
        d = jnp.where(allowed, slope * lead, MASKED)
        diag.append(jnp.concatenate([d, d], axis=0))

    def scores(j, lo):
        cols = slice(j * HEAD_WIDTH, (j + 1) * HEAD_WIDTH)
        qs = jnp.concatenate([_stack_maps(q_ref[lo:lo + tq, cols]), q_bias[j]], axis=1)
        parts = [_dot(qs, kb_ref[j, :, lo:lo + tq]) + diag[j]]
        if lo:
            parts.append(_dot(qs, kb_ref[j, :, 0:lo]))
        return parts

    def finish(j, lo, weights):
        values = [vb_ref[j, lo:lo + tq, :]] + ([vb_ref[j, 0:lo, :]] if lo else [])
        out = _weighted_values(weights, values, tq, lam)
        o_ref[lo:lo + tq, j * HEAD_WIDTH:(j + 1) * HEAD_WIDTH] = _sub_norm(out, sg_ref[...], lam_init).astype(BF16)

    chains = [(j, lo) for lo in range(0, seq, tq) for j in range(group)]

    def start(chain):
        if chain[1] == 0:
            prepare(chain[0])
        return scores(*chain)

    parts = start(chains[0])
    for i, chain in enumerate(chains):
        next_parts = start(chains[i + 1]) if i + 1 < len(chains) else None
        finish(*chain, _softmax_weights(parts))
        parts = next_parts


def _attn_prompt(q, k_t, v, lam_vecs, subln_g, batch, seq, lam_init):
    group = ATTN_HEADS_PER_STEP
    blk = pl.BlockSpec((seq, group * HEAD_WIDTH), lambda b, g: (b, g))
    vec = pl.BlockSpec((1, HEAD_DIM), lambda b, g: (0, 0))
    return pl.pallas_call(
        functools.partial(_attn_prompt_kernel, lam_init=lam_init, group=group),
        grid=(batch, DIFF_HEADS // group),
        in_specs=[blk, pl.BlockSpec((None, group * HEAD_WIDTH, seq), lambda b, g: (b, g, 0)),
                  pl.BlockSpec((_head_view_rows(seq, DIFF_HEADS, HEAD_WIDTH), LANES), lambda b, g: (b, 0)),
                  vec, vec, vec, vec, pl.BlockSpec((1, HEAD_WIDTH), lambda b, g: (0, 0))],
        out_specs=blk,
        out_shape=jax.ShapeDtypeStruct((batch * seq, DIFF_WIDTH), BF16),
        scratch_shapes=[pltpu.VMEM((group, HEAD_WIDTH + LANES, seq), BF16),
                        pltpu.VMEM((group, seq, 2 * HEAD_WIDTH), BF16)],
        compiler_params=_params("arbitrary", "arbitrary"),
        name="attn_prompt",
    )(q, k_t, v, *lam_vecs, subln_g)


def _attn_sample_kernel(q_ref, kn_ref, vn_ref, kc_ref, vc_ref, lq1_ref, lk1_ref, lq2_ref, lk2_ref, sg_ref, o_ref,
                        *, lam_init):
    t = q_ref.shape[0]
    past = kc_ref.shape[1]
    far_pos = lax.broadcasted_iota(jnp.int32, (1, past), 1).astype(F32)
    for h in range(DIFF_HEADS):
        cols = slice(h * HEAD_WIDTH, (h + 1) * HEAD_WIDTH)
        slope, lam = _head_scalars(jnp.int32(h), lq1_ref, lk1_ref, lq2_ref, lk2_ref, lam_init)
        qs = _stack_maps(q_ref[:, cols])
        near = _near_bias(slope, t, past, past, t)
        parts = [_dot_nt(qs, kn_ref[:, cols].astype(BF16)) + jnp.concatenate([near, near], axis=0),
                 _dot(qs, kc_ref[cols, :].astype(BF16)) + slope * far_pos]
        values = [_with_ones(_head_load(vn_ref, h, DIFF_HEADS, t, HEAD_WIDTH).astype(BF16)),
                  _with_ones(_head_load(vc_ref, h, DIFF_HEADS, past, HEAD_WIDTH).astype(BF16))]
        out = _diff_combine(parts, values, t, lam)
        o_ref[:, cols] = _sub_norm(out, sg_ref[...], lam_init).astype(BF16)


def _attn_sample(q, k_new, v_new, k_cache_t, v_cache, lam_vecs, subln_g, batch, seq, lam_init):
    past = k_cache_t.shape[2]
    new = pl.BlockSpec((seq, DIFF_WIDTH), lambda b: (b, 0))
    vec = pl.BlockSpec((1, HEAD_DIM), lambda b: (0, 0))
    return pl.pallas_call(
        functools.partial(_attn_sample_kernel, lam_init=lam_init),
        grid=(batch,),
        in_specs=[new, new, _head_rows(seq, DIFF_HEADS, HEAD_WIDTH),
                  pl.BlockSpec((None, DIFF_WIDTH, past), lambda b: (b, 0, 0)),
                  _head_rows(past, DIFF_HEADS, HEAD_WIDTH),
                  vec, vec, vec, vec, pl.BlockSpec((1, HEAD_WIDTH), lambda b: (0, 0))],
        out_specs=new,
        out_shape=jax.ShapeDtypeStruct((batch * seq, DIFF_WIDTH), BF16),
        compiler_params=_params("parallel"),
        name="attn_sample",
    )(q, k_new, v_new, k_cache_t, v_cache, *lam_vecs, subln_g)


def _outproj_kernel(a_ref, g_ref, x_ref, wo_ref, lg_ref, lb_ref, wq_ref, x2_ref, q2_ref, *, alpha):
    tile = x_ref.shape[0]
    sub = min(FFN_SUB_TILE, tile)
    subs = [slice(r * sub, (r + 1) * sub) for r in range(tile // sub)]
    mixes = [_dot(jnp.concatenate([a_ref[rows, :], g_ref[rows, :]], axis=1), wo_ref[...]) for rows in subs]
    for rows, mix in zip(subs, mixes):
        x2 = _layer_norm(alpha * x_ref[rows, :] + mix, lg_ref[...], lb_ref[...])
        x2_ref[rows, :] = x2
        q2_ref[rows, :] = _dot(x2.astype(BF16), wq_ref[...]).astype(BF16)


def _outproj(a, g, x, w_out, ln_g, ln_b, wq, alpha):
    n, d = x.shape
    tile = min(ROW_TILE, n)
    return pl.pallas_call(
        functools.partial(_outproj_kernel, alpha=alpha),
        grid=(n // tile,),
        in_specs=[_rows(tile, a.shape[1]), _rows(tile, g.shape[1]), _rows(tile, d), _resident(w_out.shape),
                  _resident((1, d)), _resident((1, d)), _resident(wq.shape)],
        out_specs=[_rows(tile, d), _rows(tile, d)],
        out_shape=[jax.ShapeDtypeStruct((n, d), F32), jax.ShapeDtypeStruct((n, d), BF16)],
        compiler_params=_params("parallel"),
        name="outproj",
    )(a, g, x, w_out, ln_g, ln_b, wq)


def _cross_kernel(q_ref, mk_ref, mv_ref, o_ref, *, n_mem, seq_rows, sub):
    hd = q_ref.shape[1] // MEM_HEADS
    log2_scale = hd ** -0.5 * LOG2E
    chains = []
    for b in range(q_ref.shape[0] // seq_rows):
        for h in range(MEM_HEADS):
            mk = _head_load(mk_ref, h, MEM_HEADS, n_mem, hd, b * n_mem).astype(BF16)
            mv = _head_load(mv_ref, h, MEM_HEADS, n_mem, hd, b * n_mem).astype(BF16)
            for r0 in range(b * seq_rows, (b + 1) * seq_rows, sub):
                chains.append((slice(r0, r0 + sub), slice(h * hd, (h + 1) * hd), mk, mv))

    def scores(chain):
        rows, cols, mk, _ = chain
        return _dot_nt(q_ref[rows, cols], mk)

    s = scores(chains[0])
    for i, (rows, cols, _, mv) in enumerate(chains):
        s_next = scores(chains[i + 1]) if i + 1 < len(chains) else None
        p = jnp.exp2((s - jnp.max(s, axis=-1, keepdims=True)) * log2_scale)
        o = _dot(p.astype(BF16), mv)
        o_ref[rows, cols] = (o * (1.0 / jnp.sum(p, axis=-1, keepdims=True))).astype(BF16)
        s = s_next


def _cross(q2, mem_k, mem_v, batch, seq, n_mem):
    d = q2.shape[1]
    per_step = max(1, CROSS_STEP_ROWS // seq)
    qblk = pl.BlockSpec((per_step * seq, d), lambda i: (i, 0))
    mblk = pl.BlockSpec((per_step * _head_view_rows(n_mem, MEM_HEADS, d // MEM_HEADS), LANES), lambda i: (i, 0))
    return pl.pallas_call(
        functools.partial(_cross_kernel, n_mem=n_mem, seq_rows=seq, sub=min(CROSS_SUB_TILE, seq)),
        grid=(batch // per_step,),
        in_specs=[qblk, mblk, mblk],
        out_specs=qblk,
        out_shape=jax.ShapeDtypeStruct(q2.shape, BF16),
        compiler_params=_params("parallel"),
        name="cross",
    )(q2, mem_k, mem_v)


def _memkv_kernel(*refs, n_side):
    m_ref, wk_ref, wv_ref = refs[:3]
    side_in, (k_ref, v_ref), side_out = refs[3:3 + n_side], refs[3 + n_side:5 + n_side], refs[5 + n_side:]
    _side_cast(side_in, side_out)
    mb = m_ref[...].astype(BF16)
    k = _dot(mb, wk_ref[...].astype(BF16))
    v = _dot(mb, wv_ref[...].astype(BF16))
    hd = k.shape[1] // MEM_HEADS
    for h in range(MEM_HEADS):
        _head_store(k_ref, h, MEM_HEADS, k[:, h * hd:(h + 1) * hd])
        _head_store(v_ref, h, MEM_HEADS, v[:, h * hd:(h + 1) * hd])


def _memkv(mem, wk, wv, side=()):
    n, d = mem.shape
    hd = d // MEM_HEADS
    tile = min(MEMKV_TILE, n)
    side_in, side_out, side_shapes = _side_cast_specs(side, n // tile)
    outs = pl.pallas_call(
        functools.partial(_memkv_kernel, n_side=len(side)),
        grid=(n // tile,),
        in_specs=[_rows(tile, d), _resident(wk.shape), _resident(wv.shape)] + side_in,
        out_specs=[_head_rows(tile, MEM_HEADS, hd)] * 2 + side_out,
        out_shape=[_head_shape(n, MEM_HEADS, hd)] * 2 + side_shapes,
        compiler_params=_params("parallel"),
        name="memkv",
    )(mem, wk, wv, *side)
    return outs[0], outs[1], tuple(outs[2:])


def _gating_weights(ws, bs, seq):
    n = min(seq, GMLP_CHUNK)
    rep = GMLP_CHUNK // n
    w = ws[:, :n, :n]
    if rep > 1:
        eye = jnp.eye(rep, dtype=ws.dtype)
        w = jnp.einsum('ab,gts->gatbs', eye, w).reshape(GMLP_GROUPS, GMLP_CHUNK, GMLP_CHUNK)
    return w, jnp.tile(bs[:, :n], (1, rep)).T


def _trunk(x, batch, seq, mem_k, mem_v, cache, p, layer_idx, alpha, want_vrows):
    lam_init = 0.8 - 0.6 * math.exp(-0.3 * layer_idx)
    ws, bs_t = _gating_weights(p['gmlp_ws'], p['gmlp_bs'], seq)
    lam_vecs = (p['lambda_q1'], p['lambda_k1'], p['lambda_q2'], p['lambda_k2'])

    if 'ffn2_w_gu' in p:
        x1, _ = _ffn(x, p['ffn1_w_gu'], p['ffn1_w_down'], p['ln1_g'], p['ln1_b'], alpha)
    else:
        x1, (p['ffn2_w_gu'], p['ffn2_w_down']) = _ffn(x, p['ffn1_w_gu'], p['ffn1_w_down'], p['ln1_g'], p['ln1_b'],
                                                      alpha, side=(p['ffn2_w_gu_f32'], p['ffn2_w_down_f32']))
    keys_on_lanes = cache is None
    outs = _inproj(x1, p['w_in_rest'], p['w_in_keys_t'] if keys_on_lanes else p['w_in_keys'], p['gmlp_ln_g'],
                   p['gmlp_ln_b'], ws, bs_t, batch, seq, keys_on_lanes, want_vrows)
    q, k, v, gated = outs[:4]
    if cache is None:
        a = _attn_prompt(q, k, v, lam_vecs, p['subln_g'], batch, seq, lam_init)
    else:
        a = _attn_sample(q, k, v, cache[0], cache[1], lam_vecs, p['subln_g'], batch, seq, lam_init)
    x2, q2 = _outproj(a, gated, x1, p['w_out'], p['ln2_g'], p['ln2_b'], p['cross_wq'], alpha)
    n_mem = mem_k.size // (batch * x.shape[1])
    o = _cross(q2, mem_k, mem_v, batch, seq, n_mem)
    y, _ = _ffn(x2, p['ffn2_w_gu'], p['ffn2_w_down'], p['ln4_g'], p['ln4_b'], alpha,
                proj=(o, p['cross_wo'], p['ln3_g'], p['ln3_b']))
    return y, k, v, (outs[4] if want_vrows else None)


_MATRICES = ('w_in', 'w_out', 'cross_wq', 'cross_wo')
_SIDE_CAST_MATRICES = ('ffn1_w_gu', 'ffn1_w_down', 'ffn2_w_gu', 'ffn2_w_down')
_ROW_VECTORS = ('ln1_g', 'ln1_b', 'lambda_q1', 'lambda_k1', 'lambda_q2', 'lambda_k2', 'subln_g', 'gmlp_ln_g',
                'gmlp_ln_b', 'ln2_g', 'ln2_b', 'ln3_g', 'ln3_b', 'ln4_g', 'ln4_b')


def kernel(x_prompt, x_sample, cache_k, cache_v, cache_mem_k, cache_mem_v, mem_prompt, ffn1_w_gu, ffn1_w_down, ln1_g, ln1_b, w_in, lambda_q1, lambda_k1, lambda_q2, lambda_k2, subln_g, gmlp_ln_g, gmlp_ln_b, gmlp_ws, gmlp_bs, w_out, ln2_g, ln2_b, cross_wq, cross_wk, cross_wv, cross_wo, ln3_g, ln3_b, ffn2_w_gu, ffn2_w_down, ln4_g, ln4_b):
    weights = dict(ffn1_w_gu=ffn1_w_gu, ffn1_w_down=ffn1_w_down, ln1_g=ln1_g, ln1_b=ln1_b, w_in=w_in,
                   lambda_q1=lambda_q1, lambda_k1=lambda_k1, lambda_q2=lambda_q2, lambda_k2=lambda_k2,
                   subln_g=subln_g, gmlp_ln_g=gmlp_ln_g, gmlp_ln_b=gmlp_ln_b, gmlp_ws=gmlp_ws, gmlp_bs=gmlp_bs,
                   w_out=w_out, ln2_g=ln2_g, ln2_b=ln2_b, cross_wq=cross_wq, cross_wk=cross_wk, cross_wv=cross_wv,
                   cross_wo=cross_wo, ln3_g=ln3_g, ln3_b=ln3_b, ffn2_w_gu=ffn2_w_gu, ffn2_w_down=ffn2_w_down,
                   ln4_g=ln4_g, ln4_b=ln4_b)
    depth = w_in.shape[0]
    alpha = (2 * depth) ** 0.25
    bp, tp, d = x_prompt.shape
    bs_, ts, _ = x_sample.shape
    n_mem = mem_prompt.shape[1]
    past = cache_k.shape[2]
    mem_hd = d // MEM_HEADS

    xp =x_prompt.reshape(bp * tp, d)
    xs = x_sample.reshape(bs_ * ts, d)
    mem2d = mem_prompt.reshape(bp * n_mem, d)
    per_layer = [[] for _ in range(7)]
    for l in range(depth):
        p = {}
        for name, w in weights.items():
            w = w[l]
            if name in _MATRICES:
                w = w.astype(BF16)
            elif name in _SIDE_CAST_MATRICES:
                name = name + '_f32'
            elif name in _ROW_VECTORS:
                w = w.reshape(1, -1)
            p[name] = w
        w_in_l = p.pop('w_in')
        p['w_in_rest'] = jnp.concatenate([w_in_l[:, :DIFF_WIDTH], w_in_l[:, 2 * DIFF_WIDTH:]], axis=1)
        p['w_in_keys'] = w_in_l[:, DIFF_WIDTH:2 * DIFF_WIDTH]
        p['w_in_keys_t'] = p['w_in_keys'].T
        mem_k, mem_v, (p['ffn1_w_gu'], p['ffn1_w_down']) = _memkv(
            mem2d, p['cross_wk'], p['cross_wv'], side=(p['ffn1_w_gu_f32'], p['ffn1_w_down_f32']))
        xp, kp_t, vp, _ = _trunk(xp, bp, tp, mem_k, mem_v, None, p, l, alpha, False)
        cache = (jnp.transpose(cache_k[l], (0, 2, 3, 4, 1)).reshape(bs_, DIFF_WIDTH, past),
                 _to_head_view(cache_v[l]))
        xs, ks, vs, gvs = _trunk(xs, bs_, ts, _to_head_view(cache_mem_k[l]), _to_head_view(cache_mem_v[l]),
                                 cache, p, l, alpha, True)
        kp = jnp.transpose(kp_t.reshape(bp, DIFF_HEADS, 2, HEAD_DIM, tp), (0, 4, 1, 2, 3))
        vals = (kp, _from_head_view(vp, (bp, tp), DIFF_HEADS, HEAD_WIDTH),
                _from_head_view(mem_k, (bp, n_mem), MEM_HEADS, mem_hd),
                _from_head_view(mem_v, (bp, n_mem), MEM_HEADS, mem_hd), ks,
                _from_head_view(vs, (bs_, ts), DIFF_HEADS, HEAD_WIDTH),
                _from_head_view(gvs, (bs_, ts), GMLP_GROUPS, GMLP_GROUP_DIM))
        for acc, val in zip(per_layer, vals):
            acc.append(val)

    kp, vp, mk, mv, ks, vs, gvs = (vals[0][None] if depth == 1 else jnp.stack(vals) for vals in per_layer)
    return (xp.reshape(bp, tp, d), xs.reshape(bs_, ts, d),
            kp.reshape(depth, bp, tp, DIFF_HEADS, 2, HEAD_DIM),
            vp.reshape(depth, bp, tp, DIFF_HEADS, HEAD_WIDTH),
            mk.reshape(depth, bp, n_mem, MEM_HEADS, mem_hd),
            mv.reshape(depth, bp, n_mem, MEM_HEADS, mem_hd),
            ks.reshape(depth, bs_, ts, DIFF_HEADS, 2, HEAD_DIM),
            vs.reshape(depth, bs_, ts, DIFF_HEADS, HEAD_WIDTH),
            gvs.reshape(depth, bs_, ts, GMLP_GROUPS, GMLP_GROUP_DIM))
```

```python
import functools
import math

import jax
import jax.numpy as jnp
from jax import lax
from jax.experimental import pallas as pl
from jax.experimental.pallas import tpu as pltpu

F32 = jnp.float32
BF16 = jnp.bfloat16

CHUNK = 64
DIFF_HEADS = 4
HEAD_DIM = 64
HEAD_WIDTH = 2 * HEAD_DIM
DIFF_WIDTH = DIFF_HEADS * HEAD_WIDTH
GMLP_GROUPS = 4
GMLP_CHUNK = 128
GMLP_GROUP_DIM = 128
GMLP_WIDTH = GMLP_GROUPS * GMLP_GROUP_DIM
MEM_HEADS = 4
LN_EPS = 1e-5
MASKED = -1e30
LANES = 128
BF16_SUBLANES = 16
LOG2E = math.log2(math.e)
Q_SCALE = HEAD_DIM ** -0.5 * LOG2E
POS_SPLIT = 64
BIAS_TERMS = 3

ROW_TILE = 1024
FF_CHUNK = 256
FFN_SUB_TILE = 256
MEMKV_TILE = 512
ATTN_Q_TILE = 256
ATTN_HEADS_PER_STEP = 4
ATTN_LOOKAHEAD = 2
CROSS_STEP_ROWS = 128
CROSS_SUB_TILE = 512
VMEM_LIMIT_BYTES = 56 * 1024 * 1024


def _params(*semantics):
    return pltpu.CompilerParams(dimension_semantics=semantics, vmem_limit_bytes=VMEM_LIMIT_BYTES)


def _resident(shape):
    return pl.BlockSpec(shape, lambda *_: (0,) * len(shape), pipeline_mode=pl.Buffered(1))


def _rows(tile, width):
    return pl.BlockSpec((tile, width), lambda i: (i, 0))


def _layer_norm(h, g, b):
    mu = jnp.mean(h, axis=-1, keepdims=True)
    d = h - mu
    var = jnp.mean(d * d, axis=-1, keepdims=True)
    return d * lax.rsqrt(var + LN_EPS) * g + b


def _dot(a, b):
    return jnp.dot(a, b, preferred_element_type=F32)


def _dot_nt(a, b):
    return lax.dot_general(a, b, (((1,), (1,)), ((), ())), preferred_element_type=F32)


def _head_view_rows(rows, heads, width):
    return rows * heads * (width // LANES)


def _head_load(ref, h, heads, rows, width, row0=0):
    tiles = width // LANES
    parts = [ref[pl.ds((row0 * tiles + j) * heads + h, rows, stride=heads * tiles), :] for j in range(tiles)]
    return parts[0] if tiles == 1 else jnp.concatenate(parts, axis=1)


def _head_store(ref, h, heads, value, row0=0):
    rows, width = value.shape
    tiles = width // LANES
    for j in range(tiles):
        start = (row0 * tiles + j) * heads + h
        ref[pl.ds(start, rows, stride=heads * tiles), :] = value[:, j * LANES:(j + 1) * LANES]


def _to_head_view(x):
    *lead, heads, width = x.shape
    tiles = width // LANES
    x = x.reshape(*lead, heads, tiles, LANES)
    x = jnp.swapaxes(x, -3, -2)
    return x.reshape(-1, LANES)


def _from_head_view(x, lead, heads, width):
    tiles = width // LANES
    x = x.reshape(*lead, tiles, heads, LANES)
    x = jnp.swapaxes(x, -3, -2)
    return x.reshape(*lead, heads, width)


def _side_cast_specs(weights, steps):
    in_specs, out_specs, out_shapes = [], [], []
    for w in weights:
        rows, cols = w.shape
        units = rows // BF16_SUBLANES
        assert units * BF16_SUBLANES == rows
        blocks = max(b for b in range(1, min(steps, units) + 1) if units % b == 0)
        spec = pl.BlockSpec((rows // blocks, cols), lambda i, last=blocks - 1: (jnp.minimum(i, last), 0))
        in_specs.append(spec)
        out_specs.append(spec)
        out_shapes.append(jax.ShapeDtypeStruct((rows, cols), BF16))
    return in_specs, out_specs, out_shapes


def _side_cast(src_refs, dst_refs):
    for src, dst in zip(src_refs, dst_refs):
        dst[...] = src[...].astype(BF16)


def _ffn_kernel(*refs, alpha, d_ff, with_proj, n_side):
    n_in = 9 if with_proj else 5
    side_in, side_out = refs[n_in:n_in + n_side], refs[n_in + n_side + 1:n_in + 2 * n_side + 1]
    out_ref, act_ref = refs[n_in + n_side], refs[-1]
    if with_proj:
        o_ref, xin_ref, wo_ref, lpg_ref, lpb_ref, wgu_ref, wd_ref, g_ref, b_ref = refs[:n_in]
    else:
        xin_ref, wgu_ref, wd_ref, g_ref, b_ref = refs[:n_in]
    _side_cast(side_in, side_out)
    tile = xin_ref.shape[0]
    sub = min(FFN_SUB_TILE, tile)
    subs = [slice(r * sub, (r + 1) * sub) for r in range(tile // sub)]
    xs = []
    for rows in subs:
        x = xin_ref[rows, :]
        if with_proj:
            x = _layer_norm(alpha * x + _dot(o_ref[rows, :], wo_ref[...]), lpg_ref[...], lpb_ref[...])
        xs.append(x)
    for rows, x in zip(subs, xs):
        xb = x.astype(BF16)
        for c in range(d_ff // FF_CHUNK):
            lo = c * FF_CHUNK
            gate = _dot(xb, wgu_ref[:, lo:lo + FF_CHUNK])
            up = _dot(xb, wgu_ref[:, d_ff + lo:d_ff + lo + FF_CHUNK])
            act_ref[rows, lo:lo + FF_CHUNK] = (gate * jax.nn.sigmoid(gate) * up).astype(BF16)
    ys = [_dot(act_ref[rows, :], wd_ref[...]) for rows in subs]
    for rows, x, y in zip(subs, xs, ys):
        out_ref[rows, :] = _layer_norm(alpha * x + 0.5 * y, g_ref[...], b_ref[...])


def _ffn(x, w_gu, w_down, ln_g, ln_b, alpha, proj=None, side=()):
    n, d = x.shape
    d_ff = w_down.shape[0]
    tile = min(ROW_TILE, n)
    side_in, side_out, side_shapes = _side_cast_specs(side, n // tile)
    in_specs, args = [], []
    if proj is not None:
        o, wo, pg, pb = proj
        in_specs += [_rows(tile, o.shape[1])]
        args += [o]
    in_specs += [_rows(tile, d)]
    args += [x]
    if proj is not None:
        in_specs += [_resident(wo.shape), _resident((1, d)), _resident((1, d))]
        args += [wo, pg, pb]
    in_specs += [_resident(w_gu.shape), _resident(w_down.shape), _resident((1, d)), _resident((1, d))]
    args += [w_gu, w_down, ln_g, ln_b]
    outs = pl.pallas_call(
        functools.partial(_ffn_kernel, alpha=alpha, d_ff=d_ff, with_proj=proj is not None, n_side=len(side)),
        grid=(n // tile,),
        in_specs=in_specs + side_in,
        out_specs=[_rows(tile, d)] + side_out,
        out_shape=[jax.ShapeDtypeStruct((n, d), F32)] + side_shapes,
        scratch_shapes=[pltpu.VMEM((tile, d_ff), BF16)],
        compiler_params=_params("parallel"),
        name="ffn_proj" if proj is not None else "ffn",
    )(*args, *side)
    return outs[0], tuple(outs[1:])


def _inproj_kernel(x_ref, w_ref, wk_ref, lng_ref, lnb_ref, ws_ref, bst_ref, q_ref, k_ref, v_ref, g_ref, *vrows_ref,
                   keys_on_lanes):
    tile = x_ref.shape[0]
    sub = min(FFN_SUB_TILE, tile)
    row = lax.broadcasted_iota(jnp.int32, (GMLP_CHUNK, GMLP_CHUNK), 0)
    col = lax.broadcasted_iota(jnp.int32, (GMLP_CHUNK, GMLP_CHUNK), 1)
    mix_w = [jnp.where(row >= col, ws_ref[g], 0.0).astype(BF16) for g in range(GMLP_GROUPS)]

    def project(r0):
        xb = x_ref[r0:r0 + sub, :].astype(BF16)
        z = _dot(xb, w_ref[:, :2 * DIFF_WIDTH])
        q_ref[r0:r0 + sub, :] = (z[:, :DIFF_WIDTH] * Q_SCALE).astype(BF16)
        if keys_on_lanes:
            k_ref[:, r0:r0 + sub] = _dot_nt(wk_ref[...], xb)
        else:
            k_ref[r0:r0 + sub, :] = _dot(xb, wk_ref[...])
        for h in range(DIFF_HEADS):
            _head_store(v_ref, h, DIFF_HEADS, z[:, DIFF_WIDTH + h * HEAD_WIDTH:DIFF_WIDTH + (h + 1) * HEAD_WIDTH], r0)

    def gate(r0, zg):
        u = jax.nn.gelu(zg[:, :GMLP_WIDTH])
        vn = _layer_norm(jax.nn.gelu(zg[:, GMLP_WIDTH:]), lng_ref[...], lnb_ref[...])
        if vrows_ref:
            for g in range(GMLP_GROUPS):
                _head_store(vrows_ref[0], g, GMLP_GROUPS, vn[:, g * GMLP_GROUP_DIM:(g + 1) * GMLP_GROUP_DIM], r0)
        vnb = vn.astype(BF16)
        for g in range(GMLP_GROUPS):
            cols = slice(g * GMLP_GROUP_DIM, (g + 1) * GMLP_GROUP_DIM)
            bias = bst_ref[:, g:g + 1]
            for c in range(sub // GMLP_CHUNK):
                rows = slice(c * GMLP_CHUNK, (c + 1) * GMLP_CHUNK)
                mixed = _dot(mix_w[g], vnb[rows, cols]) + bias
                g_ref[r0 + rows.start:r0 + rows.stop, cols] = (u[rows, cols] * mixed).astype(BF16)

    starts = list(range(0, tile, sub))
    zgs = {}
    for i, r0 in enumerate(starts):
        zgs[r0] = _dot(x_ref[r0:r0 + sub, :].astype(BF16), w_ref[:, 2 * DIFF_WIDTH:])
        if i:
            gate(starts[i - 1], zgs.pop(starts[i - 1]))
        project(r0)
    gate(starts[-1], zgs.pop(starts[-1]))


def _head_rows(tile, heads, width):
    return _rows(_head_view_rows(tile, heads, width), LANES)


def _head_shape(rows, heads, width):
    return jax.ShapeDtypeStruct((_head_view_rows(rows, heads, width), LANES), F32)


def _inproj(x, w_rest, w_keys, ln_g, ln_b, ws, bs_t, batch, seq, keys_on_lanes, want_vrows):
    n, d = x.shape
    tile = min(ROW_TILE, n)
    if keys_on_lanes:
        per = seq // tile
        k_shape = jax.ShapeDtypeStruct((batch, DIFF_WIDTH, seq), F32)
        k_spec = pl.BlockSpec((None, DIFF_WIDTH, tile), lambda i: (i // per, 0, i % per))
    else:
        k_shape = jax.ShapeDtypeStruct((n, DIFF_WIDTH), F32)
        k_spec = _rows(tile, DIFF_WIDTH)
    out_shape = [jax.ShapeDtypeStruct((n, DIFF_WIDTH), BF16), k_shape,
                 _head_shape(n, DIFF_HEADS, HEAD_WIDTH),
                 jax.ShapeDtypeStruct((n, GMLP_WIDTH), BF16)]
    out_specs = [_rows(tile, DIFF_WIDTH), k_spec, _head_rows(tile, DIFF_HEADS, HEAD_WIDTH), _rows(tile, GMLP_WIDTH)]
    if want_vrows:
        out_shape.append(_head_shape(n, GMLP_GROUPS, GMLP_GROUP_DIM))
        out_specs.append(_head_rows(tile, GMLP_GROUPS, GMLP_GROUP_DIM))
    return pl.pallas_call(
        functools.partial(_inproj_kernel, keys_on_lanes=keys_on_lanes),
        grid=(n // tile,),
        in_specs=[_rows(tile, d), _resident(w_rest.shape), _resident(w_keys.shape), _resident((1, GMLP_WIDTH)),
                  _resident((1, GMLP_WIDTH)), _resident(ws.shape), _resident(bs_t.shape)],
        out_specs=out_specs,
        out_shape=out_shape,
        compiler_params=_params("parallel"),
        name="inproj",
    )(x, w_rest, w_keys, ln_g, ln_b, ws, bs_t)


def _head_scalars(head, lq1_ref, lk1_ref, lq2_ref, lk2_ref, lam_init):
    slope = jnp.exp2(jnp.zeros((1, 1), F32) - 8.0 * (head + 1).astype(F32) / DIFF_HEADS) * LOG2E
    lam = (jnp.exp(jnp.sum(lq1_ref[...] * lk1_ref[...], axis=-1, keepdims=True))
           - jnp.exp(jnp.sum(lq2_ref[...] * lk2_ref[...], axis=-1, keepdims=True)) + lam_init)
    return slope, lam


def _bias_lanes(slope, rows):
    lane = lax.broadcasted_iota(jnp.int32, (1, LANES), 1)
    out = jnp.zeros((1, LANES), F32)
    rest = slope
    for i in range(BIAS_TERMS):
        term = rest.astype(BF16).astype(F32)
        out = jnp.where(lane == 2 * i, term * POS_SPLIT, jnp.where(lane == 2 * i + 1, term, out))
        rest = rest - term
    return jnp.broadcast_to(out, (rows, LANES)).astype(BF16)


def _position_rows(n):
    assert n <= POS_SPLIT * 256
    row = lax.broadcasted_iota(jnp.int32, (LANES, n), 0)
    pos = lax.broadcasted_iota(jnp.int32, (LANES, n), 1)
    val = jnp.where(row % 2 == 0, pos // POS_SPLIT, pos % POS_SPLIT)
    return jnp.where(row < 2 * BIAS_TERMS, val, 0).astype(F32).astype(BF16)


def _stack_maps(q):
    lane = lax.broadcasted_iota(jnp.int32, q.shape, 1)
    zero = jnp.zeros_like(q)
    return jnp.concatenate([jnp.where(lane < HEAD_DIM, q, zero), jnp.where(lane >= HEAD_DIM, q, zero)], axis=0)


def _near_bias(slope, t, base_q, base_k, nk):
    r = lax.broadcasted_iota(jnp.int32, (t, nk), 0) + base_q
    c = lax.broadcasted_iota(jnp.int32, (t, nk), 1) + base_k
    bias = slope * (r - jnp.abs(r - c)).astype(F32)
    return jnp.where(c // CHUNK <= r // CHUNK, bias, MASKED)


def _diff_combine(parts, values, t, lam):
    return _weighted_values(_softmax_weights(parts), values, t, lam)


def _softmax_weights(parts):
    m = functools.reduce(jnp.maximum, [jnp.max(s, axis=-1, keepdims=True) for s in parts])
    return [jnp.exp2(s - m).astype(BF16) for s in parts]


def _weighted_values(weights, values, t, lam):
    res = functools.reduce(jnp.add, [_dot(p, v) for p, v in zip(weights, values)])
    out, denom = res[:, :HEAD_WIDTH], res[:, HEAD_WIDTH:HEAD_WIDTH + 1]
    return out[:t] * (1.0 / denom[:t]) - out[t:] * (lam / denom[t:])


def _with_ones(v):
    lane = lax.broadcasted_iota(jnp.int32, v.shape, 1)
    return jnp.concatenate([v, jnp.where(lane == 0, 1.0, 0.0).astype(v.dtype)], axis=1)


def _sub_norm(o, g, lam_init):
    return o * lax.rsqrt(jnp.mean(o * o, axis=-1, keepdims=True) + LN_EPS) * g * (1.0 - lam_init)


def _attn_prompt_kernel(q_ref, k_ref, v_ref, lq1_ref, lk1_ref, lq2_ref, lk2_ref, sg_ref, o_ref,
                        kb_ref, vb_ref, *, lam_init, group):
    seq = q_ref.shape[0]
    tq = ATTN_Q_TILE
    step = pl.program_id(1)

    @pl.when((pl.program_id(0) == 0) & (step == 0))
    def _():
        for j in range(group):
            kb_ref[j, HEAD_WIDTH:, :] = _position_rows(seq)
            vb_ref[j] = _with_ones(jnp.zeros((seq, HEAD_WIDTH), BF16))

    for j in range(group):
        kb_ref[j, :HEAD_WIDTH, :] = k_ref[j * HEAD_WIDTH:(j + 1) * HEAD_WIDTH, :].astype(BF16)
    for g in range(DIFF_HEADS // group):
        @pl.when(step == g)
        def _():
            for j in range(group):
                vb_ref[j, :, :HEAD_WIDTH] = _head_load(v_ref, g * group + j, DIFF_HEADS, seq, HEAD_WIDTH).astype(BF16)

    r = lax.broadcasted_iota(jnp.int32, (tq, tq), 0)
    c = lax.broadcasted_iota(jnp.int32, (tq, tq), 1)
    allowed = c // CHUNK <= r // CHUNK
    lead = -2.0 * jnp.maximum(c - r, 0).astype(F32)
    q_bias, diag = [], []
    for j in range(group):
        slope, lam = _head_scalars(step * group + j, lq1_ref, lk1_ref, lq2_ref, lk2_ref, lam_init)
        q_bias.append(_bias_lanes(slope, 2 * tq))
        d = jnp.where(allowed, slope * lead, MASKED)
        diag.append(jnp.concatenate([d, d], axis=0))

    def scores(j, lo):
        cols = slice(j * HEAD_WIDTH, (j + 1) * HEAD_WIDTH)
        qs = jnp.concatenate([_stack_maps(q_ref[lo:lo + tq, cols]), q_bias[j]], axis=1)
        parts = [_dot(qs, kb_ref[j, :, lo:lo + tq]) + diag[j]]
        if lo:
            parts.append(_dot(qs, kb_ref[j, :, 0:lo]))
        return parts

    def finish(j, lo, weights):
        values = [vb_ref[j, lo:lo + tq, :]] + ([vb_ref[j, 0:lo, :]] if lo else [])
        out = _weighted_values(weights, values, tq, lam)
        o_ref[lo:lo + tq, j * HEAD_WIDTH:(j + 1) * HEAD_WIDTH] = _sub_norm(out, sg_ref[...], lam_init).astype(BF16)

    chains = [(j, lo) for lo in range(0, seq, tq) for j in range(group)]
    pending = [scores(*chain) for chain in chains[:ATTN_LOOKAHEAD]]
    for i, chain in enumerate(chains):
        if i + ATTN_LOOKAHEAD < len(chains):
            pending.append(scores(*chains[i + ATTN_LOOKAHEAD]))
        finish(*chain, _softmax_weights(pending.pop(0)))


def _attn_prompt(q, k_t, v, lam_vecs, subln_g, batch, seq, lam_init):
    group = ATTN_HEADS_PER_STEP
    blk = pl.BlockSpec((seq, group * HEAD_WIDTH), lambda b, g: (b, g))
    vec = pl.BlockSpec((1, HEAD_DIM), lambda b, g: (0, 0))
    return pl.pallas_call(
        functools.partial(_attn_prompt_kernel, lam_init=lam_init, group=group),
        grid=(batch, DIFF_HEADS // group),
        in_specs=[blk, pl.BlockSpec((None, group * HEAD_WIDTH, seq), lambda b, g: (b, g, 0)),
                  pl.BlockSpec((_head_view_rows(seq, DIFF_HEADS, HEAD_WIDTH), LANES), lambda b, g: (b, 0)),
                  vec, vec, vec, vec, pl.BlockSpec((1, HEAD_WIDTH), lambda b, g: (0, 0))],
        out_specs=blk,
        out_shape=jax.ShapeDtypeStruct((batch * seq, DIFF_WIDTH), BF16),
        scratch_shapes=[pltpu.VMEM((group, HEAD_WIDTH + LANES, seq), BF16),
                        pltpu.VMEM((group, seq, 2 * HEAD_WIDTH), BF16)],
        compiler_params=_params("arbitrary", "arbitrary"),
        name="attn_prompt",
    )(q, k_t, v, *lam_vecs, subln_g)


def _attn_sample_kernel(q_ref, kn_ref, vn_ref, kc_ref, vc_ref, lq1_ref, lk1_ref, lq2_ref, lk2_ref, sg_ref, o_ref,
                        *, lam_init):
    t = q_ref.shape[0]
    past = kc_ref.shape[1]
    far_pos = lax.broadcasted_iota(jnp.int32, (1, past), 1).astype(F32)
    for h in range(DIFF_HEADS):
        cols = slice(h * HEAD_WIDTH, (h + 1) * HEAD_WIDTH)
        slope, lam = _head_scalars(jnp.int32(h), lq1_ref, lk1_ref, lq2_ref, lk2_ref, lam_init)
        qs = _stack_maps(q_ref[:, cols])
        near = _near_bias(slope, t, past, past, t)
        parts = [_dot_nt(qs, kn_ref[:, cols].astype(BF16)) + jnp.concatenate([near, near], axis=0),
                 _dot(qs, kc_ref[cols, :].astype(BF16)) + slope * far_pos]
        values = [_with_ones(_head_load(vn_ref, h, DIFF_HEADS, t, HEAD_WIDTH).astype(BF16)),
                  _with_ones(_head_load(vc_ref, h, DIFF_HEADS, past, HEAD_WIDTH).astype(BF16))]
        out = _diff_combine(parts, values, t, lam)
        o_ref[:, cols] = _sub_norm(out, sg_ref[...], lam_init).astype(BF16)


def _attn_sample(q, k_new, v_new, k_cache_t, v_cache, lam_vecs, subln_g, batch, seq, lam_init):
    past = k_cache_t.shape[2]
    new = pl.BlockSpec((seq, DIFF_WIDTH), lambda b: (b, 0))
    vec = pl.BlockSpec((1, HEAD_DIM), lambda b: (0, 0))
    return pl.pallas_call(
        functools.partial(_attn_sample_kernel, lam_init=lam_init),
        grid=(batch,),
        in_specs=[new, new, _head_rows(seq, DIFF_HEADS, HEAD_WIDTH),
                  pl.BlockSpec((None, DIFF_WIDTH, past), lambda b: (b, 0, 0)),
                  _head_rows(past, DIFF_HEADS, HEAD_WIDTH),
                  vec, vec, vec, vec, pl.BlockSpec((1, HEAD_WIDTH), lambda b: (0, 0))],
        out_specs=new,
        out_shape=jax.ShapeDtypeStruct((batch * seq, DIFF_WIDTH), BF16),
        compiler_params=_params("parallel"),
        name="attn_sample",
    )(q, k_new, v_new, k_cache_t, v_cache, *lam_vecs, subln_g)


def _outproj_kernel(a_ref, g_ref, x_ref, wo_ref, lg_ref, lb_ref, wq_ref, x2_ref, q2_ref, *, alpha):
    tile = x_ref.shape[0]
    sub = min(FFN_SUB_TILE, tile)
    subs = [slice(r * sub, (r + 1) * sub) for r in range(tile // sub)]
    mixes = [_dot(jnp.concatenate([a_ref[rows, :], g_ref[rows, :]], axis=1), wo_ref[...]) for rows in subs]
    for rows, mix in zip(subs, mixes):
        x2 = _layer_norm(alpha * x_ref[rows, :] + mix, lg_ref[...], lb_ref[...])
        x2_ref[rows, :] = x2
        q2_ref[rows, :] = _dot(x2.astype(BF16), wq_ref[...]).astype(BF16)


def _outproj(a, g, x, w_out, ln_g, ln_b, wq, alpha):
    n, d = x.shape
    tile = min(ROW_TILE, n)
    return pl.pallas_call(
        functools.partial(_outproj_kernel, alpha=alpha),
        grid=(n // tile,),
        in_specs=[_rows(tile, a.shape[1]), _rows(tile, g.shape[1]), _rows(tile, d), _resident(w_out.shape),
                  _resident((1, d)), _resident((1, d)), _resident(wq.shape)],
        out_specs=[_rows(tile, d), _rows(tile, d)],
        out_shape=[jax.ShapeDtypeStruct((n, d), F32), jax.ShapeDtypeStruct((n, d), BF16)],
        compiler_params=_params("parallel"),
        name="outproj",
    )(a, g, x, w_out, ln_g, ln_b, wq)


def _cross_kernel(q_ref, mk_ref, mv_ref, o_ref, *, n_mem, seq_rows, sub):
    hd = q_ref.shape[1] // MEM_HEADS
    log2_scale = hd ** -0.5 * LOG2E
    chains = []
    for b in range(q_ref.shape[0] // seq_rows):
        for h in range(MEM_HEADS):
            mk = _head_load(mk_ref, h, MEM_HEADS, n_mem, hd, b * n_mem).astype(BF16)
            mv = _head_load(mv_ref, h, MEM_HEADS, n_mem, hd, b * n_mem).astype(BF16)
            for r0 in range(b * seq_rows, (b + 1) * seq_rows, sub):
                chains.append((slice(r0, r0 + sub), slice(h * hd, (h + 1) * hd), mk, mv))

    def scores(chain):
        rows, cols, mk, _ = chain
        return _dot_nt(q_ref[rows, cols], mk)

    s = scores(chains[0])
    for i, (rows, cols, _, mv) in enumerate(chains):
        s_next = scores(chains[i + 1]) if i + 1 < len(chains) else None
        p = jnp.exp2((s - jnp.max(s, axis=-1, keepdims=True)) * log2_scale)
        o = _dot(p.astype(BF16), mv)
        o_ref[rows, cols] = (o * (1.0 / jnp.sum(p, axis=-1, keepdims=True))).astype(BF16)
        s = s_next


def _cross(q2, mem_k, mem_v, batch, seq, n_mem):
    d = q2.shape[1]
    per_step = max(1, CROSS_STEP_ROWS // seq)
    qblk = pl.BlockSpec((per_step * seq, d), lambda i: (i, 0))
    mblk = pl.BlockSpec((per_step * _head_view_rows(n_mem, MEM_HEADS, d // MEM_HEADS), LANES), lambda i: (i, 0))
    return pl.pallas_call(
        functools.partial(_cross_kernel, n_mem=n_mem, seq_rows=seq, sub=min(CROSS_SUB_TILE, seq)),
        grid=(batch // per_step,),
        in_specs=[qblk, mblk, mblk],
        out_specs=qblk,
        out_shape=jax.ShapeDtypeStruct(q2.shape, BF16),
        compiler_params=_params("parallel"),
        name="cross",
    )(q2, mem_k, mem_v)


def _memkv_kernel(*refs, n_side):
    m_ref, wk_ref, wv_ref = refs[:3]
    side_in, (k_ref, v_ref), side_out = refs[3:3 + n_side], refs[3 + n_side:5 + n_side], refs[5 + n_side:]
    _side_cast(side_in, side_out)
    mb = m_ref[...].astype(BF16)
    k = _dot(mb, wk_ref[...].astype(BF16))
    v = _dot(mb, wv_ref[...].astype(BF16))
    hd = k.shape[1] // MEM_HEADS
    for h in range(MEM_HEADS):
        _head_store(k_ref, h, MEM_HEADS, k[:, h * hd:(h + 1) * hd])
        _head_store(v_ref, h, MEM_HEADS, v[:, h * hd:(h + 1) * hd])


def _memkv(mem, wk, wv, side=()):
    n, d = mem.shape
    hd = d // MEM_HEADS
    tile = min(MEMKV_TILE, n)
    side_in, side_out, side_shapes = _side_cast_specs(side, n // tile)
    outs = pl.pallas_call(
        functools.partial(_memkv_kernel, n_side=len(side)),
        grid=(n // tile,),
        in_specs=[_rows(tile, d), _resident(wk.shape), _resident(wv.shape)] + side_in,
        out_specs=[_head_rows(tile, MEM_HEADS, hd)] * 2 + side_out,
        out_shape=[_head_shape(n, MEM_HEADS, hd)] * 2 + side_shapes,
        compiler_params=_params("parallel"),
        name="memkv",
    )(mem, wk, wv, *side)
    return outs[0], outs[1], tuple(outs[2:])


def _gating_weights(ws, bs, seq):
    n = min(seq, GMLP_CHUNK)
    rep = GMLP_CHUNK // n
    w = ws[:, :n, :n]
    if rep > 1:
        eye = jnp.eye(rep, dtype=ws.dtype)
        w = jnp.einsum('ab,gts->gatbs', eye, w).reshape(GMLP_GROUPS, GMLP_CHUNK, GMLP_CHUNK)
    return w, jnp.tile(bs[:, :n], (1, rep)).T


def _trunk(x, batch, seq, mem_k, mem_v, cache, p, layer_idx, alpha, want_vrows):
    lam_init = 0.8 - 0.6 * math.exp(-0.3 * layer_idx)
    ws, bs_t = _gating_weights(p['gmlp_ws'], p['gmlp_bs'], seq)
    lam_vecs = (p['lambda_q1'], p['lambda_k1'], p['lambda_q2'], p['lambda_k2'])

    if 'ffn2_w_gu' in p:
        x1, _ = _ffn(x, p['ffn1_w_gu'], p['ffn1_w_down'], p['ln1_g'], p['ln1_b'], alpha)
    else:
        x1, (p['ffn2_w_gu'], p['ffn2_w_down']) = _ffn(x, p['ffn1_w_gu'], p['ffn1_w_down'], p['ln1_g'], p['ln1_b'],
                                                      alpha, side=(p['ffn2_w_gu_f32'], p['ffn2_w_down_f32']))
    keys_on_lanes = cache is None
    outs = _inproj(x1, p['w_in_rest'], p['w_in_keys_t'] if keys_on_lanes else p['w_in_keys'], p['gmlp_ln_g'],
                   p['gmlp_ln_b'], ws, bs_t, batch, seq, keys_on_lanes, want_vrows)
    q, k, v, gated = outs[:4]
    if cache is None:
        a = _attn_prompt(q, k, v, lam_vecs, p['subln_g'], batch, seq, lam_init)
    else:
        a = _attn_sample(q, k, v, cache[0], cache[1], lam_vecs, p['subln_g'], batch, seq, lam_init)
    x2, q2 = _outproj(a, gated, x1, p['w_out'], p['ln2_g'], p['ln2_b'], p['cross_wq'], alpha)
    n_mem = mem_k.size // (batch * x.shape[1])
    o = _cross(q2, mem_k, mem_v, batch, seq, n_mem)
    y, _ = _ffn(x2, p['ffn2_w_gu'], p['ffn2_w_down'], p['ln4_g'], p['ln4_b'], alpha,
                proj=(o, p['cross_wo'], p['ln3_g'], p['ln3_b']))
    return y, k, v, (outs[4] if want_vrows else None)


_MATRICES = ('w_in', 'w_out', 'cross_wq', 'cross_wo')
_SIDE_CAST_MATRICES = ('ffn1_w_gu', 'ffn1_w_down', 'ffn2_w_gu', 'ffn2_w_down')
_ROW_VECTORS = ('ln1_g', 'ln1_b', 'lambda_q1', 'lambda_k1', 'lambda_q2', 'lambda_k2', 'subln_g', 'gmlp_ln_g',
                'gmlp_ln_b', 'ln2_g', 'ln2_b', 'ln3_g', 'ln3_b', 'ln4_g', 'ln4_b')


def kernel(x_prompt, x_sample, cache_k, cache_v, cache_mem_k, cache_mem_v, mem_prompt, ffn1_w_gu, ffn1_w_down, ln1_g, ln1_b, w_in, lambda_q1, lambda_k1, lambda_q2, lambda_k2, subln_g, gmlp_ln_g, gmlp_ln_b, gmlp_ws, gmlp_bs, w_out, ln2_g, ln2_b, cross_wq, cross_wk, cross_wv, cross_wo, ln3_g, ln3_b, ffn2_w_gu, ffn2_w_down, ln4_g, ln4_b):
    weights = dict(ffn1_w_gu=ffn1_w_gu, ffn1_w_down=ffn1_w_down, ln1_g=ln1_g, ln1_b=ln1_b, w_in=w_in,
                   lambda_q1=lambda_q1, lambda_k1=lambda_k1, lambda_q2=lambda_q2, lambda_k2=lambda_k2,
                   subln_g=subln_g, gmlp_ln_g=gmlp_ln_g, gmlp_ln_b=gmlp_ln_b, gmlp_ws=gmlp_ws, gmlp_bs=gmlp_bs,
                   w_out=w_out, ln2_g=ln2_g, ln2_b=ln2_b, cross_wq=cross_wq, cross_wk=cross_wk, cross_wv=cross_wv,
                   cross_wo=cross_wo, ln3_g=ln3_g, ln3_b=ln3_b, ffn2_w_gu=ffn2_w_gu, ffn2_w_down=ffn2_w_down,
                   ln4_g=ln4_g, ln4_b=ln4_b)
    depth = w_in.shape[0]
    alpha = (2 * depth) ** 0.25
    bp, tp, d = x_prompt.shape
    bs_, ts, _ = x_sample.shape
    n_mem = mem_prompt.shape[1]
    past = cache_k.shape[2]
    mem_hd = d // MEM_HEADS

    xp =x_prompt.reshape(bp * tp, d)
    xs = x_sample.reshape(bs_ * ts, d)
    mem2d = mem_prompt.reshape(bp * n_mem, d)
    per_layer = [[] for _ in range(7)]
    for l in range(depth):
        p = {}
        for name, w in weights.items():
            w = w[l]
            if name in _MATRICES:
                w = w.astype(BF16)
            elif name in _SIDE_CAST_MATRICES:
                name = name + '_f32'
            elif name in _ROW_VECTORS:
                w = w.reshape(1, -1)
            p[name] = w
        w_in_l = p.pop('w_in')
        p['w_in_rest'] = jnp.concatenate([w_in_l[:, :DIFF_WIDTH], w_in_l[:, 2 * DIFF_WIDTH:]], axis=1)
        p['w_in_keys'] = w_in_l[:, DIFF_WIDTH:2 * DIFF_WIDTH]
        p['w_in_keys_t'] = p['w_in_keys'].T
        mem_k, mem_v, (p['ffn1_w_gu'], p['ffn1_w_down']) = _memkv(
            mem2d, p['cross_wk'], p['cross_wv'], side=(p['ffn1_w_gu_f32'], p['ffn1_w_down_f32']))
        xp, kp_t, vp, _ = _trunk(xp, bp, tp, mem_k, mem_v, None, p, l, alpha, False)
        cache = (jnp.transpose(cache_k[l], (0, 2, 3, 4, 1)).reshape(bs_, DIFF_WIDTH, past),
                 _to_head_view(cache_v[l]))
        xs, ks, vs, gvs = _trunk(xs, bs_, ts, _to_head_view(cache_mem_k[l]), _to_head_view(cache_mem_v[l]),
                                 cache, p, l, alpha, True)
        kp = jnp.transpose(kp_t.reshape(bp, DIFF_HEADS, 2, HEAD_DIM, tp), (0, 4, 1, 2, 3))
        vals = (kp, _from_head_view(vp, (bp, tp), DIFF_HEADS, HEAD_WIDTH),
                _from_head_view(mem_k, (bp, n_mem), MEM_HEADS, mem_hd),
                _from_head_view(mem_v, (bp, n_mem), MEM_HEADS, mem_hd), ks,
                _from_head_view(vs, (bs_, ts), DIFF_HEADS, HEAD_WIDTH),
                _from_head_view(gvs, (bs_, ts), GMLP_GROUPS, GMLP_GROUP_DIM))
        for acc, val in zip(per_layer, vals):
            acc.append(val)

    kp, vp, mk, mv, ks, vs, gvs = (vals[0][None] if depth == 1 else jnp.stack(vals) for vals in per_layer)
    return (xp.reshape(bp, tp, d), xs.reshape(bs_, ts, d),
            kp.reshape(depth, bp, tp, DIFF_HEADS, 2, HEAD_DIM),
            vp.reshape(depth, bp, tp, DIFF_HEADS, HEAD_WIDTH),
            mk.reshape(depth, bp, n_mem, MEM_HEADS, mem_hd),
            mv.reshape(depth, bp, n_mem, MEM_HEADS, mem_hd),
            ks.reshape(depth, bs_, ts, DIFF_HEADS, 2, HEAD_DIM),
            vs.reshape(depth, bs_, ts, DIFF_HEADS, HEAD_WIDTH),
            gvs.reshape(depth, bs_, ts, GMLP_GROUPS, GMLP_GROUP_DIM))
```

```python
import functools
import math

import jax
import jax.numpy as jnp
from jax import lax
from jax.experimental import pallas as pl
from jax.experimental.pallas import tpu as pltpu

F32 = jnp.float32
BF16 = jnp.bfloat16

CHUNK = 64
DIFF_HEADS = 4
HEAD_DIM = 64
HEAD_WIDTH = 2 * HEAD_DIM
DIFF_WIDTH = DIFF_HEADS * HEAD_WIDTH
GMLP_GROUPS = 4
GMLP_CHUNK = 128
GMLP_GROUP_DIM = 128
GMLP_WIDTH = GMLP_GROUPS * GMLP_GROUP_DIM
MEM_HEADS = 4
LN_EPS = 1e-5
MASKED = -1e30
LANES = 128
BF16_SUBLANES = 16
LOG2E = math.log2(math.e)
Q_SCALE = HEAD_DIM ** -0.5 * LOG2E
POS_SPLIT = 64
BIAS_TERMS = 3

ROW_TILE = 1024
FF_CHUNK = 256
FFN_SUB_TILE = 256
MEMKV_TILE = 512
ATTN_Q_TILE = 256
ATTN_HEADS_PER_STEP = 4
CROSS_STEP_ROWS = 128
CROSS_SUB_TILE = 512
VMEM_LIMIT_BYTES = 56 * 1024 * 1024


def _params(*semantics):
    return pltpu.CompilerParams(dimension_semantics=semantics, vmem_limit_bytes=VMEM_LIMIT_BYTES)


def _resident(shape):
    return pl.BlockSpec(shape, lambda *_: (0,) * len(shape), pipeline_mode=pl.Buffered(1))


def _rows(tile, width):
    return pl.BlockSpec((tile, width), lambda i: (i, 0))


def _layer_norm(h, g, b):
    mu = jnp.mean(h, axis=-1, keepdims=True)
    d = h - mu
    var = jnp.mean(d * d, axis=-1, keepdims=True)
    return d * lax.rsqrt(var + LN_EPS) * g + b


def _dot(a, b):
    return jnp.dot(a, b, preferred_element_type=F32)


def _dot_nt(a, b):
    return lax.dot_general(a, b, (((1,), (1,)), ((), ())), preferred_element_type=F32)


def _head_view_rows(rows, heads, width):
    return rows * heads * (width // LANES)


def _head_load(ref, h, heads, rows, width, row0=0):
    tiles = width // LANES
    parts = [ref[pl.ds((row0 * tiles + j) * heads + h, rows, stride=heads * tiles), :] for j in range(tiles)]
    return parts[0] if tiles == 1 else jnp.concatenate(parts, axis=1)


def _head_store(ref, h, heads, value, row0=0):
    rows, width = value.shape
    tiles = width // LANES
    for j in range(tiles):
        start = (row0 * tiles + j) * heads + h
        ref[pl.ds(start, rows, stride=heads * tiles), :] = value[:, j * LANES:(j + 1) * LANES]


def _to_head_view(x):
    *lead, heads, width = x.shape
    tiles = width // LANES
    x = x.reshape(*lead, heads, tiles, LANES)
    x = jnp.swapaxes(x, -3, -2)
    return x.reshape(-1, LANES)


def _from_head_view(x, lead, heads, width):
    tiles = width // LANES
    x = x.reshape(*lead, tiles, heads, LANES)
    x = jnp.swapaxes(x, -3, -2)
    return x.reshape(*lead, heads, width)


def _side_cast_specs(weights, steps):
    in_specs, out_specs, out_shapes = [], [], []
    for w in weights:
        rows, cols = w.shape
        units = rows // BF16_SUBLANES
        assert units * BF16_SUBLANES == rows
        blocks = max(b for b in range(1, min(steps, units) + 1) if units % b == 0)
        spec = pl.BlockSpec((rows // blocks, cols), lambda i, last=blocks - 1: (jnp.minimum(i, last), 0))
        in_specs.append(spec)
        out_specs.append(spec)
        out_shapes.append(jax.ShapeDtypeStruct((rows, cols), BF16))
    return in_specs, out_specs, out_shapes


def _side_cast(src_refs, dst_refs):
    for src, dst in zip(src_refs, dst_refs):
        dst[...] = src[...].astype(BF16)


def _ffn_kernel(*refs, alpha, d_ff, with_proj, n_side):
    n_in = 9 if with_proj else 5
    side_in, side_out = refs[n_in:n_in + n_side], refs[n_in + n_side + 1:n_in + 2 * n_side + 1]
    out_ref, act_ref = refs[n_in + n_side], refs[-1]
    if with_proj:
        o_ref, xin_ref, wo_ref, lpg_ref, lpb_ref, wgu_ref, wd_ref, g_ref, b_ref = refs[:n_in]
    else:
        xin_ref, wgu_ref, wd_ref, g_ref, b_ref = refs[:n_in]
    _side_cast(side_in, side_out)
    tile = xin_ref.shape[0]
    sub = min(FFN_SUB_TILE, tile)
    subs = [slice(r * sub, (r + 1) * sub) for r in range(tile // sub)]
    def load(rows):
        x = xin_ref[rows, :]
        if with_proj:
            x = _layer_norm(alpha * x + _dot(o_ref[rows, :], wo_ref[...]), lpg_ref[...], lpb_ref[...])
        return x

    xs = [load(subs[0])]
    for r, rows in enumerate(subs):
        if r + 1 < len(subs):
            xs.append(load(subs[r + 1]))
        x = xs[r]
        xb = x.astype(BF16)
        for c in range(d_ff // FF_CHUNK):
            lo = c * FF_CHUNK
            gate = _dot(xb, wgu_ref[:, lo:lo + FF_CHUNK])
            up = _dot(xb, wgu_ref[:, d_ff + lo:d_ff + lo + FF_CHUNK])
            act_ref[rows, lo:lo + FF_CHUNK] = (gate * jax.nn.sigmoid(gate) * up).astype(BF16)
    ys = [_dot(act_ref[rows, :], wd_ref[...]) for rows in subs]
    for rows, x, y in zip(subs, xs, ys):
        out_ref[rows, :] = _layer_norm(alpha * x + 0.5 * y, g_ref[...], b_ref[...])


def _ffn(x, w_gu, w_down, ln_g, ln_b, alpha, proj=None, side=()):
    n, d = x.shape
    d_ff = w_down.shape[0]
    tile = min(ROW_TILE, n)
    side_in, side_out, side_shapes = _side_cast_specs(side, n // tile)
    in_specs, args = [], []
    if proj is not None:
        o, wo, pg, pb = proj
        in_specs += [_rows(tile, o.shape[1])]
        args += [o]
    in_specs += [_rows(tile, d)]
    args += [x]
    if proj is not None:
        in_specs += [_resident(wo.shape), _resident((1, d)), _resident((1, d))]
        args += [wo, pg, pb]
    in_specs += [_resident(w_gu.shape), _resident(w_down.shape), _resident((1, d)), _resident((1, d))]
    args += [w_gu, w_down, ln_g, ln_b]
    outs = pl.pallas_call(
        functools.partial(_ffn_kernel, alpha=alpha, d_ff=d_ff, with_proj=proj is not None, n_side=len(side)),
        grid=(n // tile,),
        in_specs=in_specs + side_in,
        out_specs=[_rows(tile, d)] + side_out,
        out_shape=[jax.ShapeDtypeStruct((n, d), F32)] + side_shapes,
        scratch_shapes=[pltpu.VMEM((tile, d_ff), BF16)],
        compiler_params=_params("parallel"),
        name="ffn_proj" if proj is not None else "ffn",
    )(*args, *side)
    return outs[0], tuple(outs[1:])


def _inproj_kernel(x_ref, w_ref, wk_ref, lng_ref, lnb_ref, ws_ref, bst_ref, q_ref, k_ref, v_ref, g_ref, *vrows_ref,
                   keys_on_lanes):
    tile = x_ref.shape[0]
    sub = min(FFN_SUB_TILE, tile)
    row = lax.broadcasted_iota(jnp.int32, (GMLP_CHUNK, GMLP_CHUNK), 0)
    col = lax.broadcasted_iota(jnp.int32, (GMLP_CHUNK, GMLP_CHUNK), 1)
    mix_w = [jnp.where(row >= col, ws_ref[g], 0.0).astype(BF16) for g in range(GMLP_GROUPS)]

    def project(r0):
        xb = x_ref[r0:r0 + sub, :].astype(BF16)
        z = _dot(xb, w_ref[:, :2 * DIFF_WIDTH])
        q_ref[r0:r0 + sub, :] = (z[:, :DIFF_WIDTH] * Q_SCALE).astype(BF16)
        if keys_on_lanes:
            k_ref[:, r0:r0 + sub] = _dot_nt(wk_ref[...], xb)
        else:
            k_ref[r0:r0 + sub, :] = _dot(xb, wk_ref[...])
        for h in range(DIFF_HEADS):
            _head_store(v_ref, h, DIFF_HEADS, z[:, DIFF_WIDTH + h * HEAD_WIDTH:DIFF_WIDTH + (h + 1) * HEAD_WIDTH], r0)

    def gate(r0, zg):
        u = jax.nn.gelu(zg[:, :GMLP_WIDTH])
        vn = _layer_norm(jax.nn.gelu(zg[:, GMLP_WIDTH:]), lng_ref[...], lnb_ref[...])
        if vrows_ref:
            for g in range(GMLP_GROUPS):
                _head_store(vrows_ref[0], g, GMLP_GROUPS, vn[:, g * GMLP_GROUP_DIM:(g + 1) * GMLP_GROUP_DIM], r0)
        vnb = vn.astype(BF16)
        for g in range(GMLP_GROUPS):
            cols = slice(g * GMLP_GROUP_DIM, (g + 1) * GMLP_GROUP_DIM)
            bias = bst_ref[:, g:g + 1]
            for c in range(sub // GMLP_CHUNK):
                rows = slice(c * GMLP_CHUNK, (c + 1) * GMLP_CHUNK)
                mixed = _dot(mix_w[g], vnb[rows, cols]) + bias
                g_ref[r0 + rows.start:r0 + rows.stop, cols] = (u[rows, cols] * mixed).astype(BF16)

    starts = list(range(0, tile, sub))
    zgs = {}
    for i, r0 in enumerate(starts):
        zgs[r0] = _dot(x_ref[r0:r0 + sub, :].astype(BF16), w_ref[:, 2 * DIFF_WIDTH:])
        if i:
            gate(starts[i - 1], zgs.pop(starts[i - 1]))
        project(r0)
    gate(starts[-1], zgs.pop(starts[-1]))


def _head_rows(tile, heads, width):
    return _rows(_head_view_rows(tile, heads, width), LANES)


def _head_shape(rows, heads, width):
    return jax.ShapeDtypeStruct((_head_view_rows(rows, heads, width), LANES), F32)


def _inproj(x, w_rest, w_keys, ln_g, ln_b, ws, bs_t, batch, seq, keys_on_lanes, want_vrows):
    n, d = x.shape
    tile = min(ROW_TILE, n)
    if keys_on_lanes:
        per = seq // tile
        k_shape = jax.ShapeDtypeStruct((batch, DIFF_WIDTH, seq), F32)
        k_spec = pl.BlockSpec((None, DIFF_WIDTH, tile), lambda i: (i // per, 0, i % per))
    else:
        k_shape = jax.ShapeDtypeStruct((n, DIFF_WIDTH), F32)
        k_spec = _rows(tile, DIFF_WIDTH)
    out_shape = [jax.ShapeDtypeStruct((n, DIFF_WIDTH), BF16), k_shape,
                 _head_shape(n, DIFF_HEADS, HEAD_WIDTH),
                 jax.ShapeDtypeStruct((n, GMLP_WIDTH), BF16)]
    out_specs = [_rows(tile, DIFF_WIDTH), k_spec, _head_rows(tile, DIFF_HEADS, HEAD_WIDTH), _rows(tile, GMLP_WIDTH)]
    if want_vrows:
        out_shape.append(_head_shape(n, GMLP_GROUPS, GMLP_GROUP_DIM))
        out_specs.append(_head_rows(tile, GMLP_GROUPS, GMLP_GROUP_DIM))
    return pl.pallas_call(
        functools.partial(_inproj_kernel, keys_on_lanes=keys_on_lanes),
        grid=(n // tile,),
        in_specs=[_rows(tile, d), _resident(w_rest.shape), _resident(w_keys.shape), _resident((1, GMLP_WIDTH)),
                  _resident((1, GMLP_WIDTH)), _resident(ws.shape), _resident(bs_t.shape)],
        out_specs=out_specs,
        out_shape=out_shape,
        compiler_params=_params("parallel"),
        name="inproj",
    )(x, w_rest, w_keys, ln_g, ln_b, ws, bs_t)


def _head_scalars(head, lq1_ref, lk1_ref, lq2_ref, lk2_ref, lam_init):
    slope = jnp.exp2(jnp.zeros((1, 1), F32) - 8.0 * (head + 1).astype(F32) / DIFF_HEADS) * LOG2E
    lam = (jnp.exp(jnp.sum(lq1_ref[...] * lk1_ref[...], axis=-1, keepdims=True))
           - jnp.exp(jnp.sum(lq2_ref[...] * lk2_ref[...], axis=-1, keepdims=True)) + lam_init)
    return slope, lam


def _bias_lanes(slope, rows):
    lane = lax.broadcasted_iota(jnp.int32, (1, LANES), 1)
    out = jnp.zeros((1, LANES), F32)
    rest = slope
    for i in range(BIAS_TERMS):
        term = rest.astype(BF16).astype(F32)
        out = jnp.where(lane == 2 * i, term * POS_SPLIT, jnp.where(lane == 2 * i + 1, term, out))
        rest = rest - term
    return jnp.broadcast_to(out, (rows, LANES)).astype(BF16)


def _position_rows(n):
    assert n <= POS_SPLIT * 256
    row = lax.broadcasted_iota(jnp.int32, (LANES, n), 0)
    pos = lax.broadcasted_iota(jnp.int32, (LANES, n), 1)
    val = jnp.where(row % 2 == 0, pos // POS_SPLIT, pos % POS_SPLIT)
    return jnp.where(row < 2 * BIAS_TERMS, val, 0).astype(F32).astype(BF16)


def _stack_maps(q):
    lane = lax.broadcasted_iota(jnp.int32, q.shape, 1)
    zero = jnp.zeros_like(q)
    return jnp.concatenate([jnp.where(lane < HEAD_DIM, q, zero), jnp.where(lane >= HEAD_DIM, q, zero)], axis=0)


def _near_bias(slope, t, base_q, base_k, nk):
    r = lax.broadcasted_iota(jnp.int32, (t, nk), 0) + base_q
    c = lax.broadcasted_iota(jnp.int32, (t, nk), 1) + base_k
    bias = slope * (r - jnp.abs(r - c)).astype(F32)
    return jnp.where(c // CHUNK <= r // CHUNK, bias, MASKED)


def _diff_combine(parts, values, t, lam):
    return _weighted_values(_softmax_weights(parts), values, t, lam)


def _softmax_weights(parts):
    m = functools.reduce(jnp.maximum, [jnp.max(s, axis=-1, keepdims=True) for s in parts])
    return [jnp.exp2(s - m).astype(BF16) for s in parts]


def _weighted_values(weights, values, t, lam):
    res = functools.reduce(jnp.add, [_dot(p, v) for p, v in zip(weights, values)])
    out, denom = res[:, :HEAD_WIDTH], res[:, HEAD_WIDTH:HEAD_WIDTH + 1]
    return out[:t] * (1.0 / denom[:t]) - out[t:] * (lam / denom[t:])


def _with_ones(v):
    lane = lax.broadcasted_iota(jnp.int32, v.shape, 1)
    return jnp.concatenate([v, jnp.where(lane == 0, 1.0, 0.0).astype(v.dtype)], axis=1)


def _sub_norm(o, g, lam_init):
    return o * lax.rsqrt(jnp.mean(o * o, axis=-1, keepdims=True) + LN_EPS) * g * (1.0 - lam_init)


def _attn_prompt_kernel(q_ref, k_ref, v_ref, lq1_ref, lk1_ref, lq2_ref, lk2_ref, sg_ref, o_ref,
                        kb_ref, vb_ref, *, lam_init, group):
    seq = q_ref.shape[0]
    tq = ATTN_Q_TILE
    step = pl.program_id(1)

    @pl.when((pl.program_id(0) == 0) & (step == 0))
    def _():
        for j in range(group):
            kb_ref[j, HEAD_WIDTH:, :] = _position_rows(seq)
            vb_ref[j] = _with_ones(jnp.zeros((seq, HEAD_WIDTH), BF16))

    for j in range(group):
        kb_ref[j, :HEAD_WIDTH, :] = k_ref[j * HEAD_WIDTH:(j + 1) * HEAD_WIDTH, :].astype(BF16)
    for g in range(DIFF_HEADS // group):
        @pl.when(step == g)
        def _():
            for j in range(group):
                vb_ref[j, :, :HEAD_WIDTH] = _head_load(v_ref, g * group + j, DIFF_HEADS, seq, HEAD_WIDTH).astype(BF16)

    r = lax.broadcasted_iota(jnp.int32, (tq, tq), 0)
    c = lax.broadcasted_iota(jnp.int32, (tq, tq), 1)
    allowed = c // CHUNK <= r // CHUNK
    lead = -2.0 * jnp.maximum(c - r, 0).astype(F32)
    q_bias, diag = [], []
    for j in range(group):
        slope, lam = _head_scalars(step * group + j, lq1_ref, lk1_ref, lq2_ref, lk2_ref, lam_init)
        q_bias.append(_bias_lanes(slope, 2 * tq))
        d = jnp.where(allowed, slope * lead, MASKED)
        diag.append(jnp.concatenate([d, d], axis=0))

    def scores(j, lo):
        cols = slice(j * HEAD_WIDTH, (j + 1) * HEAD_WIDTH)
        qs = jnp.concatenate([_stack_maps(q_ref[lo:lo + tq, cols]), q_bias[j]], axis=1)
        parts = [_dot(qs, kb_ref[j, :, lo:lo + tq]) + diag[j]]
        if lo:
            parts.append(_dot(qs, kb_ref[j, :, 0:lo]))
        return parts

    def finish(j, lo, weights):
        values = [vb_ref[j, lo:lo + tq, :]] + ([vb_ref[j, 0:lo, :]] if lo else [])
        out = _weighted_values(weights, values, tq, lam)
        o_ref[lo:lo + tq, j * HEAD_WIDTH:(j + 1) * HEAD_WIDTH] = _sub_norm(out, sg_ref[...], lam_init).astype(BF16)

    chains = [(j, lo) for lo in range(0, seq, tq) for j in range(group)]
    parts = scores(*chains[0])
    for i, chain in enumerate(chains):
        next_parts = scores(*chains[i + 1]) if i + 1 < len(chains) else None
        finish(*chain, _softmax_weights(parts))
        parts = next_parts


def _attn_prompt(q, k_t, v, lam_vecs, subln_g, batch, seq, lam_init):
    group = ATTN_HEADS_PER_STEP
    blk = pl.BlockSpec((seq, group * HEAD_WIDTH), lambda b, g: (b, g))
    vec = pl.BlockSpec((1, HEAD_DIM), lambda b, g: (0, 0))
    return pl.pallas_call(
        functools.partial(_attn_prompt_kernel, lam_init=lam_init, group=group),
        grid=(batch, DIFF_HEADS // group),
        in_specs=[blk, pl.BlockSpec((None, group * HEAD_WIDTH, seq), lambda b, g: (b, g, 0)),
                  pl.BlockSpec((_head_view_rows(seq, DIFF_HEADS, HEAD_WIDTH), LANES), lambda b, g: (b, 0)),
                  vec, vec, vec, vec, pl.BlockSpec((1, HEAD_WIDTH), lambda b, g: (0, 0))],
        out_specs=blk,
        out_shape=jax.ShapeDtypeStruct((batch * seq, DIFF_WIDTH), BF16),
        scratch_shapes=[pltpu.VMEM((group, HEAD_WIDTH + LANES, seq), BF16),
                        pltpu.VMEM((group, seq, 2 * HEAD_WIDTH), BF16)],
        compiler_params=_params("arbitrary", "arbitrary"),
        name="attn_prompt",
    )(q, k_t, v, *lam_vecs, subln_g)


def _attn_sample_kernel(q_ref, kn_ref, vn_ref, kc_ref, vc_ref, lq1_ref, lk1_ref, lq2_ref, lk2_ref, sg_ref, o_ref,
                        *, lam_init):
    t = q_ref.shape[0]
    past = kc_ref.shape[1]
    far_pos = lax.broadcasted_iota(jnp.int32, (1, past), 1).astype(F32)
    for h in range(DIFF_HEADS):
        cols = slice(h * HEAD_WIDTH, (h + 1) * HEAD_WIDTH)
        slope, lam = _head_scalars(jnp.int32(h), lq1_ref, lk1_ref, lq2_ref, lk2_ref, lam_init)
        qs = _stack_maps(q_ref[:, cols])
        near = _near_bias(slope, t, past, past, t)
        parts = [_dot_nt(qs, kn_ref[:, cols].astype(BF16)) + jnp.concatenate([near, near], axis=0),
                 _dot(qs, kc_ref[cols, :].astype(BF16)) + slope * far_pos]
        values = [_with_ones(_head_load(vn_ref, h, DIFF_HEADS, t, HEAD_WIDTH).astype(BF16)),
                  _with_ones(_head_load(vc_ref, h, DIFF_HEADS, past, HEAD_WIDTH).astype(BF16))]
        out = _diff_combine(parts, values, t, lam)
        o_ref[:, cols] = _sub_norm(out, sg_ref[...], lam_init).astype(BF16)


def _attn_sample(q, k_new, v_new, k_cache_t, v_cache, lam_vecs, subln_g, batch, seq, lam_init):
    past = k_cache_t.shape[2]
    new = pl.BlockSpec((seq, DIFF_WIDTH), lambda b: (b, 0))
    vec = pl.BlockSpec((1, HEAD_DIM), lambda b: (0, 0))
    return pl.pallas_call(
        functools.partial(_attn_sample_kernel, lam_init=lam_init),
        grid=(batch,),
        in_specs=[new, new, _head_rows(seq, DIFF_HEADS, HEAD_WIDTH),
                  pl.BlockSpec((None, DIFF_WIDTH, past), lambda b: (b, 0, 0)),
                  _head_rows(past, DIFF_HEADS, HEAD_WIDTH),
                  vec, vec, vec, vec, pl.BlockSpec((1, HEAD_WIDTH), lambda b: (0, 0))],
        out_specs=new,
        out_shape=jax.ShapeDtypeStruct((batch * seq, DIFF_WIDTH), BF16),
        compiler_params=_params("parallel"),
        name="attn_sample",
    )(q, k_new, v_new, k_cache_t, v_cache, *lam_vecs, subln_g)


def _outproj_kernel(a_ref, g_ref, x_ref, wo_ref, lg_ref, lb_ref, wq_ref, x2_ref, q2_ref, *, alpha):
    tile = x_ref.shape[0]
    sub = min(FFN_SUB_TILE, tile)
    subs = [slice(r * sub, (r + 1) * sub) for r in range(tile // sub)]
    mixes = [_dot(jnp.concatenate([a_ref[rows, :], g_ref[rows, :]], axis=1), wo_ref[...]) for rows in subs]
    for rows, mix in zip(subs, mixes):
        x2 = _layer_norm(alpha * x_ref[rows, :] + mix, lg_ref[...], lb_ref[...])
        x2_ref[rows, :] = x2
        q2_ref[rows, :] = _dot(x2.astype(BF16), wq_ref[...]).astype(BF16)


def _outproj(a, g, x, w_out, ln_g, ln_b, wq, alpha):
    n, d = x.shape
    tile = min(ROW_TILE, n)
    return pl.pallas_call(
        functools.partial(_outproj_kernel, alpha=alpha),
        grid=(n // tile,),
        in_specs=[_rows(tile, a.shape[1]), _rows(tile, g.shape[1]), _rows(tile, d), _resident(w_out.shape),
                  _resident((1, d)), _resident((1, d)), _resident(wq.shape)],
        out_specs=[_rows(tile, d), _rows(tile, d)],
        out_shape=[jax.ShapeDtypeStruct((n, d), F32), jax.ShapeDtypeStruct((n, d), BF16)],
        compiler_params=_params("parallel"),
        name="outproj",
    )(a, g, x, w_out, ln_g, ln_b, wq)


def _cross_kernel(q_ref, mk_ref, mv_ref, o_ref, *, n_mem, seq_rows, sub):
    hd = q_ref.shape[1] // MEM_HEADS
    log2_scale = hd ** -0.5 * LOG2E
    chains = []
    for b in range(q_ref.shape[0] // seq_rows):
        for h in range(MEM_HEADS):
            mk = _head_load(mk_ref, h, MEM_HEADS, n_mem, hd, b * n_mem).astype(BF16)
            mv = _head_load(mv_ref, h, MEM_HEADS, n_mem, hd, b * n_mem).astype(BF16)
            for r0 in range(b * seq_rows, (b + 1) * seq_rows, sub):
                chains.append((slice(r0, r0 + sub), slice(h * hd, (h + 1) * hd), mk, mv))

    def scores(chain):
        rows, cols, mk, _ = chain
        return _dot_nt(q_ref[rows, cols], mk)

    s = scores(chains[0])
    for i, (rows, cols, _, mv) in enumerate(chains):
        s_next = scores(chains[i + 1]) if i + 1 < len(chains) else None
        p = jnp.exp2((s - jnp.max(s, axis=-1, keepdims=True)) * log2_scale)
        o = _dot(p.astype(BF16), mv)
        o_ref[rows, cols] = (o * (1.0 / jnp.sum(p, axis=-1, keepdims=True))).astype(BF16)
        s = s_next


def _cross(q2, mem_k, mem_v, batch, seq, n_mem):
    d = q2.shape[1]
    per_step = max(1, CROSS_STEP_ROWS // seq)
    qblk = pl.BlockSpec((per_step * seq, d), lambda i: (i, 0))
    mblk = pl.BlockSpec((per_step * _head_view_rows(n_mem, MEM_HEADS, d // MEM_HEADS), LANES), lambda i: (i, 0))
    return pl.pallas_call(
        functools.partial(_cross_kernel, n_mem=n_mem, seq_rows=seq, sub=min(CROSS_SUB_TILE, seq)),
        grid=(batch // per_step,),
        in_specs=[qblk, mblk, mblk],
        out_specs=qblk,
        out_shape=jax.ShapeDtypeStruct(q2.shape, BF16),
        compiler_params=_params("parallel"),
        name="cross",
    )(q2, mem_k, mem_v)


def _memkv_kernel(*refs, n_side):
    m_ref, wk_ref, wv_ref = refs[:3]
    side_in, (k_ref, v_ref), side_out = refs[3:3 + n_side], refs[3 + n_side:5 + n_side], refs[5 + n_side:]
    _side_cast(side_in, side_out)
    mb = m_ref[...].astype(BF16)
    k = _dot(mb, wk_ref[...].astype(BF16))
    v = _dot(mb, wv_ref[...].astype(BF16))
    hd = k.shape[1] // MEM_HEADS
    for h in range(MEM_HEADS):
        _head_store(k_ref, h, MEM_HEADS, k[:, h * hd:(h + 1) * hd])
        _head_store(v_ref, h, MEM_HEADS, v[:, h * hd:(h + 1) * hd])


def _memkv(mem, wk, wv, side=()):
    n, d = mem.shape
    hd = d // MEM_HEADS
    tile = min(MEMKV_TILE, n)
    side_in, side_out, side_shapes = _side_cast_specs(side, n // tile)
    outs = pl.pallas_call(
        functools.partial(_memkv_kernel, n_side=len(side)),
        grid=(n // tile,),
        in_specs=[_rows(tile, d), _resident(wk.shape), _resident(wv.shape)] + side_in,
        out_specs=[_head_rows(tile, MEM_HEADS, hd)] * 2 + side_out,
        out_shape=[_head_shape(n, MEM_HEADS, hd)] * 2 + side_shapes,
        compiler_params=_params("parallel"),
        name="memkv",
    )(mem, wk, wv, *side)
    return outs[0], outs[1], tuple(outs[2:])


def _gating_weights(ws, bs, seq):
    n = min(seq, GMLP_CHUNK)
    rep = GMLP_CHUNK // n
    w = ws[:, :n, :n]
    if rep > 1:
        eye = jnp.eye(rep, dtype=ws.dtype)
        w = jnp.einsum('ab,gts->gatbs', eye, w).reshape(GMLP_GROUPS, GMLP_CHUNK, GMLP_CHUNK)
    return w, jnp.tile(bs[:, :n], (1, rep)).T


def _trunk(x, batch, seq, mem_k, mem_v, cache, p, layer_idx, alpha, want_vrows):
    lam_init = 0.8 - 0.6 * math.exp(-0.3 * layer_idx)
    ws, bs_t = _gating_weights(p['gmlp_ws'], p['gmlp_bs'], seq)
    lam_vecs = (p['lambda_q1'], p['lambda_k1'], p['lambda_q2'], p['lambda_k2'])

    if 'ffn2_w_gu' in p:
        x1, _ = _ffn(x, p['ffn1_w_gu'], p['ffn1_w_down'], p['ln1_g'], p['ln1_b'], alpha)
    else:
        x1, (p['ffn2_w_gu'], p['ffn2_w_down']) = _ffn(x, p['ffn1_w_gu'], p['ffn1_w_down'], p['ln1_g'], p['ln1_b'],
                                                      alpha, side=(p['ffn2_w_gu_f32'], p['ffn2_w_down_f32']))
    keys_on_lanes = cache is None
    outs = _inproj(x1, p['w_in_rest'], p['w_in_keys_t'] if keys_on_lanes else p['w_in_keys'], p['gmlp_ln_g'],
                   p['gmlp_ln_b'], ws, bs_t, batch, seq, keys_on_lanes, want_vrows)
    q, k, v, gated = outs[:4]
    if cache is None:
        a = _attn_prompt(q, k, v, lam_vecs, p['subln_g'], batch, seq, lam_init)
    else:
        a = _attn_sample(q, k, v, cache[0], cache[1], lam_vecs, p['subln_g'], batch, seq, lam_init)
    x2, q2 = _outproj(a, gated, x1, p['w_out'], p['ln2_g'], p['ln2_b'], p['cross_wq'], alpha)
    n_mem = mem_k.size // (batch * x.shape[1])
    o = _cross(q2, mem_k, mem_v, batch, seq, n_mem)
    y, _ = _ffn(x2, p['ffn2_w_gu'], p['ffn2_w_down'], p['ln4_g'], p['ln4_b'], alpha,
                proj=(o, p['cross_wo'], p['ln3_g'], p['ln3_b']))
    return y, k, v, (outs[4] if want_vrows else None)


_MATRICES = ('w_in', 'w_out', 'cross_wq', 'cross_wo')
_SIDE_CAST_MATRICES = ('ffn1_w_gu', 'ffn1_w_down', 'ffn2_w_gu', 'ffn2_w_down')
_ROW_VECTORS = ('ln1_g', 'ln1_b', 'lambda_q1', 'lambda_k1', 'lambda_q2', 'lambda_k2', 'subln_g', 'gmlp_ln_g',
                'gmlp_ln_b', 'ln2_g', 'ln2_b', 'ln3_g', 'ln3_b', 'ln4_g', 'ln4_b')


def kernel(x_prompt, x_sample, cache_k, cache_v, cache_mem_k, cache_mem_v, mem_prompt, ffn1_w_gu, ffn1_w_down, ln1_g, ln1_b, w_in, lambda_q1, lambda_k1, lambda_q2, lambda_k2, subln_g, gmlp_ln_g, gmlp_ln_b, gmlp_ws, gmlp_bs, w_out, ln2_g, ln2_b, cross_wq, cross_wk, cross_wv, cross_wo, ln3_g, ln3_b, ffn2_w_gu, ffn2_w_down, ln4_g, ln4_b):
    weights = dict(ffn1_w_gu=ffn1_w_gu, ffn1_w_down=ffn1_w_down, ln1_g=ln1_g, ln1_b=ln1_b, w_in=w_in,
                   lambda_q1=lambda_q1, lambda_k1=lambda_k1, lambda_q2=lambda_q2, lambda_k2=lambda_k2,
                   subln_g=subln_g, gmlp_ln_g=gmlp_ln_g, gmlp_ln_b=gmlp_ln_b, gmlp_ws=gmlp_ws, gmlp_bs=gmlp_bs,
                   w_out=w_out, ln2_g=ln2_g, ln2_b=ln2_b, cross_wq=cross_wq, cross_wk=cross_wk, cross_wv=cross_wv,
                   cross_wo=cross_wo, ln3_g=ln3_g, ln3_b=ln3_b, ffn2_w_gu=ffn2_w_gu, ffn2_w_down=ffn2_w_down,
                   ln4_g=ln4_g, ln4_b=ln4_b)
    depth = w_in.shape[0]
    alpha = (2 * depth) ** 0.25
    bp, tp, d = x_prompt.shape
    bs_, ts, _ = x_sample.shape
    n_mem = mem_prompt.shape[1]
    past = cache_k.shape[2]
    mem_hd = d // MEM_HEADS

    xp =x_prompt.reshape(bp * tp, d)
    xs = x_sample.reshape(bs_ * ts, d)
    mem2d = mem_prompt.reshape(bp * n_mem, d)
    per_layer = [[] for _ in range(7)]
    for l in range(depth):
        p = {}
        for name, w in weights.items():
            w = w[l]
            if name in _MATRICES:
                w = w.astype(BF16)
            elif name in _SIDE_CAST_MATRICES:
                name = name + '_f32'
            elif name in _ROW_VECTORS:
                w = w.reshape(1, -1)
            p[name] = w
        w_in_l = p.pop('w_in')
        p['w_in_rest'] = jnp.concatenate([w_in_l[:, :DIFF_WIDTH], w_in_l[:, 2 * DIFF_WIDTH:]], axis=1)
        p['w_in_keys'] = w_in_l[:, DIFF_WIDTH:2 * DIFF_WIDTH]
        p['w_in_keys_t'] = p['w_in_keys'].T
        mem_k, mem_v, (p['ffn1_w_gu'], p['ffn1_w_down']) = _memkv(
            mem2d, p['cross_wk'], p['cross_wv'], side=(p['ffn1_w_gu_f32'], p['ffn1_w_down_f32']))
        xp, kp_t, vp, _ = _trunk(xp, bp, tp, mem_k, mem_v, None, p, l, alpha, False)
        cache = (jnp.transpose(cache_k[l], (0, 2, 3, 4, 1)).reshape(bs_, DIFF_WIDTH, past),
                 _to_head_view(cache_v[l]))
        xs, ks, vs, gvs = _trunk(xs, bs_, ts, _to_head_view(cache_mem_k[l]), _to_head_view(cache_mem_v[l]),
                                 cache, p, l, alpha, True)
        kp = jnp.transpose(kp_t.reshape(bp, DIFF_HEADS, 2, HEAD_DIM, tp), (0, 4, 1, 2, 3))
        vals = (kp, _from_head_view(vp, (bp, tp), DIFF_HEADS, HEAD_WIDTH),
                _from_head_view(mem_k, (bp, n_mem), MEM_HEADS, mem_hd),
                _from_head_view(mem_v, (bp, n_mem), MEM_HEADS, mem_hd), ks,
                _from_head_view(vs, (bs_, ts), DIFF_HEADS, HEAD_WIDTH),
                _from_head_view(gvs, (bs_, ts), GMLP_GROUPS, GMLP_GROUP_DIM))
        for acc, val in zip(per_layer, vals):
            acc.append(val)

    kp, vp, mk, mv, ks, vs, gvs = (vals[0][None] if depth == 1 else jnp.stack(vals) for vals in per_layer)
    return (xp.reshape(bp, tp, d), xs.reshape(bs_, ts, d),
            kp.reshape(depth, bp, tp, DIFF_HEADS, 2, HEAD_DIM),
            vp.reshape(depth, bp, tp, DIFF_HEADS, HEAD_WIDTH),
            mk.reshape(depth, bp, n_mem, MEM_HEADS, mem_hd),
            mv.reshape(depth, bp, n_mem, MEM_HEADS, mem_hd),
            ks.reshape(depth, bs_, ts, DIFF_HEADS, 2, HEAD_DIM),
            vs.reshape(depth, bs_, ts, DIFF_HEADS, HEAD_WIDTH),
            gvs.reshape(depth, bs_, ts, GMLP_GROUPS, GMLP_GROUP_DIM))
```
